```python
import jax, jax.numpy as jnp
from jax import lax
import numpy as np

D_MODEL = 1024
BATCH = 8
SEQ = 4096
DEPTH = 4

N_SELF = DEPTH // 2
N_CROSS = DEPTH - N_SELF

D_FF = 2816

RET_HEADS = 8
RET_QK_DIM = D_MODEL // RET_HEADS
RET_V_DIM = 2 * D_MODEL // RET_HEADS
RET_CHUNK = 128
RET_PROJ = 2 * RET_HEADS * RET_QK_DIM + 2 * RET_HEADS * RET_V_DIM

MLA_HEADS = 8
MLA_NOPE = 128
MLA_ROPE = 64
MLA_V = 128
Q_LORA = 384
KV_LORA = 256
ATTN_BLOCK = 128

ROPE_BASE = 10000.0
EPS = 1e-6

kernel_name = "yoco_retnet_mla_macaron"


def rmsnorm(x, g):
    xf = x.astype(jnp.float32)
    y = xf * lax.rsqrt(jnp.mean(xf * xf, axis=-1, keepdims=True) + EPS)
    return (y * g.astype(jnp.float32)).astype(x.dtype)


def swiglu(x, w_gate, w_up, w_down):
    return (jax.nn.silu(x @ w_gate) * (x @ w_up)) @ w_down


def rope(x, positions):
    d = x.shape[-1]
    inv_freq = ROPE_BASE ** (-jnp.arange(0, d, 2, dtype=jnp.float32) / d)
    ang = positions.astype(jnp.float32)[..., None] * inv_freq
    cos = jnp.cos(ang)[:, :, None, :].astype(x.dtype)
    sin = jnp.sin(ang)[:, :, None, :].astype(x.dtype)
    x1, x2 = jnp.split(x, 2, axis=-1)
    return jnp.concatenate([x1 * cos - x2 * sin, x2 * cos + x1 * sin], axis=-1)


def retention(x, positions, w_in, gn_g, w_o):
    B, S, _ = x.shape
    H, dk, dv, C = RET_HEADS, RET_QK_DIM, RET_V_DIM, RET_CHUNK
    N = S // C
    proj = x @ w_in
    q, k, v, g = jnp.split(proj, [H * dk, 2 * H * dk, 2 * H * dk + H * dv], axis=-1)
    q = rope(q.reshape(B, S, H, dk), positions)
    k = rope(k.reshape(B, S, H, dk), positions) * (dk ** -0.5)
    v = v.reshape(B, S, H, dv)

    lg = jnp.log1p(-jnp.exp2(-5.0 - jnp.arange(H, dtype=jnp.float32)))
    idx = jnp.arange(C, dtype=jnp.float32)
    diff = idx[:, None] - idx[None, :]
    d_intra = jnp.where(diff >= 0, jnp.exp(lg[:, None, None] * jnp.maximum(diff, 0.0)), 0.0).astype(x.dtype)
    q_decay = jnp.exp(lg[:, None] * (idx[None, :] + 1.0)).astype(x.dtype)
    k_decay = jnp.exp(lg[:, None] * (C - 1.0 - idx[None, :])).astype(x.dtype)
    chunk_decay = jnp.exp(lg * C).astype(x.dtype)

    to_chunks = lambda t: t.reshape(B, N, C, H, t.shape[-1]).transpose(1, 0, 3, 2, 4)
    qc, kc, vc = to_chunks(q), to_chunks(k), to_chunks(v)

    scores = jnp.einsum('nbhid,nbhjd->nbhij', qc, kc) * d_intra[None, None]
    inner = jnp.einsum('nbhij,nbhjv->nbhiv', scores, vc)

    def step(state, inp):
        q_t, k_t, v_t = inp
        cross = jnp.einsum('bhcd,bhdv->bhcv', q_t * q_decay[None, :, :, None], state)
        state = state * chunk_decay[None, :, None, None] + jnp.einsum(
            'bhcd,bhcv->bhdv', k_t * k_decay[None, :, :, None], v_t)
        return state, cross

    state0 = jnp.zeros((B, H, dk, dv), dtype=qc.dtype)
    _, cross = lax.scan(step, state0, (qc, kc, vc))
    o = (inner + cross).transpose(1, 0, 3, 2, 4).reshape(B, S, H, dv)

    of = o.astype(jnp.float32)
    mu = jnp.mean(of, axis=-1, keepdims=True)
    var = jnp.mean(jnp.square(of - mu), axis=-1, keepdims=True)
    o = ((of - mu) * lax.rsqrt(var + EPS) * gn_g.astype(jnp.float32)).astype(x.dtype)
    o = o.reshape(B, S, H * dv)
    return (jax.nn.silu(g) * o) @ w_o


def mla_shared_kv(h, positions, kv_norm_g, kv_w_down, kv_latent_norm_g, kv_w_up,
                  k_nope_norm_g, k_rope_norm_g):
    B, S, _ = h.shape
    hn = rmsnorm(h, kv_norm_g)
    c_kv, k_pe = jnp.split(hn @ kv_w_down, [KV_LORA], axis=-1)
    c_kv = rmsnorm(c_kv, kv_latent_norm_g)
    kv = (c_kv @ kv_w_up).reshape(B, S, MLA_HEADS, MLA_NOPE + MLA_V)
    k_nope, v = jnp.split(kv, [MLA_NOPE], axis=-1)
    k_nope = rmsnorm(k_nope, k_nope_norm_g)
    k_pe = rope(rmsnorm(k_pe, k_rope_norm_g)[:, :, None, :], positions)
    k = jnp.concatenate([k_nope, jnp.broadcast_to(k_pe, (B, S, MLA_HEADS, MLA_ROPE))], axis=-1)
    return k, v


def mla_attend(x, positions, k, v, w_dq, q_lora_norm_g, w_uq, q_nope_norm_g, q_rope_norm_g, w_o):
    B, S, _ = x.shape
    H = MLA_HEADS
    cq = rmsnorm(x @ w_dq, q_lora_norm_g)
    q = (cq @ w_uq).reshape(B, S, H, MLA_NOPE + MLA_ROPE)
    q_nope, q_pe = jnp.split(q, [MLA_NOPE], axis=-1)
    q = jnp.concatenate([rmsnorm(q_nope, q_nope_norm_g),
                         rope(rmsnorm(q_pe, q_rope_norm_g), positions)], axis=-1)
    scale = (MLA_NOPE + MLA_ROPE) ** -0.5
    nb = S // ATTN_BLOCK
    q_blocks = q.reshape(B, nb, ATTN_BLOCK, H, MLA_NOPE + MLA_ROPE).transpose(1, 0, 2, 3, 4)
    key_pos = jnp.arange(S)

    def block(args):
        qb, blk = args
        s = jnp.einsum('bqhd,bkhd->bhqk', qb, k).astype(jnp.float32) * scale
        q_pos = blk * ATTN_BLOCK + jnp.arange(ATTN_BLOCK)
        mask = key_pos[None, :] <= q_pos[:, None]
        p = jax.nn.softmax(jnp.where(mask[None, None], s, -jnp.inf), axis=-1)
        return jnp.einsum('bhqk,bkhd->bqhd', p.astype(v.dtype), v)

    o = lax.map(block, (q_blocks, jnp.arange(nb)))
    o = o.transpose(1, 0, 2, 3, 4).reshape(B, S, H * MLA_V)
    return o @ w_o


def setup_inputs(seed: int = 0) -> dict:
    key = jax.random.key(seed)
    ks = iter(jax.random.split(key, 32))
    f32 = jnp.float32

    def w(shape, fan_in):
        return jax.random.normal(next(ks), shape, f32) * (fan_in ** -0.5)

    def gain(shape):
        return 1.0 + 0.02 * jax.random.normal(next(ks), shape, f32)

    x = jax.random.normal(next(ks), (BATCH, SEQ, D_MODEL), f32)
    positions = jnp.broadcast_to(jnp.arange(SEQ, dtype=jnp.int32)[None, :], (BATCH, SEQ))
    return {
        "x": x,
        "positions": positions,
        "norm_g": gain((DEPTH, 3, D_MODEL)),
        "ffn_w_gate": w((DEPTH, 2, D_MODEL, D_FF), D_MODEL),
        "ffn_w_up": w((DEPTH, 2, D_MODEL, D_FF), D_MODEL),
        "ffn_w_down": w((DEPTH, 2, D_FF, D_MODEL), D_FF),
        "ret_w_in": w((N_SELF, D_MODEL, RET_PROJ), D_MODEL),
        "ret_gn_g": gain((N_SELF, RET_HEADS, RET_V_DIM)),
        "ret_w_o": w((N_SELF, RET_HEADS * RET_V_DIM, D_MODEL), RET_HEADS * RET_V_DIM),
        "kv_norm_g": gain((D_MODEL,)),
        "kv_w_down": w((D_MODEL, KV_LORA + MLA_ROPE), D_MODEL),
        "kv_latent_norm_g": gain((KV_LORA,)),
        "kv_w_up": w((KV_LORA, MLA_HEADS * (MLA_NOPE + MLA_V)), KV_LORA),
        "k_nope_norm_g": gain((MLA_NOPE,)),
        "k_rope_norm_g": gain((MLA_ROPE,)),
        "mla_w_dq": w((N_CROSS, D_MODEL, Q_LORA), D_MODEL),
        "mla_q_lora_norm_g": gain((N_CROSS, Q_LORA)),
        "mla_w_uq": w((N_CROSS, Q_LORA, MLA_HEADS * (MLA_NOPE + MLA_ROPE)), Q_LORA),
        "mla_q_nope_norm_g": gain((N_CROSS, MLA_NOPE)),
        "mla_q_rope_norm_g": gain((N_CROSS, MLA_ROPE)),
        "mla_w_o": w((N_CROSS, MLA_HEADS * MLA_V, D_MODEL), MLA_HEADS * MLA_V),
    }


def reference(x, positions, norm_g, ffn_w_gate, ffn_w_up, ffn_w_down,
              ret_w_in, ret_gn_g, ret_w_o,
              kv_norm_g, kv_w_down, kv_latent_norm_g, kv_w_up, k_nope_norm_g, k_rope_norm_g,
              mla_w_dq, mla_q_lora_norm_g, mla_w_uq, mla_q_nope_norm_g, mla_q_rope_norm_g, mla_w_o):
    k_shared = None
    v_shared = None
    for layer in range(DEPTH):
        x = x + 0.5 * swiglu(rmsnorm(x, norm_g[layer, 0]), ffn_w_gate[layer, 0],
                             ffn_w_up[layer, 0], ffn_w_down[layer, 0])
        h = rmsnorm(x, norm_g[layer, 1])
        if layer < N_SELF:
            x = x + retention(h, positions, ret_w_in[layer], ret_gn_g[layer], ret_w_o[layer])
        else:
            j = layer - N_SELF
            x = x + mla_attend(h, positions, k_shared, v_shared, mla_w_dq[j], mla_q_lora_norm_g[j],
                               mla_w_uq[j], mla_q_nope_norm_g[j], mla_q_rope_norm_g[j], mla_w_o[j])
        x = x + 0.5 * swiglu(rmsnorm(x, norm_g[layer, 2]), ffn_w_gate[layer, 1],
                             ffn_w_up[layer, 1], ffn_w_down[layer, 1])
        if layer == N_SELF - 1:
            k_shared, v_shared = mla_shared_kv(x, positions, kv_norm_g, kv_w_down, kv_latent_norm_g,
                                               kv_w_up, k_nope_norm_g, k_rope_norm_g)
    return x
```

```python
import functools
import math

import jax
import jax.numpy as jnp
from jax import lax
from jax.experimental import pallas as pl
from jax.experimental.pallas import tpu as pltpu

F32 = jnp.float32
BF16 = jnp.bfloat16

LANES = 128

D_MODEL = 1024
D_FF = 2816
HEADS = 8
RET_DK = 128
RET_DV = 256
RET_QK = HEADS * RET_DK
RET_PROJ = 2 * RET_QK + 2 * HEADS * RET_DV
MLA_NOPE = 128
MLA_ROPE = 64
MLA_V = 128
MLA_QK_PAD = 2 * LANES
Q_LORA = 384
KV_LORA = 256
ROPE_BASE = 10000.0
EPS = 1e-6

VMEM_LIMIT = 56 * 1024 * 1024

TM = 512
FF_CHUNK = D_FF // 2
RET_CHUNK = 256
TQ = 512
NEG_BIG = -1e30


def _dot(a, b):
    return jnp.dot(a, b, preferred_element_type=F32)


def _dot_nt(a, b):
    return lax.dot_general(a, b, (((1,), (1,)), ((), ())), preferred_element_type=F32)


def _dot_tn(a, b):
    return lax.dot_general(a, b, (((0,), (0,)), ((), ())), preferred_element_type=F32)


def _rms(x, g, n):
    ms = jnp.sum(x * x, axis=-1, keepdims=True) * (1.0 / n)
    return x * lax.rsqrt(ms + EPS) * g


def _params(*sem):
    return pltpu.CompilerParams(dimension_semantics=sem, vmem_limit_bytes=VMEM_LIMIT)


def _resident(shape):
    nd = len(shape)
    return pl.BlockSpec(shape, lambda *_: (0,) * nd, pipeline_mode=pl.Buffered(1))


def _rope_tables_kernel(pos_ref, rcos_ref, rsin_ref, mcos_ref, msa_ref, msb_ref):
    pos = pos_ref[...].astype(F32)
    lane = lax.broadcasted_iota(jnp.int32, (1, LANES), 1)
    inv = jnp.exp((lane & 63).astype(F32) * (-2.0 / RET_DK * math.log(ROPE_BASE)))
    ang = pos * inv
    s = jnp.sin(ang)
    rcos_ref[...] = jnp.cos(ang)
    rsin_ref[...] = jnp.where(lane < 64, -s, s)
    inv = jnp.exp((lane & 31).astype(F32) * (-2.0 / MLA_ROPE * math.log(ROPE_BASE)))
    ang = pos * inv
    s = jnp.sin(ang)
    mcos_ref[...] = jnp.where(lane < 64, jnp.cos(ang), 0.0)
    msa_ref[...] = jnp.where(lane < 32, -s, 0.0)
    msb_ref[...] = jnp.where((lane >= 32) & (lane < 64), s, 0.0)


def _rope_tables(pos):
    m = pos.shape[0]
    tm = 1024
    tab = jax.ShapeDtypeStruct((m, LANES), F32)
    spec = pl.BlockSpec((tm, LANES), lambda i: (i, 0))
    return pl.pallas_call(
        _rope_tables_kernel,
        out_shape=(tab,) * 5,
        grid=(m // tm,),
        in_specs=[pl.BlockSpec((tm, 1), lambda i: (i, 0))],
        out_specs=(spec,) * 5,
        compiler_params=_params("parallel"),
        name="rope_tables",
    )(pos)


def _rope128(x, cos, sin):
    return x * cos + pltpu.roll(x, 64, 1) * sin


def _rope64(x, cos, sa, sb):
    return x * cos + pltpu.roll(x, 96, 1) * sa + pltpu.roll(x, 32, 1) * sb


def _ffn_kernel(x_ref, g_ref, wg_ref, wu_ref, wd_ref, o_ref):
    x = x_ref[...]
    h = _rms(x, g_ref[...], D_MODEL).astype(BF16)
    acc = None
    for c in range(D_FF // FF_CHUNK):
        sl = slice(c * FF_CHUNK, (c + 1) * FF_CHUNK)
        gate = _dot(h, wg_ref[:, sl])
        up = _dot(h, wu_ref[:, sl])
        a = (gate / (1.0 + jnp.exp(-gate)) * up).astype(BF16)
        d = _dot(a, wd_ref[sl, :])
        acc = d if acc is None else acc + d
    o_ref[...] = x + 0.5 * acc


def _ffn(x, g, wg, wu, wd):
    m = x.shape[0]
    row = pl.BlockSpec((TM, D_MODEL), lambda i: (i, 0))
    return pl.pallas_call(
        _ffn_kernel,
        out_shape=jax.ShapeDtypeStruct((m, D_MODEL), F32),
        grid=(m // TM,),
        in_specs=[row, _resident((1, D_MODEL)), _resident((D_MODEL, D_FF)),
                  _resident((D_MODEL, D_FF)), _resident((D_FF, D_MODEL))],
        out_specs=row,
        compiler_params=_params("parallel"),
        name="ffn",
    )(x, g, wg, wu, wd)


def _proj_residual_kernel(x_ref, a_ref, w_ref, o_ref):
    o_ref[...] = x_ref[...] + _dot(a_ref[...], w_ref[...])


def _proj_residual(x, a, w):
    m, k = a.shape
    row = pl.BlockSpec((TM, D_MODEL), lambda i: (i, 0))
    return pl.pallas_call(
        _proj_residual_kernel,
        out_shape=jax.ShapeDtypeStruct((m, D_MODEL), F32),
        grid=(m // TM,),
        in_specs=[row, pl.BlockSpec((TM, k), lambda i: (i, 0)), _resident((k, D_MODEL))],
        out_specs=row,
        compiler_params=_params("parallel"),
        name="proj_residual",
    )(x, a, w)


def _ret_proj_kernel(x_ref, g_ref, w_ref, cos_ref, sin_ref, o_ref):
    h = _rms(x_ref[...], g_ref[...], D_MODEL).astype(BF16)
    cos = cos_ref[...]
    sin = sin_ref[...]
    for start, scale in ((0, None), (RET_QK, RET_DK ** -0.5)):
        p = _dot(h, w_ref[:, start:start + RET_QK])
        for hd in range(HEADS):
            r = _rope128(p[:, hd * RET_DK:(hd + 1) * RET_DK], cos, sin)
            if scale is not None:
                r = r * scale
            o_ref[:, start + hd * RET_DK:start + (hd + 1) * RET_DK] = r.astype(BF16)
    o_ref[:, 2 * RET_QK:] = _dot(h, w_ref[:, 2 * RET_QK:]).astype(BF16)


def _ret_proj(x, g, w, cos, sin):
    m = x.shape[0]
    row = pl.BlockSpec((TM, D_MODEL), lambda i: (i, 0))
    tab = pl.BlockSpec((TM, LANES), lambda i: (i, 0))
    return pl.pallas_call(
        _ret_proj_kernel,
        out_shape=jax.ShapeDtypeStruct((m, RET_PROJ), BF16),
        grid=(m // TM,),
        in_specs=[row, _resident((1, D_MODEL)), _resident((D_MODEL, RET_PROJ)), tab, tab],
        out_specs=pl.BlockSpec((TM, RET_PROJ), lambda i: (i, 0)),
        compiler_params=_params("parallel"),
        name="ret_proj",
    )(x, g, w, cos, sin)


def _retention_kernel(q_ref, k_ref, v_ref, gate_ref, gn_ref, o_ref, state_ref):
    c = RET_CHUNK
    seq = q_ref.shape[0]
    hd = pl.program_id(1).astype(F32)

    def log_decay(shape):
        return jnp.log1p(-jnp.exp2(jnp.full(shape, -5.0, F32) - hd))

    def rows(shape):
        return lax.broadcasted_iota(jnp.int32, shape, 0).astype(F32)

    diff = rows((c, c)) - lax.broadcasted_iota(jnp.int32, (c, c), 1).astype(F32)
    d_intra = jnp.where(diff >= 0, jnp.exp(log_decay((c, c)) * jnp.maximum(diff, 0.0)), 0.0)
    q_decay = jnp.exp(log_decay((c, RET_DV)) * (rows((c, RET_DV)) + 1.0))
    k_decay = jnp.exp(log_decay((c, RET_DK)) * (c - 1.0 - rows((c, RET_DK))))
    chunk_decay = jnp.exp(log_decay((RET_DK, RET_DV)) * c)
    gn = gn_ref[0]
    state_ref[...] = jnp.zeros_like(state_ref)

    def body(t, carry):
        off = pl.multiple_of(t * c, c)
        q = q_ref[pl.ds(off, c), :]
        k = k_ref[pl.ds(off, c), :]
        v = v_ref[pl.ds(off, c), :]
        state = state_ref[...]
        s = _dot_nt(q, k) * d_intra
        o = _dot(s.astype(BF16), v) + _dot(q, state.astype(BF16)) * q_decay
        kd = (k.astype(F32) * k_decay).astype(BF16)
        state_ref[...] = state * chunk_decay + _dot_tn(kd, v)
        mu = jnp.mean(o, axis=-1, keepdims=True)
        oc = o - mu
        var = jnp.mean(oc * oc, axis=-1, keepdims=True)
        on = oc * lax.rsqrt(var + EPS) * gn
        gate = gate_ref[pl.ds(off, c), :].astype(F32)
        o_ref[pl.ds(off, c), :] = (gate / (1.0 + jnp.exp(-gate)) * on).astype(BF16)
        return carry

    lax.fori_loop(0, seq // c, body, 0)


def _retention(proj, gn, batch, seq):
    m = proj.shape[0]
    kblk = RET_QK // RET_DK
    vblk = 2 * RET_QK // RET_DV
    gblk = vblk + HEADS
    return pl.pallas_call(
        _retention_kernel,
        out_shape=jax.ShapeDtypeStruct((m, HEADS * RET_DV), BF16),
        grid=(batch, HEADS),
        in_specs=[
            pl.BlockSpec((seq, RET_DK), lambda b, h: (b, h)),
            pl.BlockSpec((seq, RET_DK), lambda b, h: (b, kblk + h)),
            pl.BlockSpec((seq, RET_DV), lambda b, h: (b, vblk + h)),
            pl.BlockSpec((seq, RET_DV), lambda b, h: (b, gblk + h)),
            pl.BlockSpec((1, 1, RET_DV), lambda b, h: (h, 0, 0)),
        ],
        out_specs=pl.BlockSpec((seq, RET_DV), lambda b, h: (b, h)),
        scratch_shapes=[pltpu.VMEM((RET_DK, RET_DV), F32)],
        compiler_params=_params("parallel", "parallel"),
        name="retention",
    )(proj, proj, proj, proj, gn)


def _kv_kernel(x_ref, g_ref, wdc_ref, wdr_ref, lat_g_ref, wk_ref, wv_ref, kn_g_ref, kr_g_ref,
               cos_ref, sa_ref, sb_ref, k_out, v_out):
    h = _rms(x_ref[...], g_ref[...], D_MODEL).astype(BF16)
    lat = _rms(_dot(h, wdc_ref[...]), lat_g_ref[...], KV_LORA).astype(BF16)
    v_out[...] = _dot(lat, wv_ref[...]).astype(BF16)
    pe = _rms(_dot(h, wdr_ref[...]), kr_g_ref[...], MLA_ROPE)
    pe = _rope64(pe, cos_ref[...], sa_ref[...], sb_ref[...]).astype(BF16)
    kk = _dot(lat, wk_ref[...])
    kn_g = kn_g_ref[...]
    for hd in range(HEADS):
        kh = _rms(kk[:, hd * MLA_NOPE:(hd + 1) * MLA_NOPE], kn_g, MLA_NOPE)
        k_out[:, hd * MLA_QK_PAD:hd * MLA_QK_PAD + MLA_NOPE] = kh.astype(BF16)
        k_out[:, hd * MLA_QK_PAD + MLA_NOPE:(hd + 1) * MLA_QK_PAD] = pe


def _shared_kv(x, g, wdc, wdr, lat_g, wk, wv, kn_g, kr_g, cos, sa, sb):
    m = x.shape[0]
    row = pl.BlockSpec((TM, D_MODEL), lambda i: (i, 0))
    tab = pl.BlockSpec((TM, LANES), lambda i: (i, 0))
    return pl.pallas_call(
        _kv_kernel,
        out_shape=(jax.ShapeDtypeStruct((m, HEADS * MLA_QK_PAD), BF16),
                   jax.ShapeDtypeStruct((m, HEADS * MLA_V), BF16)),
        grid=(m // TM,),
        in_specs=[row, _resident((1, D_MODEL)), _resident((D_MODEL, KV_LORA)),
                  _resident((D_MODEL, LANES)), _resident((1, KV_LORA)),
                  _resident((KV_LORA, HEADS * MLA_NOPE)), _resident((KV_LORA, HEADS * MLA_V)),
                  _resident((1, MLA_NOPE)), _resident((1, LANES)), tab, tab, tab],
        out_specs=(pl.BlockSpec((TM, HEADS * MLA_QK_PAD), lambda i: (i, 0)),
                   pl.BlockSpec((TM, HEADS * MLA_V), lambda i: (i, 0))),
        compiler_params=_params("parallel"),
        name="shared_kv",
    )(x, g, wdc, wdr, lat_g, wk, wv, kn_g, kr_g, cos, sa, sb)


Q_SCALE = (MLA_NOPE + MLA_ROPE) ** -0.5 * math.log2(math.e)


def _q_kernel(x_ref, g_ref, wdq_ref, lora_g_ref, wuq_ref, qn_g_ref, qr_g_ref,
              cos_ref, sa_ref, sb_ref, q_out):
    h = _rms(x_ref[...], g_ref[...], D_MODEL).astype(BF16)
    cq = _rms(_dot(h, wdq_ref[...]), lora_g_ref[...], Q_LORA).astype(BF16)
    q = _dot(cq, wuq_ref[...])
    qn_g = qn_g_ref[...]
    qr_g = qr_g_ref[...]
    cos = cos_ref[...]
    sa = sa_ref[...]
    sb = sb_ref[...]
    for hd in range(HEADS):
        lo = hd * MLA_QK_PAD
        qn = _rms(q[:, lo:lo + MLA_NOPE], qn_g, MLA_NOPE) * Q_SCALE
        qp = _rope64(_rms(q[:, lo + MLA_NOPE:lo + MLA_QK_PAD], qr_g, MLA_ROPE), cos, sa, sb) * Q_SCALE
        q_out[:, lo:lo + MLA_NOPE] = qn.astype(BF16)
        q_out[:, lo + MLA_NOPE:lo + MLA_QK_PAD] = qp.astype(BF16)


def _mla_q(x, g, wdq, lora_g, wuq, qn_g, qr_g, cos, sa, sb):
    m = x.shape[0]
    row = pl.BlockSpec((TM, D_MODEL), lambda i: (i, 0))
    tab = pl.BlockSpec((TM, LANES), lambda i: (i, 0))
    return pl.pallas_call(
        _q_kernel,
        out_shape=jax.ShapeDtypeStruct((m, HEADS * MLA_QK_PAD), BF16),
        grid=(m // TM,),
        in_specs=[row, _resident((1, D_MODEL)), _resident((D_MODEL, Q_LORA)),
                  _resident((1, Q_LORA)), _resident((Q_LORA, HEADS * MLA_QK_PAD)),
                  _resident((1, MLA_NOPE)), _resident((1, LANES)), tab, tab, tab],
        out_specs=pl.BlockSpec((TM, HEADS * MLA_QK_PAD), lambda i: (i, 0)),
        compiler_params=_params("parallel"),
        name="mla_q",
    )(x, g, wdq, lora_g, wuq, qn_g, qr_g, cos, sa, sb)


def _flash_kernel(q_ref, k_ref, v_ref, o_ref, m_ref, l_ref, acc_ref):
    i = pl.program_id(2)
    q = q_ref[...]
    m_ref[...] = jnp.full_like(m_ref, NEG_BIG)
    l_ref[...] = jnp.zeros_like(l_ref)
    acc_ref[...] = jnp.zeros_like(acc_ref)

    def step(j, masked):
        off = pl.multiple_of(j * TQ, TQ)
        s = _dot_nt(q, k_ref[pl.ds(off, TQ), :])
        if masked:
            row = lax.broadcasted_iota(jnp.int32, (TQ, TQ), 0)
            col = lax.broadcasted_iota(jnp.int32, (TQ, TQ), 1)
            s = jnp.where(col <= row, s, NEG_BIG)
        m_prev = m_ref[...]
        m_new = jnp.maximum(m_prev, jnp.max(s, axis=1, keepdims=True))
        alpha = jnp.exp2(m_prev - m_new)
        p = jnp.exp2(s - m_new)
        l_ref[...] = alpha * l_ref[...] + jnp.sum(p, axis=1, keepdims=True)
        acc_ref[...] = alpha * acc_ref[...] + _dot(p.astype(BF16), v_ref[pl.ds(off, TQ), :])
        m_ref[...] = m_new

    def body(j, carry):
        step(j, False)
        return carry

    lax.fori_loop(0, i, body, 0)
    step(i, True)
    o_ref[...] = (acc_ref[...] / l_ref[...]).astype(BF16)


def _flash(q, k, v, batch, seq):
    m = q.shape[0]
    nq = seq // TQ
    return pl.pallas_call(
        _flash_kernel,
        out_shape=jax.ShapeDtypeStruct((m, HEADS * MLA_V), BF16),
        grid=(batch, HEADS, nq),
        in_specs=[
            pl.BlockSpec((TQ, MLA_QK_PAD), lambda b, h, i: (b * nq + i, h)),
            pl.BlockSpec((seq, MLA_QK_PAD), lambda b, h, i: (b, h)),
            pl.BlockSpec((seq, MLA_V), lambda b, h, i: (b, h)),
        ],
        out_specs=pl.BlockSpec((TQ, MLA_V), lambda b, h, i: (b * nq + i, h)),
        scratch_shapes=[pltpu.VMEM((TQ, 1), F32), pltpu.VMEM((TQ, 1), F32),
                        pltpu.VMEM((TQ, MLA_V), F32)],
        compiler_params=_params("parallel", "parallel", "arbitrary"),
        name="flash_attention",
    )(q, k, v)


def _pad_cols(w, width):
    return jnp.pad(w, ((0, 0), (0, width - w.shape[1])))


def _row(g):
    return g.reshape(1, -1).astype(F32)


def kernel(x, positions, norm_g, ffn_w_gate, ffn_w_up, ffn_w_down, ret_w_in, ret_gn_g, ret_w_o,
           kv_norm_g, kv_w_down, kv_latent_norm_g, kv_w_up, k_nope_norm_g, k_rope_norm_g,
           mla_w_dq, mla_q_lora_norm_g, mla_w_uq, mla_q_nope_norm_g, mla_q_rope_norm_g, mla_w_o):
    batch, seq, d = x.shape
    depth = norm_g.shape[0]
    n_self = ret_w_in.shape[0]
    m = batch * seq
    x = x.reshape(m, d)
    rcos, rsin, mcos, msa, msb = _rope_tables(positions.reshape(m, 1))

    def ffn(x, layer, i, gi):
        return _ffn(x, _row(norm_g[layer, gi]), ffn_w_gate[layer, i].astype(BF16),
                    ffn_w_up[layer, i].astype(BF16), ffn_w_down[layer, i].astype(BF16))

    k_shared = v_shared = None
    for layer in range(depth):
        x = ffn(x, layer, 0, 0)
        g_mix = _row(norm_g[layer, 1])
        if layer < n_self:
            proj = _ret_proj(x, g_mix, ret_w_in[layer].astype(BF16), rcos, rsin)
            gated = _retention(proj, ret_gn_g[layer].reshape(HEADS, 1, RET_DV), batch, seq)
            x = _proj_residual(x, gated, ret_w_o[layer].astype(BF16))
        else:
            j = layer - n_self
            wuq = mla_w_uq[j].reshape(Q_LORA, HEADS, MLA_NOPE + MLA_ROPE)
            wuq = jnp.pad(wuq, ((0, 0), (0, 0), (0, MLA_QK_PAD - MLA_NOPE - MLA_ROPE)))
            wuq = wuq.reshape(Q_LORA, HEADS * MLA_QK_PAD).astype(BF16)
            q = _mla_q(x, g_mix, mla_w_dq[j].astype(BF16), _row(mla_q_lora_norm_g[j]), wuq,
                       _row(mla_q_nope_norm_g[j]), _pad_cols(_row(mla_q_rope_norm_g[j]), LANES),
                       mcos, msa, msb)
            attn = _flash(q, k_shared, v_shared, batch, seq)
            x = _proj_residual(x, attn, mla_w_o[j].astype(BF16))
        x = ffn(x, layer, 1, 2)
        if layer == n_self - 1:
            wup = kv_w_up.reshape(KV_LORA, HEADS, MLA_NOPE + MLA_V)
            wk = wup[:, :, :MLA_NOPE].reshape(KV_LORA, HEADS * MLA_NOPE).astype(BF16)
            wv = wup[:, :, MLA_NOPE:].reshape(KV_LORA, HEADS * MLA_V).astype(BF16)
            k_shared, v_shared = _shared_kv(
                x, _row(kv_norm_g), kv_w_down[:, :KV_LORA].astype(BF16),
                _pad_cols(kv_w_down[:, KV_LORA:], LANES).astype(BF16), _row(kv_latent_norm_g),
                wk, wv, _row(k_nope_norm_g), _pad_cols(_row(k_rope_norm_g), LANES),
                mcos, msa, msb)
    return x.reshape(batch, seq, d)
```

```python
import functools
import math

import jax
import jax.numpy as jnp
from jax import lax
from jax.experimental import pallas as pl
from jax.experimental.pallas import tpu as pltpu

F32 = jnp.float32
BF16 = jnp.bfloat16

LANES = 128

D_MODEL = 1024
D_FF = 2816
HEADS = 8
RET_DK = 128
RET_DV = 256
RET_QK = HEADS * RET_DK
RET_PROJ = 2 * RET_QK + 2 * HEADS * RET_DV
MLA_NOPE = 128
MLA_ROPE = 64
MLA_V = 128
MLA_QK_PAD = 2 * LANES
V_PAD = 2 * LANES
HEADS_PER_STEP = 2
Q_LORA = 384
KV_LORA = 256
ROPE_BASE = 10000.0
EPS = 1e-6

VMEM_LIMIT = 56 * 1024 * 1024

TM = 512
FF_CHUNK = D_FF // 2
RET_CHUNK = 256
TQ = 512
NEG_BIG = -1e30


def _dot(a, b):
    return jnp.dot(a, b, preferred_element_type=F32)


def _dot_nt(a, b):
    return lax.dot_general(a, b, (((1,), (1,)), ((), ())), preferred_element_type=F32)


def _dot_tn(a, b):
    return lax.dot_general(a, b, (((0,), (0,)), ((), ())), preferred_element_type=F32)


def _rms(x, g, n):
    ms = jnp.sum(x * x, axis=-1, keepdims=True) * (1.0 / n)
    return x * lax.rsqrt(ms + EPS) * g


def _params(*sem):
    return pltpu.CompilerParams(dimension_semantics=sem, vmem_limit_bytes=VMEM_LIMIT)


def _resident(shape):
    nd = len(shape)
    return pl.BlockSpec(shape, lambda *_: (0,) * nd, pipeline_mode=pl.Buffered(1))


def _rope_tables_kernel(pos_ref, rcos_ref, rsin_ref, mcos_ref, msa_ref, msb_ref):
    pos = pos_ref[...].astype(F32)
    lane = lax.broadcasted_iota(jnp.int32, (1, LANES), 1)
    inv = jnp.exp((lane & 63).astype(F32) * (-2.0 / RET_DK * math.log(ROPE_BASE)))
    ang = pos * inv
    s = jnp.sin(ang)
    rcos_ref[...] = jnp.cos(ang)
    rsin_ref[...] = jnp.where(lane < 64, -s, s)
    inv = jnp.exp((lane & 31).astype(F32) * (-2.0 / MLA_ROPE * math.log(ROPE_BASE)))
    ang = pos * inv
    s = jnp.sin(ang)
    mcos_ref[...] = jnp.where(lane < 64, jnp.cos(ang), 0.0)
    msa_ref[...] = jnp.where(lane < 32, -s, 0.0)
    msb_ref[...] = jnp.where((lane >= 32) & (lane < 64), s, 0.0)


def _rope_tables(pos):
    m = pos.shape[0]
    tm = 1024
    tab = jax.ShapeDtypeStruct((m, LANES), F32)
    spec = pl.BlockSpec((tm, LANES), lambda i: (i, 0))
    return pl.pallas_call(
        _rope_tables_kernel,
        out_shape=(tab,) * 5,
        grid=(m // tm,),
        in_specs=[pl.BlockSpec((tm, 1), lambda i: (i, 0))],
        out_specs=(spec,) * 5,
        compiler_params=_params("parallel"),
        name="rope_tables",
    )(pos)


def _rope128(x, cos, sin):
    return x * cos + pltpu.roll(x, 64, 1) * sin


def _rope64(x, cos, sa, sb):
    return x * cos + pltpu.roll(x, 96, 1) * sa + pltpu.roll(x, 32, 1) * sb


def _ffn_kernel(x_ref, g_ref, wg_ref, wu_ref, wd_ref, o_ref):
    x = x_ref[...]
    h = _rms(x, g_ref[...], D_MODEL).astype(BF16)
    acc = None
    for c in range(D_FF // FF_CHUNK):
        sl = slice(c * FF_CHUNK, (c + 1) * FF_CHUNK)
        gate = _dot(h, wg_ref[:, sl])
        up = _dot(h, wu_ref[:, sl])
        a = (gate / (1.0 + jnp.exp(-gate)) * up).astype(BF16)
        d = _dot(a, wd_ref[sl, :])
        acc = d if acc is None else acc + d
    o_ref[...] = x + 0.5 * acc


def _ffn(x, g, wg, wu, wd):
    m = x.shape[0]
    row = pl.BlockSpec((TM, D_MODEL), lambda i: (i, 0))
    return pl.pallas_call(
        _ffn_kernel,
        out_shape=jax.ShapeDtypeStruct((m, D_MODEL), F32),
        grid=(m // TM,),
        in_specs=[row, _resident((1, D_MODEL)), _resident((D_MODEL, D_FF)),
                  _resident((D_MODEL, D_FF)), _resident((D_FF, D_MODEL))],
        out_specs=row,
        compiler_params=_params("parallel"),
        name="ffn",
    )(x, g, wg, wu, wd)


def _proj_residual_kernel(x_ref, a_ref, w_ref, o_ref):
    o_ref[...] = x_ref[...] + _dot(a_ref[...], w_ref[...])


def _proj_residual(x, a, w):
    m, k = a.shape
    row = pl.BlockSpec((TM, D_MODEL), lambda i: (i, 0))
    return pl.pallas_call(
        _proj_residual_kernel,
        out_shape=jax.ShapeDtypeStruct((m, D_MODEL), F32),
        grid=(m // TM,),
        in_specs=[row, pl.BlockSpec((TM, k), lambda i: (i, 0)), _resident((k, D_MODEL))],
        out_specs=row,
        compiler_params=_params("parallel"),
        name="proj_residual",
    )(x, a, w)


def _ret_proj_kernel(x_ref, g_ref, w_ref, cos_ref, sin_ref, o_ref):
    h = _rms(x_ref[...], g_ref[...], D_MODEL).astype(BF16)
    cos = cos_ref[...]
    sin = sin_ref[...]
    for start, scale in ((0, None), (RET_QK, RET_DK ** -0.5)):
        p = _dot(h, w_ref[:, start:start + RET_QK])
        for hd in range(HEADS):
            r = _rope128(p[:, hd * RET_DK:(hd + 1) * RET_DK], cos, sin)
            if scale is not None:
                r = r * scale
            o_ref[:, start + hd * RET_DK:start + (hd + 1) * RET_DK] = r.astype(BF16)
    o_ref[:, 2 * RET_QK:] = _dot(h, w_ref[:, 2 * RET_QK:]).astype(BF16)


def _ret_proj(x, g, w, cos, sin):
    m = x.shape[0]
    row = pl.BlockSpec((TM, D_MODEL), lambda i: (i, 0))
    tab = pl.BlockSpec((TM, LANES), lambda i: (i, 0))
    return pl.pallas_call(
        _ret_proj_kernel,
        out_shape=jax.ShapeDtypeStruct((m, RET_PROJ), BF16),
        grid=(m // TM,),
        in_specs=[row, _resident((1, D_MODEL)), _resident((D_MODEL, RET_PROJ)), tab, tab],
        out_specs=pl.BlockSpec((TM, RET_PROJ), lambda i: (i, 0)),
        compiler_params=_params("parallel"),
        name="ret_proj",
    )(x, g, w, cos, sin)


def _retention_kernel(q_ref, k_ref, v_ref, gate_ref, gn_ref, o_ref, state_ref):
    c = RET_CHUNK
    seq = q_ref.shape[0]
    hd = pl.program_id(1).astype(F32)

    def log_decay(shape):
        return jnp.log1p(-jnp.exp2(jnp.full(shape, -5.0, F32) - hd))

    def rows(shape):
        return lax.broadcasted_iota(jnp.int32, shape, 0).astype(F32)

    diff = rows((c, c)) - lax.broadcasted_iota(jnp.int32, (c, c), 1).astype(F32)
    d_intra = jnp.where(diff >= 0, jnp.exp(log_decay((c, c)) * jnp.maximum(diff, 0.0)), 0.0)
    q_decay = jnp.exp(log_decay((c, RET_DV)) * (rows((c, RET_DV)) + 1.0))
    k_decay = jnp.exp(log_decay((c, RET_DK)) * (c - 1.0 - rows((c, RET_DK))))
    chunk_decay = jnp.exp(log_decay((RET_DK, RET_DV)) * c)
    gn = gn_ref[0]
    state_ref[...] = jnp.zeros_like(state_ref)

    def body(t, carry):
        off = pl.multiple_of(t * c, c)
        q = q_ref[pl.ds(off, c), :]
        k = k_ref[pl.ds(off, c), :]
        v = v_ref[pl.ds(off, c), :]
        state = state_ref[...]
        s = _dot_nt(q, k) * d_intra
        o = _dot(s.astype(BF16), v) + _dot(q, state.astype(BF16)) * q_decay
        kd = (k.astype(F32) * k_decay).astype(BF16)
        state_ref[...] = state * chunk_decay + _dot_tn(kd, v)
        mu = jnp.mean(o, axis=-1, keepdims=True)
        oc = o - mu
        var = jnp.mean(oc * oc, axis=-1, keepdims=True)
        on = oc * lax.rsqrt(var + EPS) * gn
        gate = gate_ref[pl.ds(off, c), :].astype(F32)
        o_ref[pl.ds(off, c), :] = (gate / (1.0 + jnp.exp(-gate)) * on).astype(BF16)
        return carry

    lax.fori_loop(0, seq // c, body, 0, unroll=4)


def _retention(proj, gn, batch, seq):
    m = proj.shape[0]
    kblk = RET_QK // RET_DK
    vblk = 2 * RET_QK // RET_DV
    gblk = vblk + HEADS
    return pl.pallas_call(
        _retention_kernel,
        out_shape=jax.ShapeDtypeStruct((m, HEADS * RET_DV), BF16),
        grid=(batch, HEADS),
        in_specs=[
            pl.BlockSpec((seq, RET_DK), lambda b, h: (b, h)),
            pl.BlockSpec((seq, RET_DK), lambda b, h: (b, kblk + h)),
            pl.BlockSpec((seq, RET_DV), lambda b, h: (b, vblk + h)),
            pl.BlockSpec((seq, RET_DV), lambda b, h: (b, gblk + h)),
            pl.BlockSpec((1, 1, RET_DV), lambda b, h: (h, 0, 0)),
        ],
        out_specs=pl.BlockSpec((seq, RET_DV), lambda b, h: (b, h)),
        scratch_shapes=[pltpu.VMEM((RET_DK, RET_DV), F32)],
        compiler_params=_params("parallel", "parallel"),
        name="retention",
    )(proj, proj, proj, proj, gn)


def _kv_kernel(x_ref, g_ref, wdc_ref, wdr_ref, lat_g_ref, wk_ref, wv_ref, kn_g_ref, kr_g_ref,
               cos_ref, sa_ref, sb_ref, k_out, v_out):
    h = _rms(x_ref[...], g_ref[...], D_MODEL).astype(BF16)
    lat = _rms(_dot(h, wdc_ref[...]), lat_g_ref[...], KV_LORA).astype(BF16)
    vv = _dot(lat, wv_ref[...]).astype(BF16)
    ones = jnp.ones((vv.shape[0], V_PAD - MLA_V), BF16)
    for hd in range(HEADS):
        v_out[:, hd * V_PAD:hd * V_PAD + MLA_V] = vv[:, hd * MLA_V:(hd + 1) * MLA_V]
        v_out[:, hd * V_PAD + MLA_V:(hd + 1) * V_PAD] = ones
    pe =_rms(_dot(h, wdr_ref[...]), kr_g_ref[...], MLA_ROPE)
    pe = _rope64(pe, cos_ref[...], sa_ref[...], sb_ref[...]).astype(BF16)
    kk = _dot(lat, wk_ref[...])
    kn_g = kn_g_ref[...]
    for hd in range(HEADS):
        kh = _rms(kk[:, hd * MLA_NOPE:(hd + 1) * MLA_NOPE], kn_g, MLA_NOPE)
        k_out[:, hd * MLA_QK_PAD:hd * MLA_QK_PAD + MLA_NOPE] = kh.astype(BF16)
        k_out[:, hd * MLA_QK_PAD + MLA_NOPE:(hd + 1) * MLA_QK_PAD] = pe


def _shared_kv(x, g, wdc, wdr, lat_g, wk, wv, kn_g, kr_g, cos, sa, sb):
    m = x.shape[0]
    row = pl.BlockSpec((TM, D_MODEL), lambda i: (i, 0))
    tab = pl.BlockSpec((TM, LANES), lambda i: (i, 0))
    return pl.pallas_call(
        _kv_kernel,
        out_shape=(jax.ShapeDtypeStruct((m, HEADS * MLA_QK_PAD), BF16),
                   jax.ShapeDtypeStruct((m, HEADS * V_PAD), BF16)),
        grid=(m // TM,),
        in_specs=[row, _resident((1, D_MODEL)), _resident((D_MODEL, KV_LORA)),
                  _resident((D_MODEL, LANES)), _resident((1, KV_LORA)),
                  _resident((KV_LORA, HEADS * MLA_NOPE)), _resident((KV_LORA, HEADS * MLA_V)),
                  _resident((1, MLA_NOPE)), _resident((1, LANES)), tab, tab, tab],
        out_specs=(pl.BlockSpec((TM, HEADS * MLA_QK_PAD), lambda i: (i, 0)),
                   pl.BlockSpec((TM, HEADS * V_PAD), lambda i: (i, 0))),
        compiler_params=_params("parallel"),
        name="shared_kv",
    )(x, g, wdc, wdr, lat_g, wk, wv, kn_g, kr_g, cos, sa, sb)


Q_SCALE = (MLA_NOPE + MLA_ROPE) ** -0.5 * math.log2(math.e)


def _q_kernel(x_ref, g_ref, wdq_ref, lora_g_ref, wuq_ref, qn_g_ref, qr_g_ref,
              cos_ref, sa_ref, sb_ref, q_out):
    h = _rms(x_ref[...], g_ref[...], D_MODEL).astype(BF16)
    cq = _rms(_dot(h, wdq_ref[...]), lora_g_ref[...], Q_LORA).astype(BF16)
    q = _dot(cq, wuq_ref[...])
    qn_g = qn_g_ref[...]
    qr_g = qr_g_ref[...]
    cos = cos_ref[...]
    sa = sa_ref[...]
    sb = sb_ref[...]
    for hd in range(HEADS):
        lo = hd * MLA_QK_PAD
        qn = _rms(q[:, lo:lo + MLA_NOPE], qn_g, MLA_NOPE) * Q_SCALE
        qp = _rope64(_rms(q[:, lo + MLA_NOPE:lo + MLA_QK_PAD], qr_g, MLA_ROPE), cos, sa, sb) * Q_SCALE
        q_out[:, lo:lo + MLA_NOPE] = qn.astype(BF16)
        q_out[:, lo + MLA_NOPE:lo + MLA_QK_PAD] = qp.astype(BF16)


def _mla_q(x, g, wdq, lora_g, wuq, qn_g, qr_g, cos, sa, sb):
    m = x.shape[0]
    row = pl.BlockSpec((TM, D_MODEL), lambda i: (i, 0))
    tab = pl.BlockSpec((TM, LANES), lambda i: (i, 0))
    return pl.pallas_call(
        _q_kernel,
        out_shape=jax.ShapeDtypeStruct((m, HEADS * MLA_QK_PAD), BF16),
        grid=(m // TM,),
        in_specs=[row, _resident((1, D_MODEL)), _resident((D_MODEL, Q_LORA)),
                  _resident((1, Q_LORA)), _resident((Q_LORA, HEADS * MLA_QK_PAD)),
                  _resident((1, MLA_NOPE)), _resident((1, LANES)), tab, tab, tab],
        out_specs=pl.BlockSpec((TM, HEADS * MLA_QK_PAD), lambda i: (i, 0)),
        compiler_params=_params("parallel"),
        name="mla_q",
    )(x, g, wdq, lora_g, wuq, qn_g, qr_g, cos, sa, sb)


def _flash_kernel(q_ref, k_ref, v_ref, o_ref, s0_ref, s1_ref, m_ref, acc_ref):
    i = pl.program_id(2)
    m_ref[...] = jnp.full_like(m_ref, NEG_BIG)
    acc_ref[...] = jnp.zeros_like(acc_ref)

    def scores(j, s_ref):
        off = pl.multiple_of(j * TQ, TQ)
        for hd in range(HEADS_PER_STEP):
            qk = slice(hd * MLA_QK_PAD, (hd + 1) * MLA_QK_PAD)
            s_ref[hd] = _dot_nt(q_ref[:, qk], k_ref[pl.ds(off, TQ), qk])

    def consume(j, s_ref, masked=False):
        off = pl.multiple_of(j * TQ, TQ)
        for hd in range(HEADS_PER_STEP):
            s = s_ref[hd]
            if masked:
                row = lax.broadcasted_iota(jnp.int32, (TQ, TQ), 0)
                col = lax.broadcasted_iota(jnp.int32, (TQ, TQ), 1)
                s = jnp.where(col <= row, s, NEG_BIG)
            m_prev = m_ref[hd]
            m_new = jnp.maximum(m_prev, jnp.max(s, axis=1, keepdims=True))
            alpha = jnp.exp2(m_prev - m_new)
            p = jnp.exp2(s - pltpu.repeat(m_new, TQ // LANES, axis=1))
            pv = _dot(p.astype(BF16), v_ref[pl.ds(off, TQ), hd * V_PAD:(hd + 1) * V_PAD])
            acc_ref[hd] = pltpu.repeat(alpha, V_PAD // LANES, axis=1) * acc_ref[hd] + pv
            m_ref[hd] = m_new

    def pair(t, carry):
        j = 2 * t
        scores(j + 1, s1_ref)
        consume(j, s0_ref)
        scores(j + 2, s0_ref)
        consume(j + 1, s1_ref)
        return carry

    scores(0, s0_ref)
    lax.fori_loop(0, i // 2, pair, 0)

    @pl.when(i % 2 == 0)
    def _():
        consume(i, s0_ref, masked=True)

    @pl.when(i % 2 == 1)
    def _():
        scores(i, s1_ref)
        consume(i - 1, s0_ref)
        consume(i, s1_ref, masked=True)

    for hd in range(HEADS_PER_STEP):
        acc = acc_ref[hd]
        o_ref[:, hd * MLA_V:(hd + 1) * MLA_V] = (acc[:, :MLA_V] / acc[:, MLA_V:]).astype(BF16)


def _flash(q, k, v, batch, seq):
    m = q.shape[0]
    nq = seq // TQ
    g = HEADS_PER_STEP
    return pl.pallas_call(
        _flash_kernel,
        out_shape=jax.ShapeDtypeStruct((m, HEADS * MLA_V), BF16),
        grid=(batch, HEADS // g, nq),
        in_specs=[
            pl.BlockSpec((TQ, g * MLA_QK_PAD), lambda b, h, i: (b * nq + i, h)),
            pl.BlockSpec((seq, g * MLA_QK_PAD), lambda b, h, i: (b, h)),
            pl.BlockSpec((seq, g * V_PAD), lambda b, h, i: (b, h)),
        ],
        out_specs=pl.BlockSpec((TQ, g * MLA_V), lambda b, h, i: (b * nq + i, h)),
        scratch_shapes=[pltpu.VMEM((g, TQ, TQ), F32), pltpu.VMEM((g, TQ, TQ), F32),
                        pltpu.VMEM((g, TQ, LANES), F32),
                        pltpu.VMEM((g, TQ, V_PAD), F32)],
        compiler_params=_params("parallel", "parallel", "arbitrary"),
        name="flash_attention",
    )(q, k, v)


def _pad_cols(w, width):
    return jnp.pad(w, ((0, 0), (0, width - w.shape[1])))


def _row(g):
    return g.reshape(1, -1).astype(F32)


def kernel(x, positions, norm_g, ffn_w_gate, ffn_w_up, ffn_w_down, ret_w_in, ret_gn_g, ret_w_o,
           kv_norm_g, kv_w_down, kv_latent_norm_g, kv_w_up, k_nope_norm_g, k_rope_norm_g,
           mla_w_dq, mla_q_lora_norm_g, mla_w_uq, mla_q_nope_norm_g, mla_q_rope_norm_g, mla_w_o):
    batch, seq, d = x.shape
    depth = norm_g.shape[0]
    n_self = ret_w_in.shape[0]
    m = batch * seq
    x = x.reshape(m, d)
    rcos, rsin, mcos, msa, msb = _rope_tables(positions.reshape(m, 1))

    def ffn(x, layer, i, gi):
        return _ffn(x, _row(norm_g[layer, gi]), ffn_w_gate[layer, i].astype(BF16),
                    ffn_w_up[layer, i].astype(BF16), ffn_w_down[layer, i].astype(BF16))

    k_shared = v_shared = None
    for layer in range(depth):
        x = ffn(x, layer, 0, 0)
        g_mix = _row(norm_g[layer, 1])
        if layer < n_self:
            proj = _ret_proj(x, g_mix, ret_w_in[layer].astype(BF16), rcos, rsin)
            gated = _retention(proj, ret_gn_g[layer].reshape(HEADS, 1, RET_DV), batch, seq)
            x = _proj_residual(x, gated, ret_w_o[layer].astype(BF16))
        else:
            j = layer - n_self
            wuq = mla_w_uq[j].reshape(Q_LORA, HEADS, MLA_NOPE + MLA_ROPE)
            wuq = jnp.pad(wuq, ((0, 0), (0, 0), (0, MLA_QK_PAD - MLA_NOPE - MLA_ROPE)))
            wuq = wuq.reshape(Q_LORA, HEADS * MLA_QK_PAD).astype(BF16)
            q = _mla_q(x, g_mix, mla_w_dq[j].astype(BF16), _row(mla_q_lora_norm_g[j]), wuq,
                       _row(mla_q_nope_norm_g[j]), _pad_cols(_row(mla_q_rope_norm_g[j]), LANES),
                       mcos, msa, msb)
            attn = _flash(q, k_shared, v_shared, batch, seq)
            x = _proj_residual(x, attn, mla_w_o[j].astype(BF16))
        x = ffn(x, layer, 1, 2)
        if layer == n_self - 1:
            wup = kv_w_up.reshape(KV_LORA, HEADS, MLA_NOPE + MLA_V)
            wk = wup[:, :, :MLA_NOPE].reshape(KV_LORA, HEADS * MLA_NOPE).astype(BF16)
            wv = wup[:, :, MLA_NOPE:].reshape(KV_LORA, HEADS * MLA_V).astype(BF16)
            k_shared, v_shared = _shared_kv(
                x, _row(kv_norm_g), kv_w_down[:, :KV_LORA].astype(BF16),
                _pad_cols(kv_w_down[:, KV_LORA:], LANES).astype(BF16), _row(kv_latent_norm_g),
                wk, wv, _row(k_nope_norm_g), _pad_cols(_row(k_rope_norm_g), LANES),
                mcos, msa, msb)
    return x.reshape(batch, seq, d)
```

```python
import math

import jax
import jax.numpy as jnp
from jax import lax
from jax.experimental import pallas as pl
from jax.experimental.pallas import tpu as pltpu

F32 = jnp.float32
BF16 = jnp.bfloat16

LANES = 128

D_MODEL = 1024
D_FF = 2816
HEADS = 8
RET_DK = 128
RET_DV = 256
RET_QK = HEADS * RET_DK
RET_PROJ = 2 * RET_QK + 2 * HEADS * RET_DV
MLA_NOPE = 128
MLA_ROPE = 64
MLA_V = 128
MLA_QK_PAD = 2 * LANES
V_PAD = 2 * LANES
HEADS_PER_STEP = 2
Q_LORA = 384
KV_LORA = 256
ROPE_BASE = 10000.0
EPS = 1e-6

VMEM_LIMIT = 56 * 1024 * 1024

TM = 512
FF_CHUNK = D_FF // 2
RET_CHUNK = 256
TQ = 512
NEG_BIG = -1e30


def _dot(a, b):
    return jnp.dot(a, b, preferred_element_type=F32)


def _dot_nt(a, b):
    return lax.dot_general(a, b, (((1,), (1,)), ((), ())), preferred_element_type=F32)


def _dot_tn(a, b):
    return lax.dot_general(a, b, (((0,), (0,)), ((), ())), preferred_element_type=F32)


def _rms(x, g, n):
    ms = jnp.sum(x * x, axis=-1, keepdims=True) * (1.0 / n)
    return x * lax.rsqrt(ms + EPS) * g


def _silu(x):
    return x / (1.0 + jnp.exp(-x))


def _lane_tile(x, n):
    return jnp.concatenate([x] * n, axis=1)


def _group_ones(width, group):
    shift = group.bit_length() - 1
    r = lax.broadcasted_iota(jnp.int32, (width, width), 0) >> shift
    c = lax.broadcasted_iota(jnp.int32, (width, width), 1) >> shift
    return jnp.where(r == c, 1.0, 0.0).astype(BF16)


def _group_rms(x, ones, group, g):
    ms = _dot((x * x).astype(BF16), ones) * (1.0 / group)
    return x * lax.rsqrt(ms + EPS) * g


def _params(*sem):
    return pltpu.CompilerParams(dimension_semantics=sem, vmem_limit_bytes=VMEM_LIMIT)


def _resident(shape):
    nd = len(shape)
    return pl.BlockSpec(shape, lambda *_: (0,) * nd, pipeline_mode=pl.Buffered(1))


def _rows(width):
    return pl.BlockSpec((TM, width), lambda i: (i, 0))


def _rope_tables_kernel(pos_ref, rcos_ref, rsin_ref, mcos_ref, msa_ref, msb_ref):
    pos = pos_ref[...].astype(F32)
    lane = lax.broadcasted_iota(jnp.int32, (1, LANES), 1)
    inv = jnp.exp((lane & 63).astype(F32) * (-2.0 / RET_DK * math.log(ROPE_BASE)))
    ang = pos * inv
    s = jnp.sin(ang)
    rcos_ref[...] = jnp.cos(ang)
    rsin_ref[...] = jnp.where(lane < 64, -s, s)
    inv = jnp.exp((lane & 31).astype(F32) * (-2.0 / MLA_ROPE * math.log(ROPE_BASE)))
    ang = pos * inv
    s = jnp.sin(ang)
    first_half = (lane & 63) < 32
    mcos_ref[...] = jnp.cos(ang)
    msa_ref[...] = jnp.where(first_half, -s, 0.0)
    msb_ref[...] = jnp.where(first_half, 0.0, s)


def _rope_tables(pos):
    m = pos.shape[0]
    tm = 1024
    tab = jax.ShapeDtypeStruct((m, LANES), F32)
    spec = pl.BlockSpec((tm, LANES), lambda i: (i, 0))
    return pl.pallas_call(
        _rope_tables_kernel,
        out_shape=(tab,) * 5,
        grid=(m // tm,),
        in_specs=[pl.BlockSpec((tm, 1), lambda i: (i, 0))],
        out_specs=(spec,) * 5,
        compiler_params=_params("parallel"),
        name="rope_tables",
    )(pos)


def _rope128(x, cos, sin):
    return x * cos + pltpu.roll(x, 64, 1) * sin


def _rope64x2(x, cos, sa, sb):
    return x * cos + pltpu.roll(x, 96, 1) * sa + pltpu.roll(x, 32, 1) * sb


def _ffn_body(x, g_ref, wg_ref, wu_ref, wd_ref, o_ref):
    h = _rms(x, g_ref[...], D_MODEL).astype(BF16)
    acc = None
    for c in range(D_FF // FF_CHUNK):
        sl = slice(c * FF_CHUNK, (c + 1) * FF_CHUNK)
        a = (_silu(_dot(h, wg_ref[:, sl])) * _dot(h, wu_ref[:, sl])).astype(BF16)
        d = _dot(a, wd_ref[sl, :])
        acc = d if acc is None else acc + d
    o_ref[...] = x + 0.5 * acc


def _ffn_kernel(x_ref, g_ref, wg_ref, wu_ref, wd_ref, o_ref):
    _ffn_body(x_ref[...], g_ref, wg_ref, wu_ref, wd_ref, o_ref)


def _mix_ffn_kernel(x_ref, a_ref, wo_ref, g_ref, wg_ref, wu_ref, wd_ref, o_ref):
    _ffn_body(x_ref[...] + _dot(a_ref[...], wo_ref[...]), g_ref, wg_ref, wu_ref, wd_ref, o_ref)


def _ffn(x, g, wg, wu, wd, mix=None):
    m = x.shape[0]
    w_specs = [_resident((1, D_MODEL)), _resident((D_MODEL, D_FF)),
               _resident((D_MODEL, D_FF)), _resident((D_FF, D_MODEL))]
    if mix is None:
        body, args, specs = _ffn_kernel, (x,), [_rows(D_MODEL)]
    else:
        a, wo = mix
        body, args = _mix_ffn_kernel, (x, a, wo)
        specs = [_rows(D_MODEL), _rows(a.shape[1]), _resident(wo.shape)]
    return pl.pallas_call(
        body,
        out_shape=jax.ShapeDtypeStruct((m, D_MODEL), F32),
        grid=(m // TM,),
        in_specs=specs + w_specs,
        out_specs=_rows(D_MODEL),
        compiler_params=_params("parallel"),
        name="ffn" if mix is None else "mix_ffn",
    )(*args, g, wg, wu, wd)


def _ret_proj_kernel(x_ref, g_ref, w_ref, cos_ref, sin_ref, o_ref):
    h = _rms(x_ref[...], g_ref[...], D_MODEL).astype(BF16)
    cos = cos_ref[...]
    sin = sin_ref[...]
    for start, scale in ((0, None), (RET_QK, RET_DK ** -0.5)):
        p = _dot(h, w_ref[:, start:start + RET_QK])
        for hd in range(HEADS):
            r = _rope128(p[:, hd * RET_DK:(hd + 1) * RET_DK], cos, sin)
            if scale is not None:
                r = r * scale
            o_ref[:, start + hd * RET_DK:start + (hd + 1) * RET_DK] = r.astype(BF16)
    v0 = 2 * RET_QK
    g0 = v0 + HEADS * RET_DV
    o_ref[:, v0:g0] = _dot(h, w_ref[:, v0:g0]).astype(BF16)
    o_ref[:, g0:] = _silu(_dot(h, w_ref[:, g0:])).astype(BF16)


def _ret_proj(x, g, w, cos, sin):
    m = x.shape[0]
    return pl.pallas_call(
        _ret_proj_kernel,
        out_shape=jax.ShapeDtypeStruct((m, RET_PROJ), BF16),
        grid=(m // TM,),
        in_specs=[_rows(D_MODEL), _resident((1, D_MODEL)), _resident((D_MODEL, RET_PROJ)),
                  _rows(LANES), _rows(LANES)],
        out_specs=_rows(RET_PROJ),
        compiler_params=_params("parallel"),
        name="ret_proj",
    )(x, g, w, cos, sin)


def _retention_kernel(q_ref, k_ref, v_ref, gate_ref, gn_ref, o_ref, state_ref):
    c = RET_CHUNK
    seq = q_ref.shape[0]
    hd = pl.program_id(1).astype(F32)

    def log_decay(shape):
        return jnp.log1p(-jnp.exp2(jnp.full(shape, -5.0, F32) - hd))

    def rows(shape):
        return lax.broadcasted_iota(jnp.int32, shape, 0).astype(F32)

    diff = rows((c, c)) - lax.broadcasted_iota(jnp.int32, (c, c), 1).astype(F32)
    d_intra = jnp.where(diff >= 0, jnp.exp(log_decay((c, c)) * jnp.maximum(diff, 0.0)), 0.0)
    q_decay = jnp.exp(log_decay((c, RET_DV)) * (rows((c, RET_DV)) + 1.0))
    k_decay = jnp.exp(log_decay((c, RET_DK)) * (c - 1.0 - rows((c, RET_DK))))
    chunk_decay = jnp.exp(log_decay((RET_DK, RET_DV)) * c)
    gn = gn_ref[0]
    state_ref[...] = jnp.zeros_like(state_ref)

    def body(t, carry):
        off = pl.multiple_of(t * c, c)
        q = q_ref[pl.ds(off, c), :]
        k = k_ref[pl.ds(off, c), :]
        v = v_ref[pl.ds(off, c), :]
        state = state_ref[...]
        s = _dot_nt(q, k) * d_intra
        o = _dot(s.astype(BF16), v) + _dot(q, state.astype(BF16)) * q_decay
        kd = (k.astype(F32) * k_decay).astype(BF16)
        state_ref[...] = state * chunk_decay + _dot_tn(kd, v)
        mu = jnp.mean(o, axis=-1, keepdims=True)
        oc = o - mu
        var = jnp.mean(oc * oc, axis=-1, keepdims=True)
        on = oc * lax.rsqrt(var + EPS) * gn
        o_ref[pl.ds(off, c), :] = (gate_ref[pl.ds(off, c), :].astype(F32) * on).astype(BF16)
        return carry

    lax.fori_loop(0, seq // c, body, 0, unroll=4)


def _retention(proj, gn, batch, seq):
    m = proj.shape[0]
    kblk = RET_QK // RET_DK
    vblk = 2 * RET_QK // RET_DV
    gblk = vblk + HEADS
    return pl.pallas_call(
        _retention_kernel,
        out_shape=jax.ShapeDtypeStruct((m, HEADS * RET_DV), BF16),
        grid=(batch, HEADS),
        in_specs=[
            pl.BlockSpec((seq, RET_DK), lambda b, h: (b, h)),
            pl.BlockSpec((seq, RET_DK), lambda b, h: (b, kblk + h)),
            pl.BlockSpec((seq, RET_DV), lambda b, h: (b, vblk + h)),
            pl.BlockSpec((seq, RET_DV), lambda b, h: (b, gblk + h)),
            pl.BlockSpec((1, 1, RET_DV), lambda b, h: (h, 0, 0)),
        ],
        out_specs=pl.BlockSpec((seq, RET_DV), lambda b, h: (b, h)),
        scratch_shapes=[pltpu.VMEM((RET_DK, RET_DV), F32)],
        compiler_params=_params("parallel", "parallel"),
        name="retention",
    )(proj, proj, proj, proj, gn)


def _rope_tile_for_head(tile, hd):
    lane = lax.broadcasted_iota(jnp.int32, tile.shape, 1)
    keep = (lane < MLA_ROPE) if hd % 2 == 0 else (lane >= MLA_ROPE)
    return jnp.where(keep, tile, jnp.zeros_like(tile))


def _kv_kernel(x_ref, g_ref, wdc_ref, wdr_ref, lat_g_ref, wk_ref, wv_ref, kn_g_ref, kr_g_ref,
               cos_ref, sa_ref, sb_ref, k_out, v_out):
    h = _rms(x_ref[...], g_ref[...], D_MODEL).astype(BF16)
    lat = _rms(_dot(h, wdc_ref[...]), lat_g_ref[...], KV_LORA).astype(BF16)
    vv = _dot(lat, wv_ref[...]).astype(BF16)
    ones = jnp.ones((vv.shape[0], V_PAD - MLA_V), BF16)
    for hd in range(HEADS):
        v_out[:, hd * V_PAD:hd * V_PAD + MLA_V] = vv[:, hd * MLA_V:(hd + 1) * MLA_V]
        v_out[:, hd * V_PAD + MLA_V:(hd + 1) * V_PAD] = ones
    pe = _rms(_dot(h, wdr_ref[...]), kr_g_ref[...], LANES)
    pe = _rope64x2(pe, cos_ref[...], sa_ref[...], sb_ref[...]).astype(BF16)
    pe_tiles = (_rope_tile_for_head(pe, 0), _rope_tile_for_head(pe, 1))
    kk = _dot(lat, wk_ref[...])
    ones_nope = _group_ones(2 * MLA_NOPE, MLA_NOPE)
    kn_g = kn_g_ref[...]
    for pair in range(HEADS // 2):
        lo = pair * 2 * MLA_NOPE
        kn = _group_rms(kk[:, lo:lo + 2 * MLA_NOPE], ones_nope, MLA_NOPE, kn_g).astype(BF16)
        for sub in range(2):
            hd = 2 * pair + sub
            k_out[:, hd * MLA_QK_PAD:hd * MLA_QK_PAD + MLA_NOPE] = (
                kn[:, sub * MLA_NOPE:(sub + 1) * MLA_NOPE])
            k_out[:, hd * MLA_QK_PAD + MLA_NOPE:(hd + 1) * MLA_QK_PAD] = pe_tiles[sub]


def _shared_kv(x, g, wdc, wdr, lat_g, wk, wv, kn_g, kr_g, cos, sa, sb):
    m = x.shape[0]
    return pl.pallas_call(
        _kv_kernel,
        out_shape=(jax.ShapeDtypeStruct((m, HEADS * MLA_QK_PAD), BF16),
                   jax.ShapeDtypeStruct((m, HEADS * V_PAD), BF16)),
        grid=(m // TM,),
        in_specs=[_rows(D_MODEL), _resident((1, D_MODEL)), _resident((D_MODEL, KV_LORA)),
                  _resident((D_MODEL, LANES)), _resident((1, KV_LORA)),
                  _resident((KV_LORA, HEADS * MLA_NOPE)), _resident((KV_LORA, HEADS * MLA_V)),
                  _resident((1, 2 * MLA_NOPE)), _resident((1, LANES)),
                  _rows(LANES), _rows(LANES), _rows(LANES)],
        out_specs=(_rows(HEADS * MLA_QK_PAD), _rows(HEADS * V_PAD)),
        compiler_params=_params("parallel"),
        name="shared_kv",
    )(x, g, wdc, wdr, lat_g, wk, wv, kn_g, kr_g, cos, sa, sb)


Q_SCALE = (MLA_NOPE + MLA_ROPE) ** -0.5 * math.log2(math.e)
Q_NOPE_COLS = HEADS * MLA_NOPE


def _q_kernel(x_ref, g_ref, wdq_ref, lora_g_ref, wuq_ref, qn_g_ref, qr_g_ref,
              cos_ref, sa_ref, sb_ref, q_out):
    h = _rms(x_ref[...], g_ref[...], D_MODEL).astype(BF16)
    cq = _rms(_dot(h, wdq_ref[...]), lora_g_ref[...], Q_LORA).astype(BF16)
    q = _dot(cq, wuq_ref[...])
    ones_nope = _group_ones(2 * MLA_NOPE, MLA_NOPE)
    ones_rope = _group_ones(LANES, MLA_ROPE)
    qn_g = qn_g_ref[...] * Q_SCALE
    qr_g = qr_g_ref[...] * Q_SCALE
    cos = cos_ref[...]
    sa = sa_ref[...]
    sb = sb_ref[...]
    for pair in range(HEADS // 2):
        lo = pair * 2 * MLA_NOPE
        qn = _group_rms(q[:, lo:lo + 2 * MLA_NOPE], ones_nope, MLA_NOPE, qn_g).astype(BF16)
        lo = Q_NOPE_COLS + pair * LANES
        qp = _group_rms(q[:, lo:lo + LANES], ones_rope, MLA_ROPE, qr_g)
        qp = _rope64x2(qp, cos, sa, sb).astype(BF16)
        for sub in range(2):
            hd = 2 * pair + sub
            q_out[:, hd * MLA_QK_PAD:hd * MLA_QK_PAD + MLA_NOPE] = (
                qn[:, sub * MLA_NOPE:(sub + 1) * MLA_NOPE])
            q_out[:, hd * MLA_QK_PAD + MLA_NOPE:(hd + 1) * MLA_QK_PAD] = _rope_tile_for_head(qp, sub)


def _mla_q(x, g, wdq, lora_g, wuq, qn_g, qr_g, cos, sa, sb):
    m = x.shape[0]
    return pl.pallas_call(
        _q_kernel,
        out_shape=jax.ShapeDtypeStruct((m, HEADS * MLA_QK_PAD), BF16),
        grid=(m // TM,),
        in_specs=[_rows(D_MODEL), _resident((1, D_MODEL)), _resident((D_MODEL, Q_LORA)),
                  _resident((1, Q_LORA)), _resident(wuq.shape),
                  _resident((1, 2 * MLA_NOPE)), _resident((1, LANES)),
                  _rows(LANES), _rows(LANES), _rows(LANES)],
        out_specs=_rows(HEADS * MLA_QK_PAD),
        compiler_params=_params("parallel"),
        name="mla_q",
    )(x, g, wdq, lora_g, wuq, qn_g, qr_g, cos, sa, sb)


def _flash_kernel(q_ref, k_ref, v_ref, o_ref, s0_ref, s1_ref, m_ref, acc_ref):
    i = pl.program_id(2)
    m_ref[...] = jnp.full_like(m_ref, NEG_BIG)
    acc_ref[...] = jnp.zeros_like(acc_ref)

    def scores(j, s_ref):
        off = pl.multiple_of(j * TQ, TQ)
        for hd in range(HEADS_PER_STEP):
            qk = slice(hd * MLA_QK_PAD, (hd + 1) * MLA_QK_PAD)
            s_ref[hd] = _dot_nt(q_ref[:, qk], k_ref[pl.ds(off, TQ), qk])

    def consume(j, s_ref, masked=False):
        off = pl.multiple_of(j * TQ, TQ)
        for hd in range(HEADS_PER_STEP):
            s = s_ref[hd]
            if masked:
                row = lax.broadcasted_iota(jnp.int32, (TQ, TQ), 0)
                col = lax.broadcasted_iota(jnp.int32, (TQ, TQ), 1)
                s = jnp.where(col <= row, s, NEG_BIG)
            m_prev = m_ref[hd]
            m_new = jnp.maximum(m_prev, jnp.max(s, axis=1, keepdims=True))
            alpha = jnp.exp2(m_prev - m_new)
            p = jnp.exp2(s - _lane_tile(m_new, TQ // LANES))
            pv = _dot(p.astype(BF16), v_ref[pl.ds(off, TQ), hd * V_PAD:(hd + 1) * V_PAD])
            acc_ref[hd] = _lane_tile(alpha, V_PAD // LANES) * acc_ref[hd] + pv
            m_ref[hd] = m_new

    def pair(t, carry):
        j = 2 * t
        scores(j + 1, s1_ref)
        consume(j, s0_ref)
        scores(j + 2, s0_ref)
        consume(j + 1, s1_ref)
        return carry

    scores(0, s0_ref)
    lax.fori_loop(0, i // 2, pair, 0)

    @pl.when(i % 2 == 0)
    def _():
        consume(i, s0_ref, masked=True)

    @pl.when(i % 2 == 1)
    def _():
        scores(i, s1_ref)
        consume(i - 1, s0_ref)
        consume(i, s1_ref, masked=True)

    for hd in range(HEADS_PER_STEP):
        acc = acc_ref[hd]
        o_ref[:, hd * MLA_V:(hd + 1) * MLA_V] = (acc[:, :MLA_V] / acc[:, MLA_V:]).astype(BF16)


def _flash(q, k, v, batch, seq):
    m = q.shape[0]
    nq = seq // TQ
    g = HEADS_PER_STEP
    return pl.pallas_call(
        _flash_kernel,
        out_shape=jax.ShapeDtypeStruct((m, HEADS * MLA_V), BF16),
        grid=(batch, HEADS // g, nq),
        in_specs=[
            pl.BlockSpec((TQ, g * MLA_QK_PAD), lambda b, h, i: (b * nq + i, h)),
            pl.BlockSpec((seq, g * MLA_QK_PAD), lambda b, h, i: (b, h)),
            pl.BlockSpec((seq, g * V_PAD), lambda b, h, i: (b, h)),
        ],
        out_specs=pl.BlockSpec((TQ, g * MLA_V), lambda b, h, i: (b * nq + i, h)),
        scratch_shapes=[pltpu.VMEM((g, TQ, TQ), F32), pltpu.VMEM((g, TQ, TQ), F32),
                        pltpu.VMEM((g, TQ, LANES), F32),
                        pltpu.VMEM((g, TQ, V_PAD), F32)],
        compiler_params=_params("parallel", "parallel", "arbitrary"),
        name="flash_attention",
    )(q, k, v)


def _row(g, repeat=1):
    return jnp.tile(g.reshape(1, -1).astype(F32), (1, repeat))


def kernel(x, positions, norm_g, ffn_w_gate, ffn_w_up, ffn_w_down, ret_w_in, ret_gn_g, ret_w_o,
           kv_norm_g, kv_w_down, kv_latent_norm_g, kv_w_up, k_nope_norm_g, k_rope_norm_g,
           mla_w_dq, mla_q_lora_norm_g, mla_w_uq, mla_q_nope_norm_g, mla_q_rope_norm_g, mla_w_o):
    batch, seq, d = x.shape
    depth = norm_g.shape[0]
    n_self = ret_w_in.shape[0]
    m = batch * seq
    x = x.reshape(m, d)
    rcos, rsin, mcos, msa, msb = _rope_tables(positions.reshape(m, 1))
    wg_all = ffn_w_gate.astype(BF16)
    wu_all = ffn_w_up.astype(BF16)
    wd_all = ffn_w_down.astype(BF16)

    def ffn(x, layer, i, mix=None):
        return _ffn(x, _row(norm_g[layer, 2 * i]), wg_all[layer, i], wu_all[layer, i],
                    wd_all[layer, i], mix)

    k_shared = v_shared = None
    for layer in range(depth):
        x = ffn(x, layer, 0)
        g_mix = _row(norm_g[layer, 1])
        if layer < n_self:
            proj = _ret_proj(x, g_mix, ret_w_in[layer].astype(BF16), rcos, rsin)
            mixed = _retention(proj, ret_gn_g[layer].reshape(HEADS, 1, RET_DV), batch, seq)
            w_o = ret_w_o[layer].astype(BF16)
        else:
            j = layer - n_self
            wuq = mla_w_uq[j].reshape(Q_LORA, HEADS, MLA_NOPE + MLA_ROPE)
            wuq = jnp.concatenate([wuq[:, :, :MLA_NOPE].reshape(Q_LORA, Q_NOPE_COLS),
                                   wuq[:, :, MLA_NOPE:].reshape(Q_LORA, HEADS * MLA_ROPE)], axis=1)
            q = _mla_q(x, g_mix, mla_w_dq[j].astype(BF16), _row(mla_q_lora_norm_g[j]),
                       wuq.astype(BF16), _row(mla_q_nope_norm_g[j], 2), _row(mla_q_rope_norm_g[j], 2),
                       mcos, msa, msb)
            mixed = _flash(q, k_shared, v_shared, batch, seq)
            w_o = mla_w_o[j].astype(BF16)
        x = ffn(x, layer, 1, mix=(mixed, w_o))
        if layer == n_self - 1:
            wup = kv_w_up.reshape(KV_LORA, HEADS, MLA_NOPE + MLA_V)
            wk = wup[:, :, :MLA_NOPE].reshape(KV_LORA, HEADS * MLA_NOPE).astype(BF16)
            wv = wup[:, :, MLA_NOPE:].reshape(KV_LORA, HEADS * MLA_V).astype(BF16)
            wdr = jnp.tile(kv_w_down[:, KV_LORA:], (1, 2)).astype(BF16)
            k_shared, v_shared = _shared_kv(
                x, _row(kv_norm_g), kv_w_down[:, :KV_LORA].astype(BF16), wdr,
                _row(kv_latent_norm_g), wk, wv, _row(k_nope_norm_g, 2), _row(k_rope_norm_g, 2),
                mcos, msa, msb)
    return x.reshape(batch, seq, d)
```

```python
import math

import jax
import jax.numpy as jnp
from jax import lax
from jax.experimental import pallas as pl
from jax.experimental.pallas import tpu as pltpu

F32 = jnp.float32
BF16 = jnp.bfloat16

LANES = 128

D_MODEL = 1024
D_FF = 2816
HEADS = 8
RET_DK = 128
RET_DV = 256
RET_QK = HEADS * RET_DK
RET_PROJ = 2 * RET_QK + 2 * HEADS * RET_DV
MLA_NOPE = 128
MLA_ROPE = 64
MLA_V = 128
MLA_QK_PAD = 2 * LANES
V_PAD = 2 * LANES
HEADS_PER_STEP = 2
Q_LORA = 384
KV_LORA = 256
ROPE_BASE = 10000.0
EPS = 1e-6

VMEM_LIMIT = 56 * 1024 * 1024

TM = 512
TM_FFN = 1024
FF_CHUNK = 256
RET_CHUNK = 256
TQ = 512
NEG_BIG = -1e30


def _dot(a, b):
    return jnp.dot(a, b, preferred_element_type=F32)


def _dot_nt(a, b):
    return lax.dot_general(a, b, (((1,), (1,)), ((), ())), preferred_element_type=F32)


def _dot_tn(a, b):
    return lax.dot_general(a, b, (((0,), (0,)), ((), ())), preferred_element_type=F32)


def _rms(x, g, n):
    ms = jnp.sum(x * x, axis=-1, keepdims=True) * (1.0 / n)
    return x * lax.rsqrt(ms + EPS) * g


def _silu(x):
    return x / (1.0 + jnp.exp(-x))


def _lane_tile(x, n):
    return jnp.concatenate([x] * n, axis=1)


def _group_ones(width, group):
    shift = group.bit_length() - 1
    r = lax.broadcasted_iota(jnp.int32, (width, width), 0) >> shift
    c = lax.broadcasted_iota(jnp.int32, (width, width), 1) >> shift
    return jnp.where(r == c, 1.0, 0.0).astype(BF16)


def _group_rms(x, ones, group, g):
    ms = _dot((x * x).astype(BF16), ones) * (1.0 / group)
    return x * lax.rsqrt(ms + EPS) * g


def _params(*sem):
    return pltpu.CompilerParams(dimension_semantics=sem, vmem_limit_bytes=VMEM_LIMIT)


def _resident(shape):
    nd = len(shape)
    return pl.BlockSpec(shape, lambda *_: (0,) * nd, pipeline_mode=pl.Buffered(1))


def _rows(width, tm=TM):
    return pl.BlockSpec((tm, width), lambda i: (i, 0))


def _rope_tables_kernel(pos_ref, rcos_ref, rsin_ref, mcos_ref, msa_ref, msb_ref):
    pos = pos_ref[...].astype(F32)
    lane = lax.broadcasted_iota(jnp.int32, (1, LANES), 1)
    is_ret = lane < 64
    freq = jnp.where(is_ret, lane, lane & 31).astype(F32)
    step = jnp.where(is_ret, -2.0 / RET_DK, -2.0 / MLA_ROPE) * math.log(ROPE_BASE)
    ang = pos * jnp.exp(freq * step)
    c = jnp.cos(ang)
    s = jnp.sin(ang)
    rcos_ref[...] = jnp.where(is_ret, c, pltpu.roll(c, 64, 1))
    rsin_ref[...] = jnp.where(is_ret, -s, pltpu.roll(s, 64, 1))

    def spread(t):
        quarter = lane >> 5
        return jnp.where(quarter == 0, pltpu.roll(t, 64, 1),
                         jnp.where(quarter == 1, pltpu.roll(t, 96, 1),
                                   jnp.where(quarter == 2, t, pltpu.roll(t, 32, 1))))

    s = spread(s)
    first_half = (lane & 63) < 32
    mcos_ref[...] = spread(c)
    msa_ref[...] = jnp.where(first_half, -s, 0.0)
    msb_ref[...] = jnp.where(first_half, 0.0, s)


def _rope_tables(pos):
    m = pos.shape[0]
    tm = 1024
    tab = jax.ShapeDtypeStruct((m, LANES), F32)
    spec = pl.BlockSpec((tm, LANES), lambda i: (i, 0))
    return pl.pallas_call(
        _rope_tables_kernel,
        out_shape=(tab,) * 5,
        grid=(m // tm,),
        in_specs=[pl.BlockSpec((tm, 1), lambda i: (i, 0))],
        out_specs=(spec,) * 5,
        compiler_params=_params("parallel"),
        name="rope_tables",
    )(pos)


def _rope128(x, cos, sin):
    return x * cos + pltpu.roll(x, 64, 1) * sin


def _rope64x2(x, cos, sa, sb):
    return x * cos + pltpu.roll(x, 96, 1) * sa + pltpu.roll(x, 32, 1) * sb


def _ffn_body(x, g_ref, wg_ref, wu_ref, wd_ref, o_ref):
    h = _rms(x, g_ref[...], D_MODEL).astype(BF16)
    acc = None
    for c in range(D_FF // FF_CHUNK):
        sl = slice(c * FF_CHUNK, (c + 1) * FF_CHUNK)
        a = (_silu(_dot(h, wg_ref[:, sl])) * _dot(h, wu_ref[:, sl])).astype(BF16)
        d = _dot(a, wd_ref[sl, :])
        acc = d if acc is None else acc + d
    o_ref[...] = x + 0.5 * acc


def _ffn_kernel(x_ref, g_ref, wg_ref, wu_ref, wd_ref, o_ref):
    _ffn_body(x_ref[...], g_ref, wg_ref, wu_ref, wd_ref, o_ref)


def _mix_ffn_kernel(x_ref, a_ref, wo_ref, g_ref, wg_ref, wu_ref, wd_ref, o_ref):
    _ffn_body(x_ref[...] + _dot(a_ref[...], wo_ref[...]), g_ref, wg_ref, wu_ref, wd_ref, o_ref)


def _ffn(x, g, wg, wu, wd, layer, half, mix=None):
    m = x.shape[0]

    def stacked(rows, cols):
        return pl.BlockSpec((None, None, rows, cols), lambda i: (layer, half, 0, 0),
                            pipeline_mode=pl.Buffered(1))

    w_specs = [_resident((1, D_MODEL)), stacked(D_MODEL, D_FF), stacked(D_MODEL, D_FF),
               stacked(D_FF, D_MODEL)]
    if mix is None:
        tm = TM_FFN
        body, args, specs = _ffn_kernel, (x,), [_rows(D_MODEL, tm)]
    else:
        tm = TM
        a, wo = mix
        body, args = _mix_ffn_kernel, (x, a, wo)
        specs = [_rows(D_MODEL, tm), _rows(a.shape[1], tm), _resident(wo.shape)]
    return pl.pallas_call(
        body,
        out_shape=jax.ShapeDtypeStruct((m, D_MODEL), F32),
        grid=(m // tm,),
        in_specs=specs + w_specs,
        out_specs=_rows(D_MODEL, tm),
        compiler_params=_params("parallel"),
        name="ffn" if mix is None else "mix_ffn",
    )(*args, g, wg, wu, wd)


def _ret_proj_kernel(x_ref, g_ref, w_ref, cos_ref, sin_ref, o_ref, kt_ref):
    h = _rms(x_ref[...], g_ref[...], D_MODEL).astype(BF16)
    cos = cos_ref[...]
    sin = sin_ref[...]
    q = _dot(h, w_ref[:, :RET_QK])
    k = _dot(h, w_ref[:, RET_QK:2 * RET_QK])
    for hd in range(HEADS):
        head = slice(hd * RET_DK, (hd + 1) * RET_DK)
        o_ref[:, head] = _rope128(q[:, head], cos, sin).astype(BF16)
        kh = _rope128(k[:, head], cos, sin) * RET_DK ** -0.5
        for c in range(TM // RET_CHUNK):
            kt_ref[c, head, :] = kh[c * RET_CHUNK:(c + 1) * RET_CHUNK, :].T.astype(BF16)
    o_ref[:, RET_QK:] = _dot(h, w_ref[:, 2 * RET_QK:]).astype(BF16)


def _ret_proj(x, g, w, cos, sin):
    m = x.shape[0]
    chunks = TM // RET_CHUNK
    return pl.pallas_call(
        _ret_proj_kernel,
        out_shape=(jax.ShapeDtypeStruct((m, RET_PROJ - RET_QK), BF16),
                   jax.ShapeDtypeStruct((m // RET_CHUNK, RET_QK, RET_CHUNK), BF16)),
        grid=(m // TM,),
        in_specs=[_rows(D_MODEL), _resident((1, D_MODEL)), _resident((D_MODEL, RET_PROJ)),
                  _rows(LANES), _rows(LANES)],
        out_specs=(_rows(RET_PROJ - RET_QK),
                   pl.BlockSpec((chunks, RET_QK, RET_CHUNK), lambda i: (i, 0, 0))),
        compiler_params=_params("parallel"),
        name="ret_proj",
    )(x, g, w, cos, sin)


def _retention_kernel(q_ref, kt_ref, v_ref, gate_ref, gn_ref, o_ref, state_ref):
    c = RET_CHUNK
    seq = q_ref.shape[0]
    hd = pl.program_id(1).astype(F32)

    def log_decay(shape):
        return jnp.log1p(-jnp.exp2(jnp.full(shape, -5.0, F32) - hd))

    def index(shape, axis):
        return lax.broadcasted_iota(jnp.int32, shape, axis).astype(F32)

    diff = index((c, c), 0) - index((c, c), 1)
    d_intra = jnp.where(diff >= 0, jnp.exp(log_decay((c, c)) * jnp.maximum(diff, 0.0)), 0.0)
    q_decay = jnp.exp(log_decay((c, RET_DV)) * (index((c, RET_DV), 0) + 1.0))
    k_decay = jnp.exp(log_decay((RET_DK, c)) * (c - 1.0 - index((RET_DK, c), 1)))
    chunk_decay = jnp.exp(log_decay((RET_DK, RET_DV)) * c)
    gn = gn_ref[0]
    state_ref[...] = jnp.zeros_like(state_ref)

    def body(t, carry):
        off = pl.multiple_of(t * c, c)
        q = q_ref[pl.ds(off, c), :]
        kt = kt_ref[t]
        v = v_ref[pl.ds(off, c), :]
        state = state_ref[...]
        s = _dot(q, kt) * d_intra
        o = _dot(s.astype(BF16), v) + _dot(q, state.astype(BF16)) * q_decay
        ktd = (kt.astype(F32) * k_decay).astype(BF16)
        state_ref[...] = state * chunk_decay + _dot(ktd, v)
        mu = jnp.mean(o, axis=-1, keepdims=True)
        oc = o - mu
        var = jnp.mean(oc * oc, axis=-1, keepdims=True)
        on = oc * lax.rsqrt(var + EPS) * gn
        gate = _silu(gate_ref[pl.ds(off, c), :].astype(F32))
        o_ref[pl.ds(off, c), :] = (gate * on).astype(BF16)
        return carry

    lax.fori_loop(0, seq // c, body, 0, unroll=4)


def _retention(proj, kt, gn, batch, seq):
    m = proj.shape[0]
    vblk = RET_QK // RET_DV
    gblk = vblk + HEADS
    return pl.pallas_call(
        _retention_kernel,
        out_shape=jax.ShapeDtypeStruct((m, HEADS * RET_DV), BF16),
        grid=(batch, HEADS),
        in_specs=[
            pl.BlockSpec((seq, RET_DK), lambda b, h: (b, h)),
            pl.BlockSpec((seq // RET_CHUNK, RET_DK, RET_CHUNK), lambda b, h: (b, h, 0)),
            pl.BlockSpec((seq, RET_DV), lambda b, h: (b, vblk + h)),
            pl.BlockSpec((seq, RET_DV), lambda b, h: (b, gblk + h)),
            pl.BlockSpec((1, 1, RET_DV), lambda b, h: (h, 0, 0)),
        ],
        out_specs=pl.BlockSpec((seq, RET_DV), lambda b, h: (b, h)),
        scratch_shapes=[pltpu.VMEM((RET_DK, RET_DV), F32)],
        compiler_params=_params("parallel", "parallel"),
        name="retention",
    )(proj, kt, proj, proj, gn)


def _rope_tile_for_head(tile, hd):
    lane = lax.broadcasted_iota(jnp.int32, tile.shape, 1)
    keep = (lane < MLA_ROPE) if hd % 2 == 0 else (lane >= MLA_ROPE)
    return jnp.where(keep, tile, jnp.zeros_like(tile))


def _kv_kernel(x_ref, g_ref, wdc_ref, wdr_ref, lat_g_ref, wk_ref, wv_ref, kn_g_ref, kr_g_ref,
               cos_ref, sa_ref, sb_ref, k_out, v_out):
    h = _rms(x_ref[...], g_ref[...], D_MODEL).astype(BF16)
    lat = _rms(_dot(h, wdc_ref[...]), lat_g_ref[...], KV_LORA).astype(BF16)
    vv = _dot(lat, wv_ref[...]).astype(BF16)
    ones = jnp.ones((vv.shape[0], V_PAD - MLA_V), BF16)
    for hd in range(HEADS):
        v_out[:, hd * V_PAD:hd * V_PAD + MLA_V] = vv[:, hd * MLA_V:(hd + 1) * MLA_V]
        v_out[:, hd * V_PAD + MLA_V:(hd + 1) * V_PAD] = ones
    pe = _rms(_dot(h, wdr_ref[...]), kr_g_ref[...], LANES)
    pe = _rope64x2(pe, cos_ref[...], sa_ref[...], sb_ref[...]).astype(BF16)
    pe_tiles = (_rope_tile_for_head(pe, 0), _rope_tile_for_head(pe, 1))
    kk = _dot(lat, wk_ref[...])
    ones_nope = _group_ones(2 * MLA_NOPE, MLA_NOPE)
    kn_g = kn_g_ref[...]
    for pair in range(HEADS // 2):
        lo = pair * 2 * MLA_NOPE
        kn = _group_rms(kk[:, lo:lo + 2 * MLA_NOPE], ones_nope, MLA_NOPE, kn_g).astype(BF16)
        for sub in range(2):
            hd = 2 * pair + sub
            k_out[:, hd * MLA_QK_PAD:hd * MLA_QK_PAD + MLA_NOPE] = (
                kn[:, sub * MLA_NOPE:(sub + 1) * MLA_NOPE])
            k_out[:, hd * MLA_QK_PAD + MLA_NOPE:(hd + 1) * MLA_QK_PAD] = pe_tiles[sub]


def _shared_kv(x, g, wdc, wdr, lat_g, wk, wv, kn_g, kr_g, cos, sa, sb):
    m = x.shape[0]
    return pl.pallas_call(
        _kv_kernel,
        out_shape=(jax.ShapeDtypeStruct((m, HEADS * MLA_QK_PAD), BF16),
                   jax.ShapeDtypeStruct((m, HEADS * V_PAD), BF16)),
        grid=(m // TM,),
        in_specs=[_rows(D_MODEL), _resident((1, D_MODEL)), _resident((D_MODEL, KV_LORA)),
                  _resident((D_MODEL, LANES)), _resident((1, KV_LORA)),
                  _resident((KV_LORA, HEADS * MLA_NOPE)), _resident((KV_LORA, HEADS * MLA_V)),
                  _resident((1, 2 * MLA_NOPE)), _resident((1, LANES)),
                  _rows(LANES), _rows(LANES), _rows(LANES)],
        out_specs=(_rows(HEADS * MLA_QK_PAD), _rows(HEADS * V_PAD)),
        compiler_params=_params("parallel"),
        name="shared_kv",
    )(x, g, wdc, wdr, lat_g, wk, wv, kn_g, kr_g, cos, sa, sb)


Q_SCALE = (MLA_NOPE + MLA_ROPE) ** -0.5 * math.log2(math.e)
Q_NOPE_COLS = HEADS * MLA_NOPE


def _q_kernel(x_ref, g_ref, wdq_ref, lora_g_ref, wuq_ref, qn_g_ref, qr_g_ref,
              cos_ref, sa_ref, sb_ref, q_out):
    h = _rms(x_ref[...], g_ref[...], D_MODEL).astype(BF16)
    cq = _rms(_dot(h, wdq_ref[...]), lora_g_ref[...], Q_LORA).astype(BF16)
    q = _dot(cq, wuq_ref[...])
    ones_nope = _group_ones(2 * MLA_NOPE, MLA_NOPE)
    ones_rope = _group_ones(LANES, MLA_ROPE)
    qn_g = qn_g_ref[...] * Q_SCALE
    qr_g = qr_g_ref[...] * Q_SCALE
    cos = cos_ref[...]
    sa = sa_ref[...]
    sb = sb_ref[...]
    for pair in range(HEADS // 2):
        lo = pair * 2 * MLA_NOPE
        qn = _group_rms(q[:, lo:lo + 2 * MLA_NOPE], ones_nope, MLA_NOPE, qn_g).astype(BF16)
        lo = Q_NOPE_COLS + pair * LANES
        qp = _group_rms(q[:, lo:lo + LANES], ones_rope, MLA_ROPE, qr_g)
        qp = _rope64x2(qp, cos, sa, sb).astype(BF16)
        for sub in range(2):
            hd = 2 * pair + sub
            q_out[:, hd * MLA_QK_PAD:hd * MLA_QK_PAD + MLA_NOPE] = (
                qn[:, sub * MLA_NOPE:(sub + 1) * MLA_NOPE])
            q_out[:, hd * MLA_QK_PAD + MLA_NOPE:(hd + 1) * MLA_QK_PAD] = _rope_tile_for_head(qp, sub)


def _mla_q(x, g, wdq, lora_g, wuq, qn_g, qr_g, cos, sa, sb):
    m = x.shape[0]
    return pl.pallas_call(
        _q_kernel,
        out_shape=jax.ShapeDtypeStruct((m, HEADS * MLA_QK_PAD), BF16),
        grid=(m // TM,),
        in_specs=[_rows(D_MODEL), _resident((1, D_MODEL)), _resident((D_MODEL, Q_LORA)),
                  _resident((1, Q_LORA)), _resident(wuq.shape),
                  _resident((1, 2 * MLA_NOPE)), _resident((1, LANES)),
                  _rows(LANES), _rows(LANES), _rows(LANES)],
        out_specs=_rows(HEADS * MLA_QK_PAD),
        compiler_params=_params("parallel"),
        name="mla_q",
    )(x, g, wdq, lora_g, wuq, qn_g, qr_g, cos, sa, sb)


def _flash_kernel(q_ref, k_ref, v_ref, o_ref, s0_ref, s1_ref, m_ref, acc_ref):
    i = pl.program_id(2)
    m_ref[...] = jnp.full_like(m_ref, NEG_BIG)
    acc_ref[...] = jnp.zeros_like(acc_ref)

    def scores(j, s_ref):
        off = pl.multiple_of(j * TQ, TQ)
        for hd in range(HEADS_PER_STEP):
            qk = slice(hd * MLA_QK_PAD, (hd + 1) * MLA_QK_PAD)
            s_ref[hd] = _dot_nt(q_ref[:, qk], k_ref[pl.ds(off, TQ), qk])

    def consume(j, s_ref, masked=False):
        off = pl.multiple_of(j * TQ, TQ)
        for hd in range(HEADS_PER_STEP):
            s = s_ref[hd]
            if masked:
                row = lax.broadcasted_iota(jnp.int32, (TQ, TQ), 0)
                col = lax.broadcasted_iota(jnp.int32, (TQ, TQ), 1)
                s = jnp.where(col <= row, s, NEG_BIG)
            m_prev = m_ref[hd]
            m_new = jnp.maximum(m_prev, jnp.max(s, axis=1, keepdims=True))
            alpha = jnp.exp2(m_prev - m_new)
            p = jnp.exp2(s - _lane_tile(m_new, TQ // LANES))
            pv = _dot(p.astype(BF16), v_ref[pl.ds(off, TQ), hd * V_PAD:(hd + 1) * V_PAD])
            acc_ref[hd] = _lane_tile(alpha, V_PAD // LANES) * acc_ref[hd] + pv
            m_ref[hd] = m_new

    def pair(t, carry):
        j = 2 * t
        scores(j + 1, s1_ref)
        consume(j, s0_ref)
        scores(j + 2, s0_ref)
        consume(j + 1, s1_ref)
        return carry

    scores(0, s0_ref)
    lax.fori_loop(0, i // 2, pair, 0)

    @pl.when(i % 2 == 0)
    def _():
        consume(i, s0_ref, masked=True)

    @pl.when(i % 2 == 1)
    def _():
        scores(i, s1_ref)
        consume(i - 1, s0_ref)
        consume(i, s1_ref, masked=True)

    for hd in range(HEADS_PER_STEP):
        acc = acc_ref[hd]
        o_ref[:, hd * MLA_V:(hd + 1) * MLA_V] = (acc[:, :MLA_V] / acc[:, MLA_V:]).astype(BF16)


def _flash(q, k, v, batch, seq):
    m = q.shape[0]
    nq = seq // TQ
    g = HEADS_PER_STEP
    return pl.pallas_call(
        _flash_kernel,
        out_shape=jax.ShapeDtypeStruct((m, HEADS * MLA_V), BF16),
        grid=(batch, HEADS // g, nq),
        in_specs=[
            pl.BlockSpec((TQ, g * MLA_QK_PAD), lambda b, h, i: (b * nq + i, h)),
            pl.BlockSpec((seq, g * MLA_QK_PAD), lambda b, h, i: (b, h)),
            pl.BlockSpec((seq, g * V_PAD), lambda b, h, i: (b, h)),
        ],
        out_specs=pl.BlockSpec((TQ, g * MLA_V), lambda b, h, i: (b * nq + i, h)),
        scratch_shapes=[pltpu.VMEM((g, TQ, TQ), F32), pltpu.VMEM((g, TQ, TQ), F32),
                        pltpu.VMEM((g, TQ, LANES), F32),
                        pltpu.VMEM((g, TQ, V_PAD), F32)],
        compiler_params=_params("parallel", "parallel", "arbitrary"),
        name="flash_attention",
    )(q, k, v)


def _row(g, repeat=1):
    return jnp.tile(g.reshape(1, -1).astype(F32), (1, repeat))


def kernel(x, positions, norm_g, ffn_w_gate, ffn_w_up, ffn_w_down, ret_w_in, ret_gn_g, ret_w_o,
           kv_norm_g, kv_w_down, kv_latent_norm_g, kv_w_up, k_nope_norm_g, k_rope_norm_g,
           mla_w_dq, mla_q_lora_norm_g, mla_w_uq, mla_q_nope_norm_g, mla_q_rope_norm_g, mla_w_o):
    batch, seq, d = x.shape
    depth = norm_g.shape[0]
    n_self = ret_w_in.shape[0]
    m = batch * seq
    x = x.reshape(m, d)
    rcos, rsin, mcos, msa, msb = _rope_tables(positions.reshape(m, 1))
    wg_all = ffn_w_gate.astype(BF16)
    wu_all = ffn_w_up.astype(BF16)
    wd_all = ffn_w_down.astype(BF16)

    def ffn(x, layer, i, mix=None):
        return _ffn(x, _row(norm_g[layer, 2 * i]), wg_all, wu_all, wd_all, layer, i, mix)

    k_shared = v_shared = None
    for layer in range(depth):
        x = ffn(x, layer, 0)
        g_mix = _row(norm_g[layer, 1])
        if layer < n_self:
            proj, kt = _ret_proj(x, g_mix, ret_w_in[layer].astype(BF16), rcos, rsin)
            mixed = _retention(proj, kt, ret_gn_g[layer].reshape(HEADS, 1, RET_DV), batch, seq)
            w_o = ret_w_o[layer].astype(BF16)
        else:
            j = layer - n_self
            wuq = mla_w_uq[j].reshape(Q_LORA, HEADS, MLA_NOPE + MLA_ROPE)
            wuq = jnp.concatenate([wuq[:, :, :MLA_NOPE].reshape(Q_LORA, Q_NOPE_COLS),
                                   wuq[:, :, MLA_NOPE:].reshape(Q_LORA, HEADS * MLA_ROPE)], axis=1)
            q = _mla_q(x, g_mix, mla_w_dq[j].astype(BF16), _row(mla_q_lora_norm_g[j]),
                       wuq.astype(BF16), _row(mla_q_nope_norm_g[j], 2), _row(mla_q_rope_norm_g[j], 2),
                       mcos, msa, msb)
            mixed = _flash(q, k_shared, v_shared, batch, seq)
            w_o = mla_w_o[j].astype(BF16)
        x = ffn(x, layer, 1, mix=(mixed, w_o))
        if layer == n_self - 1:
            wup = kv_w_up.reshape(KV_LORA, HEADS, MLA_NOPE + MLA_V)
            wk = wup[:, :, :MLA_NOPE].reshape(KV_LORA, HEADS * MLA_NOPE).astype(BF16)
            wv = wup[:, :, MLA_NOPE:].reshape(KV_LORA, HEADS * MLA_V).astype(BF16)
            wdr = jnp.tile(kv_w_down[:, KV_LORA:], (1, 2)).astype(BF16)
            k_shared, v_shared = _shared_kv(
                x, _row(kv_norm_g), kv_w_down[:, :KV_LORA].astype(BF16), wdr,
                _row(kv_latent_norm_g), wk, wv, _row(k_nope_norm_g, 2), _row(k_rope_norm_g, 2),
                mcos, msa, msb)
    return x.reshape(batch, seq, d)
```

```python
import math

import jax
import jax.numpy as jnp
from jax import lax
from jax.experimental import pallas as pl
from jax.experimental.pallas import tpu as pltpu

F32 = jnp.float32
BF16 = jnp.bfloat16

LANES = 128

D_MODEL = 1024
D_FF = 2816
HEADS = 8
RET_DK = 128
RET_DV = 256
RET_QK = HEADS * RET_DK
RET_PROJ = 2 * RET_QK + 2 * HEADS * RET_DV
MLA_NOPE = 128
MLA_ROPE = 64
MLA_V = 128
MLA_QK_PAD = 2 * LANES
V_PAD = 2 * LANES
HEADS_PER_STEP = 2
Q_LORA = 384
KV_LORA = 256
ROPE_BASE = 10000.0
EPS = 1e-6

VMEM_LIMIT = 56 * 1024 * 1024

TM = 512
TM_FFN = 1024
FF_CHUNK = 256
RET_CHUNK = 256
RET_HEADS_PER_STEP = 2
TQ = 512
NEG_BIG = -1e30


def _dot(a, b):
    return jnp.dot(a, b, preferred_element_type=F32)


def _dot_nt(a, b):
    return lax.dot_general(a, b, (((1,), (1,)), ((), ())), preferred_element_type=F32)


def _dot_tn(a, b):
    return lax.dot_general(a, b, (((0,), (0,)), ((), ())), preferred_element_type=F32)


def _rms(x, g, n):
    ms = jnp.sum(x * x, axis=-1, keepdims=True) * (1.0 / n)
    return x * lax.rsqrt(ms + EPS) * g


def _silu(x):
    return x / (1.0 + jnp.exp2(x * -math.log2(math.e)))


def _lane_tile(x, n):
    return jnp.concatenate([x] * n, axis=1)


def _group_ones(width, group):
    shift = group.bit_length() - 1
    r = lax.broadcasted_iota(jnp.int32, (width, width), 0) >> shift
    c = lax.broadcasted_iota(jnp.int32, (width, width), 1) >> shift
    return jnp.where(r == c, 1.0, 0.0).astype(BF16)


def _group_rms(x, ones, group, g):
    ms = _dot((x * x).astype(BF16), ones) * (1.0 / group)
    return x * lax.rsqrt(ms + EPS) * g


def _params(*sem):
    return pltpu.CompilerParams(dimension_semantics=sem, vmem_limit_bytes=VMEM_LIMIT)


def _resident(shape):
    nd = len(shape)
    return pl.BlockSpec(shape, lambda *_: (0,) * nd, pipeline_mode=pl.Buffered(1))


def _rows(width, tm=TM):
    return pl.BlockSpec((tm, width), lambda i: (i, 0))


def _rope_tables_kernel(pos_ref, rcos_ref, rsin_ref, mcos_ref, msa_ref, msb_ref):
    pos = pos_ref[...].astype(F32)
    lane = lax.broadcasted_iota(jnp.int32, (1, LANES), 1)
    is_ret = lane < 64
    freq = jnp.where(is_ret, lane, lane & 31).astype(F32)
    step = jnp.where(is_ret, -2.0 / RET_DK, -2.0 / MLA_ROPE) * math.log(ROPE_BASE)
    ang = pos * jnp.exp(freq * step)
    c = jnp.cos(ang)
    s = jnp.sin(ang)
    rcos_ref[...] = jnp.where(is_ret, c, pltpu.roll(c, 64, 1))
    rsin_ref[...] = jnp.where(is_ret, -s, pltpu.roll(s, 64, 1))

    def spread(t):
        quarter = lane >> 5
        return jnp.where(quarter == 0, pltpu.roll(t, 64, 1),
                         jnp.where(quarter == 1, pltpu.roll(t, 96, 1),
                                   jnp.where(quarter == 2, t, pltpu.roll(t, 32, 1))))

    s = spread(s)
    first_half = (lane & 63) < 32
    mcos_ref[...] = spread(c)
    msa_ref[...] = jnp.where(first_half, -s, 0.0)
    msb_ref[...] = jnp.where(first_half, 0.0, s)


def _rope_tables(pos):
    m = pos.shape[0]
    tm = 1024
    tab = jax.ShapeDtypeStruct((m, LANES), F32)
    spec = pl.BlockSpec((tm, LANES), lambda i: (i, 0))
    return pl.pallas_call(
        _rope_tables_kernel,
        out_shape=(tab,) * 5,
        grid=(m // tm,),
        in_specs=[pl.BlockSpec((tm, 1), lambda i: (i, 0))],
        out_specs=(spec,) * 5,
        compiler_params=_params("parallel"),
        name="rope_tables",
    )(pos)


def _rope128(x, cos, sin):
    return x * cos + pltpu.roll(x, 64, 1) * sin


def _rope64x2(x, cos, sa, sb):
    return x * cos + pltpu.roll(x, 96, 1) * sa + pltpu.roll(x, 32, 1) * sb


def _ffn_body(x, g_ref, wg_ref, wu_ref, wd_ref, o_ref):
    h = _rms(x, g_ref[...], D_MODEL).astype(BF16)
    acc = None
    for c in range(D_FF // FF_CHUNK):
        sl = slice(c * FF_CHUNK, (c + 1) * FF_CHUNK)
        a = (_silu(_dot(h, wg_ref[:, sl])) * _dot(h, wu_ref[:, sl])).astype(BF16)
        d = _dot(a, wd_ref[sl, :])
        acc = d if acc is None else acc + d
    o_ref[...] = x + 0.5 * acc


def _ffn_kernel(x_ref, g_ref, wg_ref, wu_ref, wd_ref, o_ref):
    _ffn_body(x_ref[...], g_ref, wg_ref, wu_ref, wd_ref, o_ref)


def _mix_ffn_kernel(x_ref, a_ref, wo_ref, g_ref, wg_ref, wu_ref, wd_ref, o_ref):
    _ffn_body(x_ref[...] + _dot(a_ref[...], wo_ref[...]), g_ref, wg_ref, wu_ref, wd_ref, o_ref)


def _ffn(x, g, wg, wu, wd, layer, half, mix=None):
    m = x.shape[0]

    def stacked(rows, cols):
        return pl.BlockSpec((None, None, rows, cols), lambda i: (layer, half, 0, 0),
                            pipeline_mode=pl.Buffered(1))

    w_specs = [_resident((1, D_MODEL)), stacked(D_MODEL, D_FF), stacked(D_MODEL, D_FF),
               stacked(D_FF, D_MODEL)]
    if mix is None:
        tm = TM_FFN
        body, args, specs = _ffn_kernel, (x,), [_rows(D_MODEL, tm)]
    else:
        tm = TM
        a, wo = mix
        body, args = _mix_ffn_kernel, (x, a, wo)
        specs = [_rows(D_MODEL, tm), _rows(a.shape[1], tm), _resident(wo.shape)]
    return pl.pallas_call(
        body,
        out_shape=jax.ShapeDtypeStruct((m, D_MODEL), F32),
        grid=(m // tm,),
        in_specs=specs + w_specs,
        out_specs=_rows(D_MODEL, tm),
        compiler_params=_params("parallel"),
        name="ffn" if mix is None else "mix_ffn",
    )(*args, g, wg, wu, wd)


def _ret_proj_kernel(x_ref, g_ref, w_ref, cos_ref, sin_ref, o_ref, kt_ref):
    h = _rms(x_ref[...], g_ref[...], D_MODEL).astype(BF16)
    cos = cos_ref[...]
    sin = sin_ref[...]
    q = _dot(h, w_ref[:, :RET_QK])
    k = _dot(h, w_ref[:, RET_QK:2 * RET_QK])
    for hd in range(HEADS):
        head = slice(hd * RET_DK, (hd + 1) * RET_DK)
        o_ref[:, head] = _rope128(q[:, head], cos, sin).astype(BF16)
        kh = _rope128(k[:, head], cos, sin) * RET_DK ** -0.5
        for c in range(TM // RET_CHUNK):
            kt_ref[c, head, :] = kh[c * RET_CHUNK:(c + 1) * RET_CHUNK, :].T.astype(BF16)
    o_ref[:, RET_QK:] = _dot(h, w_ref[:, 2 * RET_QK:]).astype(BF16)


def _ret_proj(x, g, w, cos, sin):
    m = x.shape[0]
    chunks = TM // RET_CHUNK
    return pl.pallas_call(
        _ret_proj_kernel,
        out_shape=(jax.ShapeDtypeStruct((m, RET_PROJ - RET_QK), BF16),
                   jax.ShapeDtypeStruct((m // RET_CHUNK, RET_QK, RET_CHUNK), BF16)),
        grid=(m // TM,),
        in_specs=[_rows(D_MODEL), _resident((1, D_MODEL)), _resident((D_MODEL, RET_PROJ)),
                  _rows(LANES), _rows(LANES)],
        out_specs=(_rows(RET_PROJ - RET_QK),
                   pl.BlockSpec((chunks, RET_QK, RET_CHUNK), lambda i: (i, 0, 0))),
        compiler_params=_params("parallel"),
        name="ret_proj",
    )(x, g, w, cos, sin)


def _retention_kernel(q_ref, kt_ref, v_ref, gate_ref, gn_ref, o_ref, state_ref):
    c = RET_CHUNK
    seq = q_ref.shape[0]

    def index(shape, axis):
        return lax.broadcasted_iota(jnp.int32, shape, axis).astype(F32)

    def decays(sub):
        hd = (pl.program_id(1) * RET_HEADS_PER_STEP + sub).astype(F32)

        def log_decay(shape):
            return jnp.log1p(-jnp.exp2(jnp.full(shape, -5.0, F32) - hd))

        diff = index((c, c), 0) - index((c, c), 1)
        d_intra = jnp.where(diff >= 0, jnp.exp(log_decay((c, c)) * jnp.maximum(diff, 0.0)), 0.0)
        q_decay = jnp.exp(log_decay((c, RET_DV)) * (index((c, RET_DV), 0) + 1.0))
        k_decay = jnp.exp(log_decay((RET_DK, c)) * (c - 1.0 - index((RET_DK, c), 1)))
        chunk_decay = jnp.exp(log_decay((RET_DK, RET_DV)) * c)
        return d_intra, q_decay, k_decay, chunk_decay

    per_head = [decays(sub) for sub in range(RET_HEADS_PER_STEP)]
    state_ref[...] = jnp.zeros_like(state_ref)

    def body(t, carry):
        off = pl.multiple_of(t * c, c)
        for sub, (d_intra, q_decay, k_decay, chunk_decay) in enumerate(per_head):
            qk = slice(sub * RET_DK, (sub + 1) * RET_DK)
            vo = slice(sub * RET_DV, (sub + 1) * RET_DV)
            q = q_ref[pl.ds(off, c), qk]
            kt = kt_ref[t, qk, :]
            v = v_ref[pl.ds(off, c), vo]
            state = state_ref[sub]
            s = _dot(q, kt) * d_intra
            o = _dot(s.astype(BF16), v) + _dot(q, state.astype(BF16)) * q_decay
            ktd = (kt.astype(F32) * k_decay).astype(BF16)
            state_ref[sub] = state * chunk_decay + _dot(ktd, v)
            mu = jnp.mean(o, axis=-1, keepdims=True)
            oc = o - mu
            var = jnp.mean(oc * oc, axis=-1, keepdims=True)
            on = oc * lax.rsqrt(var + EPS) * gn_ref[sub]
            gate = _silu(gate_ref[pl.ds(off, c), vo].astype(F32))
            o_ref[pl.ds(off, c), vo] = (gate * on).astype(BF16)
        return carry

    lax.fori_loop(0, seq // c, body, 0, unroll=4)


def _retention(proj, kt, gn, batch, seq):
    m = proj.shape[0]
    g = RET_HEADS_PER_STEP
    groups = HEADS // g
    vblk = RET_QK // (g * RET_DV)
    gblk = vblk + groups
    return pl.pallas_call(
        _retention_kernel,
        out_shape=jax.ShapeDtypeStruct((m, HEADS * RET_DV), BF16),
        grid=(batch, groups),
        in_specs=[
            pl.BlockSpec((seq, g * RET_DK), lambda b, h: (b, h)),
            pl.BlockSpec((seq // RET_CHUNK, g * RET_DK, RET_CHUNK), lambda b, h: (b, h, 0)),
            pl.BlockSpec((seq, g * RET_DV), lambda b, h: (b, vblk + h)),
            pl.BlockSpec((seq, g * RET_DV), lambda b, h: (b, gblk + h)),
            pl.BlockSpec((g, 1, RET_DV), lambda b, h: (h, 0, 0)),
        ],
        out_specs=pl.BlockSpec((seq, g * RET_DV), lambda b, h: (b, h)),
        scratch_shapes=[pltpu.VMEM((g, RET_DK, RET_DV), F32)],
        compiler_params=_params("parallel", "parallel"),
        name="retention",
    )(proj, kt, proj, proj, gn)


def _rope_tile_for_head(tile, hd):
    lane = lax.broadcasted_iota(jnp.int32, tile.shape, 1)
    keep = (lane < MLA_ROPE) if hd % 2 == 0 else (lane >= MLA_ROPE)
    return jnp.where(keep, tile, jnp.zeros_like(tile))


def _kv_kernel(x_ref, g_ref, wdc_ref, wdr_ref, lat_g_ref, wk_ref, wv_ref, kn_g_ref, kr_g_ref,
               cos_ref, sa_ref, sb_ref, k_out, v_out):
    h = _rms(x_ref[...], g_ref[...], D_MODEL).astype(BF16)
    lat = _rms(_dot(h, wdc_ref[...]), lat_g_ref[...], KV_LORA).astype(BF16)
    pe = _rms(_dot(h, wdr_ref[...]), kr_g_ref[...], LANES)
    pe = _rope64x2(pe, cos_ref[...], sa_ref[...], sb_ref[...]).astype(BF16)
    pe_tiles = (_rope_tile_for_head(pe, 0), _rope_tile_for_head(pe, 1))
    ones = jnp.ones((pe.shape[0], V_PAD - MLA_V), BF16)
    ones_nope = _group_ones(2 * MLA_NOPE, MLA_NOPE)
    kn_g = kn_g_ref[...]
    for pair in range(HEADS // 2):
        lo = pair * 2 * MLA_NOPE
        kn = _group_rms(_dot(lat, wk_ref[:, lo:lo + 2 * MLA_NOPE]), ones_nope, MLA_NOPE, kn_g)
        kn = kn.astype(BF16)
        vv = _dot(lat, wv_ref[:, pair * 2 * MLA_V:(pair + 1) * 2 * MLA_V]).astype(BF16)
        for sub in range(2):
            hd = 2 * pair + sub
            k_out[:, hd * MLA_QK_PAD:hd * MLA_QK_PAD + MLA_NOPE] = (
                kn[:, sub * MLA_NOPE:(sub + 1) * MLA_NOPE])
            k_out[:, hd * MLA_QK_PAD + MLA_NOPE:(hd + 1) * MLA_QK_PAD] = pe_tiles[sub]
            v_out[:, hd * V_PAD:hd * V_PAD + MLA_V] = vv[:, sub * MLA_V:(sub + 1) * MLA_V]
            v_out[:, hd * V_PAD + MLA_V:(hd + 1) * V_PAD] = ones


def _shared_kv(x, g, wdc, wdr, lat_g, wk, wv, kn_g, kr_g, cos, sa, sb):
    m = x.shape[0]
    return pl.pallas_call(
        _kv_kernel,
        out_shape=(jax.ShapeDtypeStruct((m, HEADS * MLA_QK_PAD), BF16),
                   jax.ShapeDtypeStruct((m, HEADS * V_PAD), BF16)),
        grid=(m // TM,),
        in_specs=[_rows(D_MODEL), _resident((1, D_MODEL)), _resident((D_MODEL, KV_LORA)),
                  _resident((D_MODEL, LANES)), _resident((1, KV_LORA)),
                  _resident((KV_LORA, HEADS * MLA_NOPE)), _resident((KV_LORA, HEADS * MLA_V)),
                  _resident((1, 2 * MLA_NOPE)), _resident((1, LANES)),
                  _rows(LANES), _rows(LANES), _rows(LANES)],
        out_specs=(_rows(HEADS * MLA_QK_PAD), _rows(HEADS * V_PAD)),
        compiler_params=_params("parallel"),
        name="shared_kv",
    )(x, g, wdc, wdr, lat_g, wk, wv, kn_g, kr_g, cos, sa, sb)


Q_SCALE = (MLA_NOPE + MLA_ROPE) ** -0.5 * math.log2(math.e)
Q_NOPE_COLS = HEADS * MLA_NOPE


def _q_kernel(x_ref, g_ref, wdq_ref, lora_g_ref, wuq_ref, qn_g_ref, qr_g_ref,
              cos_ref, sa_ref, sb_ref, q_out):
    h = _rms(x_ref[...], g_ref[...], D_MODEL).astype(BF16)
    cq = _rms(_dot(h, wdq_ref[...]), lora_g_ref[...], Q_LORA).astype(BF16)
    q = _dot(cq, wuq_ref[...])
    ones_nope = _group_ones(2 * MLA_NOPE, MLA_NOPE)
    ones_rope = _group_ones(2 * LANES, MLA_ROPE)
    qn_g = qn_g_ref[...] * Q_SCALE
    qr_g = qr_g_ref[...] * Q_SCALE
    cos = cos_ref[...]
    sa = sa_ref[...]
    sb = sb_ref[...]
    for quad in range(HEADS // 4):
        lo = Q_NOPE_COLS + quad * 2 * LANES
        qp = _group_rms(q[:, lo:lo + 2 * LANES], ones_rope, MLA_ROPE, qr_g)
        for half in range(2):
            pair = 2 * quad + half
            lo = pair * 2 * MLA_NOPE
            qn = _group_rms(q[:, lo:lo + 2 * MLA_NOPE], ones_nope, MLA_NOPE, qn_g).astype(BF16)
            tile = _rope64x2(qp[:, half * LANES:(half + 1) * LANES], cos, sa, sb).astype(BF16)
            for sub in range(2):
                hd = 2 * pair + sub
                q_out[:, hd * MLA_QK_PAD:hd * MLA_QK_PAD + MLA_NOPE] = (
                    qn[:, sub * MLA_NOPE:(sub + 1) * MLA_NOPE])
                q_out[:, hd * MLA_QK_PAD + MLA_NOPE:(hd + 1) * MLA_QK_PAD] = (
                    _rope_tile_for_head(tile, sub))


def _mla_q(x, g, wdq, lora_g, wuq, qn_g, qr_g, cos, sa, sb):
    m = x.shape[0]
    return pl.pallas_call(
        _q_kernel,
        out_shape=jax.ShapeDtypeStruct((m, HEADS * MLA_QK_PAD), BF16),
        grid=(m // TM,),
        in_specs=[_rows(D_MODEL), _resident((1, D_MODEL)), _resident((D_MODEL, Q_LORA)),
                  _resident((1, Q_LORA)), _resident(wuq.shape),
                  _resident((1, 2 * MLA_NOPE)), _resident((1, 2 * LANES)),
                  _rows(LANES), _rows(LANES), _rows(LANES)],
        out_specs=_rows(HEADS * MLA_QK_PAD),
        compiler_params=_params("parallel"),
        name="mla_q",
    )(x, g, wdq, lora_g, wuq, qn_g, qr_g, cos, sa, sb)


def _flash_kernel(q_ref, k_ref, v_ref, o_ref, s0_ref, s1_ref, m_ref, acc_ref):
    i = pl.program_id(2)
    m_ref[...] = jnp.full_like(m_ref, NEG_BIG)
    acc_ref[...] = jnp.zeros_like(acc_ref)

    def scores(j, s_ref):
        off = pl.multiple_of(j * TQ, TQ)
        for hd in range(HEADS_PER_STEP):
            qk = slice(hd * MLA_QK_PAD, (hd + 1) * MLA_QK_PAD)
            s_ref[hd] = _dot_nt(q_ref[:, qk], k_ref[pl.ds(off, TQ), qk])

    def consume(j, s_ref, masked=False):
        off = pl.multiple_of(j * TQ, TQ)
        for hd in range(HEADS_PER_STEP):
            s = s_ref[hd]
            if masked:
                row = lax.broadcasted_iota(jnp.int32, (TQ, TQ), 0)
                col = lax.broadcasted_iota(jnp.int32, (TQ, TQ), 1)
                s = jnp.where(col <= row, s, NEG_BIG)
            m_prev = m_ref[hd]
            m_new = jnp.maximum(m_prev, jnp.max(s, axis=1, keepdims=True))
            alpha = jnp.exp2(m_prev - m_new)
            p = jnp.exp2(s - _lane_tile(m_new, TQ // LANES))
            pv = _dot(p.astype(BF16), v_ref[pl.ds(off, TQ), hd * V_PAD:(hd + 1) * V_PAD])
            acc_ref[hd] = _lane_tile(alpha, V_PAD // LANES) * acc_ref[hd] + pv
            m_ref[hd] = m_new

    def pair(t, carry):
        j = 2 * t
        scores(j + 1, s1_ref)
        consume(j, s0_ref)
        scores(j + 2, s0_ref)
        consume(j + 1, s1_ref)
        return carry

    scores(0, s0_ref)
    lax.fori_loop(0, i // 2, pair, 0)

    @pl.when(i % 2 == 0)
    def _():
        consume(i, s0_ref, masked=True)

    @pl.when(i % 2 == 1)
    def _():
        scores(i, s1_ref)
        consume(i - 1, s0_ref)
        consume(i, s1_ref, masked=True)

    for hd in range(HEADS_PER_STEP):
        acc = acc_ref[hd]
        o_ref[:, hd * MLA_V:(hd + 1) * MLA_V] = (acc[:, :MLA_V] / acc[:, MLA_V:]).astype(BF16)


def _flash(q, k, v, batch, seq):
    m = q.shape[0]
    nq = seq // TQ
    g = HEADS_PER_STEP
    return pl.pallas_call(
        _flash_kernel,
        out_shape=jax.ShapeDtypeStruct((m, HEADS * MLA_V), BF16),
        grid=(batch, HEADS // g, nq),
        in_specs=[
            pl.BlockSpec((TQ, g * MLA_QK_PAD), lambda b, h, i: (b * nq + i, h)),
            pl.BlockSpec((seq, g * MLA_QK_PAD), lambda b, h, i: (b, h)),
            pl.BlockSpec((seq, g * V_PAD), lambda b, h, i: (b, h)),
        ],
        out_specs=pl.BlockSpec((TQ, g * MLA_V), lambda b, h, i: (b * nq + i, h)),
        scratch_shapes=[pltpu.VMEM((g, TQ, TQ), F32), pltpu.VMEM((g, TQ, TQ), F32),
                        pltpu.VMEM((g, TQ, LANES), F32),
                        pltpu.VMEM((g, TQ, V_PAD), F32)],
        compiler_params=_params("parallel", "parallel", "arbitrary"),
        name="flash_attention",
    )(q, k, v)


def _row(g, repeat=1):
    return jnp.tile(g.reshape(1, -1).astype(F32), (1, repeat))


def kernel(x, positions, norm_g, ffn_w_gate, ffn_w_up, ffn_w_down, ret_w_in, ret_gn_g, ret_w_o,
           kv_norm_g, kv_w_down, kv_latent_norm_g, kv_w_up, k_nope_norm_g, k_rope_norm_g,
           mla_w_dq, mla_q_lora_norm_g, mla_w_uq, mla_q_nope_norm_g, mla_q_rope_norm_g, mla_w_o):
    batch, seq, d = x.shape
    depth = norm_g.shape[0]
    n_self = ret_w_in.shape[0]
    m = batch * seq
    x = x.reshape(m, d)
    rcos, rsin, mcos, msa, msb = _rope_tables(positions.reshape(m, 1))
    wg_all = ffn_w_gate.astype(BF16)
    wu_all = ffn_w_up.astype(BF16)
    wd_all = ffn_w_down.astype(BF16)

    def ffn(x, layer, i, mix=None):
        return _ffn(x, _row(norm_g[layer, 2 * i]), wg_all, wu_all, wd_all, layer, i, mix)

    k_shared = v_shared = None
    for layer in range(depth):
        x = ffn(x, layer, 0)
        g_mix = _row(norm_g[layer, 1])
        if layer < n_self:
            proj, kt = _ret_proj(x, g_mix, ret_w_in[layer].astype(BF16), rcos, rsin)
            mixed = _retention(proj, kt, ret_gn_g[layer].reshape(HEADS, 1, RET_DV), batch, seq)
            w_o = ret_w_o[layer].astype(BF16)
        else:
            j = layer - n_self
            wuq = mla_w_uq[j].reshape(Q_LORA, HEADS, MLA_NOPE + MLA_ROPE)
            wuq = jnp.concatenate([wuq[:, :, :MLA_NOPE].reshape(Q_LORA, Q_NOPE_COLS),
                                   wuq[:, :, MLA_NOPE:].reshape(Q_LORA, HEADS * MLA_ROPE)], axis=1)
            q = _mla_q(x, g_mix, mla_w_dq[j].astype(BF16), _row(mla_q_lora_norm_g[j]),
                       wuq.astype(BF16), _row(mla_q_nope_norm_g[j], 2), _row(mla_q_rope_norm_g[j], 4),
                       mcos, msa, msb)
            mixed = _flash(q, k_shared, v_shared, batch, seq)
            w_o = mla_w_o[j].astype(BF16)
        x = ffn(x, layer, 1, mix=(mixed, w_o))
        if layer == n_self - 1:
            wup = kv_w_up.reshape(KV_LORA, HEADS, MLA_NOPE + MLA_V)
            wk = wup[:, :, :MLA_NOPE].reshape(KV_LORA, HEADS * MLA_NOPE).astype(BF16)
            wv = wup[:, :, MLA_NOPE:].reshape(KV_LORA, HEADS * MLA_V).astype(BF16)
            wdr = jnp.tile(kv_w_down[:, KV_LORA:], (1, 2)).astype(BF16)
            k_shared, v_shared = _shared_kv(
                x, _row(kv_norm_g), kv_w_down[:, :KV_LORA].astype(BF16), wdr,
                _row(kv_latent_norm_g), wk, wv, _row(k_nope_norm_g, 2), _row(k_rope_norm_g, 2),
                mcos, msa, msb)
    return x.reshape(batch, seq, d)
```

```python
import math

import jax
import jax.numpy as jnp
from jax import lax
from jax.experimental import pallas as pl
from jax.experimental.pallas import tpu as pltpu

F32 = jnp.float32
BF16 = jnp.bfloat16

LANES = 128

D_MODEL = 1024
D_FF = 2816
HEADS = 8
RET_DK = 128
RET_DV = 256
RET_QK = HEADS * RET_DK
RET_PROJ = 2 * RET_QK + 2 * HEADS * RET_DV
MLA_NOPE = 128
MLA_ROPE = 64
MLA_V = 128
MLA_QK_PAD = 2 * LANES
V_PAD = 2 * LANES
HEADS_PER_STEP = 2
Q_LORA = 384
KV_LORA = 256
ROPE_BASE = 10000.0
EPS = 1e-6

VMEM_LIMIT = 56 * 1024 * 1024

TM = 512
TM_FFN = 1024
FF_CHUNK = 256
RET_CHUNK = 256
RET_HEADS_PER_STEP = 2
TQ = 512
SCORE_BUFFERS = 3
NEG_BIG = -1e30


def _dot(a, b):
    return jnp.dot(a, b, preferred_element_type=F32)


def _dot_nt(a, b):
    return lax.dot_general(a, b, (((1,), (1,)), ((), ())), preferred_element_type=F32)


def _dot_tn(a, b):
    return lax.dot_general(a, b, (((0,), (0,)), ((), ())), preferred_element_type=F32)


def _rms(x, g, n):
    ms = jnp.sum(x * x, axis=-1, keepdims=True) * (1.0 / n)
    return x * lax.rsqrt(ms + EPS) * g


def _silu(x):
    return x / (1.0 + jnp.exp2(x * -math.log2(math.e)))


def _lane_tile(x, n):
    return jnp.concatenate([x] * n, axis=1)


def _group_ones(width, group):
    shift = group.bit_length() - 1
    r = lax.broadcasted_iota(jnp.int32, (width, width), 0) >> shift
    c = lax.broadcasted_iota(jnp.int32, (width, width), 1) >> shift
    return jnp.where(r == c, 1.0, 0.0).astype(BF16)


def _group_rms(x, ones, group, g):
    ms = _dot((x * x).astype(BF16), ones) * (1.0 / group)
    return x * lax.rsqrt(ms + EPS) * g


def _params(*sem):
    return pltpu.CompilerParams(dimension_semantics=sem, vmem_limit_bytes=VMEM_LIMIT)


def _resident(shape):
    nd = len(shape)
    return pl.BlockSpec(shape, lambda *_: (0,) * nd, pipeline_mode=pl.Buffered(1))


def _rows(width, tm=TM):
    return pl.BlockSpec((tm, width), lambda i: (i, 0))


def _rope_tables_kernel(pos_ref, rcos_ref, rsin_ref, mcos_ref, msa_ref, msb_ref):
    pos = pos_ref[...].astype(F32)
    lane = lax.broadcasted_iota(jnp.int32, (1, LANES), 1)
    is_ret = lane < 64
    freq = jnp.where(is_ret, lane, lane & 31).astype(F32)
    step = jnp.where(is_ret, -2.0 / RET_DK, -2.0 / MLA_ROPE) * math.log(ROPE_BASE)
    ang = pos * jnp.exp(freq * step)
    c = jnp.cos(ang)
    s = jnp.sin(ang)
    rcos_ref[...] = jnp.where(is_ret, c, pltpu.roll(c, 64, 1))
    rsin_ref[...] = jnp.where(is_ret, -s, pltpu.roll(s, 64, 1))

    def spread(t):
        quarter = lane >> 5
        return jnp.where(quarter == 0, pltpu.roll(t, 64, 1),
                         jnp.where(quarter == 1, pltpu.roll(t, 96, 1),
                                   jnp.where(quarter == 2, t, pltpu.roll(t, 32, 1))))

    s = spread(s)
    first_half = (lane & 63) < 32
    mcos_ref[...] = spread(c)
    msa_ref[...] = jnp.where(first_half, -s, 0.0)
    msb_ref[...] = jnp.where(first_half, 0.0, s)


def _rope_tables(pos):
    m = pos.shape[0]
    tm = 1024
    tab = jax.ShapeDtypeStruct((m, LANES), F32)
    spec = pl.BlockSpec((tm, LANES), lambda i: (i, 0))
    return pl.pallas_call(
        _rope_tables_kernel,
        out_shape=(tab,) * 5,
        grid=(m // tm,),
        in_specs=[pl.BlockSpec((tm, 1), lambda i: (i, 0))],
        out_specs=(spec,) * 5,
        compiler_params=_params("parallel"),
        name="rope_tables",
    )(pos)


def _rope128(x, cos, sin):
    return x * cos + pltpu.roll(x, 64, 1) * sin


def _rope64x2(x, cos, sa, sb):
    return x * cos + pltpu.roll(x, 96, 1) * sa + pltpu.roll(x, 32, 1) * sb


def _ffn_body(x, g_ref, wg_ref, wu_ref, wd_ref, o_ref):
    h = _rms(x, g_ref[...], D_MODEL).astype(BF16)
    acc = None
    for c in range(D_FF // FF_CHUNK):
        sl = slice(c * FF_CHUNK, (c + 1) * FF_CHUNK)
        a = (_silu(_dot(h, wg_ref[:, sl])) * _dot(h, wu_ref[:, sl])).astype(BF16)
        d = _dot(a, wd_ref[sl, :])
        acc = d if acc is None else acc + d
    o_ref[...] = x + 0.5 * acc


def _ffn_kernel(x_ref, g_ref, wg_ref, wu_ref, wd_ref, o_ref):
    _ffn_body(x_ref[...], g_ref, wg_ref, wu_ref, wd_ref, o_ref)


def _mix_ffn_kernel(x_ref, a_ref, wo_ref, g_ref, wg_ref, wu_ref, wd_ref, o_ref):
    _ffn_body(x_ref[...] + _dot(a_ref[...], wo_ref[...]), g_ref, wg_ref, wu_ref, wd_ref, o_ref)


def _ffn(x, g, wg, wu, wd, layer, half, mix=None):
    m = x.shape[0]

    def stacked(rows, cols):
        return pl.BlockSpec((None, None, rows, cols), lambda i: (layer, half, 0, 0),
                            pipeline_mode=pl.Buffered(1))

    w_specs = [_resident((1, D_MODEL)), stacked(D_MODEL, D_FF), stacked(D_MODEL, D_FF),
               stacked(D_FF, D_MODEL)]
    if mix is None:
        tm = TM_FFN
        body, args, specs = _ffn_kernel, (x,), [_rows(D_MODEL, tm)]
    else:
        tm = TM
        a, wo = mix
        body, args = _mix_ffn_kernel, (x, a, wo)
        specs = [_rows(D_MODEL, tm), _rows(a.shape[1], tm), _resident(wo.shape)]
    return pl.pallas_call(
        body,
        out_shape=jax.ShapeDtypeStruct((m, D_MODEL), F32),
        grid=(m // tm,),
        in_specs=specs + w_specs,
        out_specs=_rows(D_MODEL, tm),
        compiler_params=_params("parallel"),
        name="ffn" if mix is None else "mix_ffn",
    )(*args, g, wg, wu, wd)


def _ret_proj_kernel(x_ref, g_ref, w_ref, cos_ref, sin_ref, o_ref, kt_ref):
    h = _rms(x_ref[...], g_ref[...], D_MODEL).astype(BF16)
    cos = cos_ref[...]
    sin = sin_ref[...]
    q = _dot(h, w_ref[:, :RET_QK])
    k = _dot(h, w_ref[:, RET_QK:2 * RET_QK])
    for hd in range(HEADS):
        head = slice(hd * RET_DK, (hd + 1) * RET_DK)
        o_ref[:, head] = _rope128(q[:, head], cos, sin).astype(BF16)
        kh = _rope128(k[:, head], cos, sin) * RET_DK ** -0.5
        for c in range(TM // RET_CHUNK):
            kt_ref[c, head, :] = kh[c * RET_CHUNK:(c + 1) * RET_CHUNK, :].T.astype(BF16)
    o_ref[:, RET_QK:] = _dot(h, w_ref[:, 2 * RET_QK:]).astype(BF16)


def _ret_proj(x, g, w, cos, sin):
    m = x.shape[0]
    chunks = TM // RET_CHUNK
    return pl.pallas_call(
        _ret_proj_kernel,
        out_shape=(jax.ShapeDtypeStruct((m, RET_PROJ - RET_QK), BF16),
                   jax.ShapeDtypeStruct((m // RET_CHUNK, RET_QK, RET_CHUNK), BF16)),
        grid=(m // TM,),
        in_specs=[_rows(D_MODEL), _resident((1, D_MODEL)), _resident((D_MODEL, RET_PROJ)),
                  _rows(LANES), _rows(LANES)],
        out_specs=(_rows(RET_PROJ - RET_QK),
                   pl.BlockSpec((chunks, RET_QK, RET_CHUNK), lambda i: (i, 0, 0))),
        compiler_params=_params("parallel"),
        name="ret_proj",
    )(x, g, w, cos, sin)


def _retention_kernel(q_ref, kt_ref, v_ref, gate_ref, gn_ref, o_ref, state_ref):
    c = RET_CHUNK
    seq = q_ref.shape[0]

    def index(shape, axis):
        return lax.broadcasted_iota(jnp.int32, shape, axis).astype(F32)

    def decays(sub):
        hd = (pl.program_id(1) * RET_HEADS_PER_STEP + sub).astype(F32)

        def log_decay(shape):
            return jnp.log1p(-jnp.exp2(jnp.full(shape, -5.0, F32) - hd))

        diff = index((c, c), 0) - index((c, c), 1)
        d_intra = jnp.where(diff >= 0, jnp.exp(log_decay((c, c)) * jnp.maximum(diff, 0.0)), 0.0)
        q_decay = jnp.exp(log_decay((c, RET_DV)) * (index((c, RET_DV), 0) + 1.0))
        k_decay = jnp.exp(log_decay((RET_DK, c)) * (c - 1.0 - index((RET_DK, c), 1)))
        chunk_decay = jnp.exp(log_decay((RET_DK, RET_DV)) * c)
        return d_intra, q_decay, k_decay, chunk_decay

    per_head = [decays(sub) for sub in range(RET_HEADS_PER_STEP)]
    state_ref[...] = jnp.zeros_like(state_ref)

    def body(t, carry):
        off = pl.multiple_of(t * c, c)
        for sub, (d_intra, q_decay, k_decay, chunk_decay) in enumerate(per_head):
            qk = slice(sub * RET_DK, (sub + 1) * RET_DK)
            vo = slice(sub * RET_DV, (sub + 1) * RET_DV)
            q = q_ref[pl.ds(off, c), qk]
            kt = kt_ref[t, qk, :]
            v = v_ref[pl.ds(off, c), vo]
            state = state_ref[sub]
            s = _dot(q, kt) * d_intra
            o = _dot(s.astype(BF16), v) + _dot(q, state.astype(BF16)) * q_decay
            ktd = (kt.astype(F32) * k_decay).astype(BF16)
            state_ref[sub] = state * chunk_decay + _dot(ktd, v)
            mu = jnp.mean(o, axis=-1, keepdims=True)
            oc = o - mu
            var = jnp.mean(oc * oc, axis=-1, keepdims=True)
            on = oc * lax.rsqrt(var + EPS) * gn_ref[sub]
            gate = _silu(gate_ref[pl.ds(off, c), vo].astype(F32))
            o_ref[pl.ds(off, c), vo] = (gate * on).astype(BF16)
        return carry

    lax.fori_loop(0, seq // c, body, 0, unroll=4)


def _retention(proj, kt, gn, batch, seq):
    m = proj.shape[0]
    g = RET_HEADS_PER_STEP
    groups = HEADS // g
    vblk = RET_QK // (g * RET_DV)
    gblk = vblk + groups
    return pl.pallas_call(
        _retention_kernel,
        out_shape=jax.ShapeDtypeStruct((m, HEADS * RET_DV), BF16),
        grid=(batch, groups),
        in_specs=[
            pl.BlockSpec((seq, g * RET_DK), lambda b, h: (b, h)),
            pl.BlockSpec((seq // RET_CHUNK, g * RET_DK, RET_CHUNK), lambda b, h: (b, h, 0)),
            pl.BlockSpec((seq, g * RET_DV), lambda b, h: (b, vblk + h)),
            pl.BlockSpec((seq, g * RET_DV), lambda b, h: (b, gblk + h)),
            pl.BlockSpec((g, 1, RET_DV), lambda b, h: (h, 0, 0)),
        ],
        out_specs=pl.BlockSpec((seq, g * RET_DV), lambda b, h: (b, h)),
        scratch_shapes=[pltpu.VMEM((g, RET_DK, RET_DV), F32)],
        compiler_params=_params("parallel", "parallel"),
        name="retention",
    )(proj, kt, proj, proj, gn)


def _rope_tile_for_head(tile, hd):
    lane = lax.broadcasted_iota(jnp.int32, tile.shape, 1)
    keep = (lane < MLA_ROPE) if hd % 2 == 0 else (lane >= MLA_ROPE)
    return jnp.where(keep, tile, jnp.zeros_like(tile))


def _kv_kernel(x_ref, g_ref, wdc_ref, wdr_ref, lat_g_ref, wk_ref, wv_ref, kn_g_ref, kr_g_ref,
               cos_ref, sa_ref, sb_ref, k_out, v_out):
    h = _rms(x_ref[...], g_ref[...], D_MODEL).astype(BF16)
    lat = _rms(_dot(h, wdc_ref[...]), lat_g_ref[...], KV_LORA).astype(BF16)
    pe = _rms(_dot(h, wdr_ref[...]), kr_g_ref[...], LANES)
    pe = _rope64x2(pe, cos_ref[...], sa_ref[...], sb_ref[...]).astype(BF16)
    pe_tiles = (_rope_tile_for_head(pe, 0), _rope_tile_for_head(pe, 1))
    ones = jnp.ones((pe.shape[0], V_PAD - MLA_V), BF16)
    ones_nope = _group_ones(2 * MLA_NOPE, MLA_NOPE)
    kn_g = kn_g_ref[...]
    for pair in range(HEADS // 2):
        lo = pair * 2 * MLA_NOPE
        kn = _group_rms(_dot(lat, wk_ref[:, lo:lo + 2 * MLA_NOPE]), ones_nope, MLA_NOPE, kn_g)
        kn = kn.astype(BF16)
        vv = _dot(lat, wv_ref[:, pair * 2 * MLA_V:(pair + 1) * 2 * MLA_V]).astype(BF16)
        for sub in range(2):
            hd = 2 * pair + sub
            k_out[:, hd * MLA_QK_PAD:hd * MLA_QK_PAD + MLA_NOPE] = (
                kn[:, sub * MLA_NOPE:(sub + 1) * MLA_NOPE])
            k_out[:, hd * MLA_QK_PAD + MLA_NOPE:(hd + 1) * MLA_QK_PAD] = pe_tiles[sub]
            v_out[:, hd * V_PAD:hd * V_PAD + MLA_V] = vv[:, sub * MLA_V:(sub + 1) * MLA_V]
            v_out[:, hd * V_PAD + MLA_V:(hd + 1) * V_PAD] = ones


def _shared_kv(x, g, wdc, wdr, lat_g, wk, wv, kn_g, kr_g, cos, sa, sb):
    m = x.shape[0]
    return pl.pallas_call(
        _kv_kernel,
        out_shape=(jax.ShapeDtypeStruct((m, HEADS * MLA_QK_PAD), BF16),
                   jax.ShapeDtypeStruct((m, HEADS * V_PAD), BF16)),
        grid=(m // TM,),
        in_specs=[_rows(D_MODEL), _resident((1, D_MODEL)), _resident((D_MODEL, KV_LORA)),
                  _resident((D_MODEL, LANES)), _resident((1, KV_LORA)),
                  _resident((KV_LORA, HEADS * MLA_NOPE)), _resident((KV_LORA, HEADS * MLA_V)),
                  _resident((1, 2 * MLA_NOPE)), _resident((1, LANES)),
                  _rows(LANES), _rows(LANES), _rows(LANES)],
        out_specs=(_rows(HEADS * MLA_QK_PAD), _rows(HEADS * V_PAD)),
        compiler_params=_params("parallel"),
        name="shared_kv",
    )(x, g, wdc, wdr, lat_g, wk, wv, kn_g, kr_g, cos, sa, sb)


Q_SCALE = (MLA_NOPE + MLA_ROPE) ** -0.5 * math.log2(math.e)
Q_NOPE_COLS = HEADS * MLA_NOPE


def _q_kernel(x_ref, g_ref, wdq_ref, lora_g_ref, wuq_ref, qn_g_ref, qr_g_ref,
              cos_ref, sa_ref, sb_ref, q_out):
    h = _rms(x_ref[...], g_ref[...], D_MODEL).astype(BF16)
    cq = _rms(_dot(h, wdq_ref[...]), lora_g_ref[...], Q_LORA).astype(BF16)
    q = _dot(cq, wuq_ref[...])
    ones_nope = _group_ones(2 * MLA_NOPE, MLA_NOPE)
    ones_rope = _group_ones(2 * LANES, MLA_ROPE)
    qn_g = qn_g_ref[...] * Q_SCALE
    qr_g = qr_g_ref[...] * Q_SCALE
    cos = cos_ref[...]
    sa = sa_ref[...]
    sb = sb_ref[...]
    for quad in range(HEADS // 4):
        lo = Q_NOPE_COLS + quad * 2 * LANES
        qp = _group_rms(q[:, lo:lo + 2 * LANES], ones_rope, MLA_ROPE, qr_g)
        for half in range(2):
            pair = 2 * quad + half
            lo = pair * 2 * MLA_NOPE
            qn = _group_rms(q[:, lo:lo + 2 * MLA_NOPE], ones_nope, MLA_NOPE, qn_g).astype(BF16)
            tile = _rope64x2(qp[:, half * LANES:(half + 1) * LANES], cos, sa, sb).astype(BF16)
            for sub in range(2):
                hd = 2 * pair + sub
                q_out[:, hd * MLA_QK_PAD:hd * MLA_QK_PAD + MLA_NOPE] = (
                    qn[:, sub * MLA_NOPE:(sub + 1) * MLA_NOPE])
                q_out[:, hd * MLA_QK_PAD + MLA_NOPE:(hd + 1) * MLA_QK_PAD] = (
                    _rope_tile_for_head(tile, sub))


def _mla_q(x, g, wdq, lora_g, wuq, qn_g, qr_g, cos, sa, sb):
    m = x.shape[0]
    return pl.pallas_call(
        _q_kernel,
        out_shape=jax.ShapeDtypeStruct((m, HEADS * MLA_QK_PAD), BF16),
        grid=(m // TM,),
        in_specs=[_rows(D_MODEL), _resident((1, D_MODEL)), _resident((D_MODEL, Q_LORA)),
                  _resident((1, Q_LORA)), _resident(wuq.shape),
                  _resident((1, 2 * MLA_NOPE)), _resident((1, 2 * LANES)),
                  _rows(LANES), _rows(LANES), _rows(LANES)],
        out_specs=_rows(HEADS * MLA_QK_PAD),
        compiler_params=_params("parallel"),
        name="mla_q",
    )(x, g, wdq, lora_g, wuq, qn_g, qr_g, cos, sa, sb)


def _flash_kernel(q_ref, k_ref, v_ref, o_ref, s_ref, m_ref, acc_ref):
    tiles = q_ref.shape[0] // TQ
    i = pl.program_id(2)
    pair = (i, tiles - 1 - i)
    m_ref[...] = jnp.full_like(m_ref, NEG_BIG)
    acc_ref[...] = jnp.zeros_like(acc_ref)

    def item(k):
        if k < tiles - 1:
            second = k >= i
            return (jnp.where(second, 1, 0), jnp.where(second, pair[1], pair[0]),
                    jnp.where(second, k - i, k), False)
        which = k - (tiles - 1)
        return which, pair[which], pair[which], True

    def scores(k):
        _, qt, kt, _ = item(k)
        rows = pl.ds(pl.multiple_of(qt * TQ, TQ), TQ)
        keys = pl.ds(pl.multiple_of(kt * TQ, TQ), TQ)
        for hd in range(HEADS_PER_STEP):
            qk = slice(hd * MLA_QK_PAD, (hd + 1) * MLA_QK_PAD)
            s_ref[k % SCORE_BUFFERS, hd] = _dot_nt(q_ref[rows, qk], k_ref[keys, qk])

    def consume(k):
        which, _, kt, diagonal = item(k)
        keys = pl.ds(pl.multiple_of(kt * TQ, TQ), TQ)
        for hd in range(HEADS_PER_STEP):
            s = s_ref[k % SCORE_BUFFERS, hd]
            if diagonal:
                row = lax.broadcasted_iota(jnp.int32, (TQ, TQ), 0)
                col = lax.broadcasted_iota(jnp.int32, (TQ, TQ), 1)
                s = jnp.where(col <= row, s, NEG_BIG)
            m_prev = m_ref[which, hd]
            m_new = jnp.maximum(m_prev, jnp.max(s, axis=1, keepdims=True))
            alpha = jnp.exp2(m_prev - m_new)
            p = jnp.exp2(s - _lane_tile(m_new, TQ // LANES))
            pv = _dot(p.astype(BF16), v_ref[keys, hd * V_PAD:(hd + 1) * V_PAD])
            acc_ref[which, hd] = _lane_tile(alpha, V_PAD // LANES) * acc_ref[which, hd] + pv
            m_ref[which, hd] = m_new

    n_items = tiles + 1
    for k in range(SCORE_BUFFERS - 1):
        scores(k)
    for k in range(n_items):
        if k + SCORE_BUFFERS - 1 < n_items:
            scores(k + SCORE_BUFFERS - 1)
        consume(k)

    for which in range(2):
        rows = pl.ds(pl.multiple_of(pair[which] * TQ, TQ), TQ)
        for hd in range(HEADS_PER_STEP):
            acc = acc_ref[which, hd]
            o_ref[rows, hd * MLA_V:(hd + 1) * MLA_V] = (acc[:, :MLA_V] / acc[:, MLA_V:]).astype(BF16)


def _flash(q, k, v, batch, seq):
    m = q.shape[0]
    g = HEADS_PER_STEP
    tiles = seq // TQ
    assert tiles % 2 == 0, "query tiles are processed in (i, tiles-1-i) pairs"
    return pl.pallas_call(
        _flash_kernel,
        out_shape=jax.ShapeDtypeStruct((m, HEADS * MLA_V), BF16),
        grid=(batch, HEADS // g, tiles // 2),
        in_specs=[
            pl.BlockSpec((seq, g * MLA_QK_PAD), lambda b, h, i: (b, h)),
            pl.BlockSpec((seq, g * MLA_QK_PAD), lambda b, h, i: (b, h)),
            pl.BlockSpec((seq, g * V_PAD), lambda b, h, i: (b, h)),
        ],
        out_specs=pl.BlockSpec((seq, g * MLA_V), lambda b, h, i: (b, h)),
        scratch_shapes=[pltpu.VMEM((SCORE_BUFFERS, g, TQ, TQ), F32),
                        pltpu.VMEM((2, g, TQ, LANES), F32),
                        pltpu.VMEM((2, g, TQ, V_PAD), F32)],
        compiler_params=_params("parallel", "parallel", "arbitrary"),
        name="flash_attention",
    )(q, k, v)


def _row(g, repeat=1):
    return jnp.tile(g.reshape(1, -1).astype(F32), (1, repeat))


def kernel(x, positions, norm_g, ffn_w_gate, ffn_w_up, ffn_w_down, ret_w_in, ret_gn_g, ret_w_o,
           kv_norm_g, kv_w_down, kv_latent_norm_g, kv_w_up, k_nope_norm_g, k_rope_norm_g,
           mla_w_dq, mla_q_lora_norm_g, mla_w_uq, mla_q_nope_norm_g, mla_q_rope_norm_g, mla_w_o):
    batch, seq, d = x.shape
    depth = norm_g.shape[0]
    n_self = ret_w_in.shape[0]
    m = batch * seq
    x = x.reshape(m, d)
    rcos, rsin, mcos, msa, msb = _rope_tables(positions.reshape(m, 1))
    wg_all = ffn_w_gate.astype(BF16)
    wu_all = ffn_w_up.astype(BF16)
    wd_all = ffn_w_down.astype(BF16)

    def ffn(x, layer, i, mix=None):
        return _ffn(x, _row(norm_g[layer, 2 * i]), wg_all, wu_all, wd_all, layer, i, mix)

    k_shared = v_shared = None
    for layer in range(depth):
        x = ffn(x, layer, 0)
        g_mix = _row(norm_g[layer, 1])
        if layer < n_self:
            proj, kt = _ret_proj(x, g_mix, ret_w_in[layer].astype(BF16), rcos, rsin)
            mixed = _retention(proj, kt, ret_gn_g[layer].reshape(HEADS, 1, RET_DV), batch, seq)
            w_o = ret_w_o[layer].astype(BF16)
        else:
            j = layer - n_self
            wuq = mla_w_uq[j].reshape(Q_LORA, HEADS, MLA_NOPE + MLA_ROPE)
            wuq = jnp.concatenate([wuq[:, :, :MLA_NOPE].reshape(Q_LORA, Q_NOPE_COLS),
                                   wuq[:, :, MLA_NOPE:].reshape(Q_LORA, HEADS * MLA_ROPE)], axis=1)
            q = _mla_q(x, g_mix, mla_w_dq[j].astype(BF16), _row(mla_q_lora_norm_g[j]),
                       wuq.astype(BF16), _row(mla_q_nope_norm_g[j], 2), _row(mla_q_rope_norm_g[j], 4),
                       mcos, msa, msb)
            mixed = _flash(q, k_shared, v_shared, batch, seq)
            w_o = mla_w_o[j].astype(BF16)
        x = ffn(x, layer, 1, mix=(mixed, w_o))
        if layer == n_self - 1:
            wup = kv_w_up.reshape(KV_LORA, HEADS, MLA_NOPE + MLA_V)
            wk = wup[:, :, :MLA_NOPE].reshape(KV_LORA, HEADS * MLA_NOPE).astype(BF16)
            wv = wup[:, :, MLA_NOPE:].reshape(KV_LORA, HEADS * MLA_V).astype(BF16)
            wdr = jnp.tile(kv_w_down[:, KV_LORA:], (1, 2)).astype(BF16)
            k_shared, v_shared = _shared_kv(
                x, _row(kv_norm_g), kv_w_down[:, :KV_LORA].astype(BF16), wdr,
                _row(kv_latent_norm_g), wk, wv, _row(k_nope_norm_g, 2), _row(k_rope_norm_g, 2),
                mcos, msa, msb)
    return x.reshape(batch, seq, d)
```

```python
import math

import jax
import jax.numpy as jnp
from jax import lax
from jax.experimental import pallas as pl
from jax.experimental.pallas import tpu as pltpu

F32 = jnp.float32
BF16 = jnp.bfloat16

LANES = 128

D_MODEL = 1024
D_FF = 2816
HEADS = 8
RET_DK = 128
RET_DV = 256
RET_QK = HEADS * RET_DK
RET_PROJ = 2 * RET_QK + 2 * HEADS * RET_DV
MLA_NOPE = 128
MLA_ROPE = 64
MLA_V = 128
MLA_QK_PAD = 2 * LANES
V_PAD = 2 * LANES
HEADS_PER_STEP = 2
Q_LORA = 384
KV_LORA = 256
ROPE_BASE = 10000.0
EPS = 1e-6

VMEM_LIMIT = 56 * 1024 * 1024

TM = 512
TM_FFN = 1024
TM_MLA = 1024
FF_CHUNK = 256
RET_CHUNK = 256
RET_HEADS_PER_STEP = 2
TQ = 512
SCORES_AHEAD = 2
NEG_BIG = -1e30


def _dot(a, b):
    return jnp.dot(a, b, preferred_element_type=F32)


def _dot_nt(a, b):
    return lax.dot_general(a, b, (((1,), (1,)), ((), ())), preferred_element_type=F32)


def _dot_tn(a, b):
    return lax.dot_general(a, b, (((0,), (0,)), ((), ())), preferred_element_type=F32)


def _rms(x, g, n):
    ms = jnp.sum(x * x, axis=-1, keepdims=True) * (1.0 / n)
    return x * lax.rsqrt(ms + EPS) * g


def _silu(x):
    return x / (1.0 + jnp.exp2(x * -math.log2(math.e)))


def _lane_tile(x, n):
    return jnp.concatenate([x] * n, axis=1)


def _group_mean_matrix(width, group):
    shift = group.bit_length() - 1
    r = lax.broadcasted_iota(jnp.int32, (width, width), 0) >> shift
    c = lax.broadcasted_iota(jnp.int32, (width, width), 1) >> shift
    return jnp.where(r == c, 1.0 / group, 0.0).astype(BF16)


def _group_rms(x, mean_matrix, g):
    ms = _dot((x * x).astype(BF16), mean_matrix)
    return x * lax.rsqrt(ms + EPS) * g


def _params(*sem):
    return pltpu.CompilerParams(dimension_semantics=sem, vmem_limit_bytes=VMEM_LIMIT)


def _resident(shape):
    nd = len(shape)
    return pl.BlockSpec(shape, lambda *_: (0,) * nd, pipeline_mode=pl.Buffered(1))


def _rows(width, tm=TM):
    return pl.BlockSpec((tm, width), lambda i: (i, 0))


def _rope_tables_kernel(pos_ref, rcos_ref, rsin_ref, mcos_ref, msa_ref, msb_ref):
    pos = pos_ref[...].astype(F32)
    lane = lax.broadcasted_iota(jnp.int32, (1, LANES), 1)
    is_ret = lane < 64
    freq = jnp.where(is_ret, lane, lane & 31).astype(F32)
    step = jnp.where(is_ret, -2.0 / RET_DK, -2.0 / MLA_ROPE) * math.log(ROPE_BASE)
    ang = pos * jnp.exp(freq * step)
    c = jnp.cos(ang)
    s = jnp.sin(ang)
    rcos_ref[...] = jnp.where(is_ret, c, pltpu.roll(c, 64, 1))
    rsin_ref[...] = jnp.where(is_ret, -s, pltpu.roll(s, 64, 1))

    def spread(t):
        quarter = lane >> 5
        return jnp.where(quarter == 0, pltpu.roll(t, 64, 1),
                         jnp.where(quarter == 1, pltpu.roll(t, 96, 1),
                                   jnp.where(quarter == 2, t, pltpu.roll(t, 32, 1))))

    s = spread(s)
    first_half = (lane & 63) < 32
    mcos_ref[...] = spread(c)
    msa_ref[...] = jnp.where(first_half, -s, 0.0)
    msb_ref[...] = jnp.where(first_half, 0.0, s)


def _rope_tables(pos):
    m = pos.shape[0]
    tm = 1024
    tab = jax.ShapeDtypeStruct((m, LANES), F32)
    spec = pl.BlockSpec((tm, LANES), lambda i: (i, 0))
    return pl.pallas_call(
        _rope_tables_kernel,
        out_shape=(tab,) * 5,
        grid=(m // tm,),
        in_specs=[pl.BlockSpec((tm, 1), lambda i: (i, 0))],
        out_specs=(spec,) * 5,
        compiler_params=_params("parallel"),
        name="rope_tables",
    )(pos)


def _rope128(x, cos, sin):
    return x * cos + pltpu.roll(x, 64, 1) * sin


def _rope64x2(x, cos, sa, sb):
    return x * cos + pltpu.roll(x, 96, 1) * sa + pltpu.roll(x, 32, 1) * sb


def _ffn_body(x, g_ref, wg_ref, wu_ref, wd_ref, o_ref):
    h = _rms(x, g_ref[...], D_MODEL).astype(BF16)
    acc = None
    for c in range(D_FF // FF_CHUNK):
        sl = slice(c * FF_CHUNK, (c + 1) * FF_CHUNK)
        a = (_silu(_dot(h, wg_ref[:, sl])) * _dot(h, wu_ref[:, sl])).astype(BF16)
        d = _dot(a, wd_ref[sl, :])
        acc = d if acc is None else acc + d
    o_ref[...] = x + 0.5 * acc


def _ffn_kernel(x_ref, g_ref, wg_ref, wu_ref, wd_ref, o_ref):
    _ffn_body(x_ref[...], g_ref, wg_ref, wu_ref, wd_ref, o_ref)


def _mix_ffn_kernel(x_ref, a_ref, wo_ref, g_ref, wg_ref, wu_ref, wd_ref, o_ref):
    _ffn_body(x_ref[...] + _dot(a_ref[...], wo_ref[...]), g_ref, wg_ref, wu_ref, wd_ref, o_ref)


def _ffn(x, g, wg, wu, wd, layer, half, mix=None):
    m = x.shape[0]

    def stacked(rows, cols):
        return pl.BlockSpec((None, None, rows, cols), lambda i: (layer, half, 0, 0),
                            pipeline_mode=pl.Buffered(1))

    w_specs = [_resident((1, D_MODEL)), stacked(D_MODEL, D_FF), stacked(D_MODEL, D_FF),
               stacked(D_FF, D_MODEL)]
    if mix is None:
        tm = TM_FFN
        body, args, specs = _ffn_kernel, (x,), [_rows(D_MODEL, tm)]
    else:
        a, wo = mix
        tm = TM_FFN if a.shape[1] <= D_MODEL else TM
        body, args = _mix_ffn_kernel, (x, a, wo)
        specs = [_rows(D_MODEL, tm), _rows(a.shape[1], tm), _resident(wo.shape)]
    return pl.pallas_call(
        body,
        out_shape=jax.ShapeDtypeStruct((m, D_MODEL), F32),
        grid=(m // tm,),
        in_specs=specs + w_specs,
        out_specs=_rows(D_MODEL, tm),
        compiler_params=_params("parallel"),
        name="ffn" if mix is None else "mix_ffn",
    )(*args, g, wg, wu, wd)


def _ret_proj_kernel(x_ref, g_ref, w_ref, cos_ref, sin_ref, o_ref, kt_ref):
    h = _rms(x_ref[...], g_ref[...], D_MODEL).astype(BF16)
    cos = cos_ref[...]
    sin = sin_ref[...]
    q = _dot(h, w_ref[:, :RET_QK])
    k = _dot(h, w_ref[:, RET_QK:2 * RET_QK])
    for hd in range(HEADS):
        head = slice(hd * RET_DK, (hd + 1) * RET_DK)
        o_ref[:, head] = _rope128(q[:, head], cos, sin).astype(BF16)
        kh = _rope128(k[:, head], cos, sin) * RET_DK ** -0.5
        for c in range(TM // RET_CHUNK):
            kt_ref[c, head, :] = kh[c * RET_CHUNK:(c + 1) * RET_CHUNK, :].T.astype(BF16)
    o_ref[:, RET_QK:] = _dot(h, w_ref[:, 2 * RET_QK:]).astype(BF16)


def _ret_proj(x, g, w, cos, sin):
    m = x.shape[0]
    chunks = TM // RET_CHUNK
    return pl.pallas_call(
        _ret_proj_kernel,
        out_shape=(jax.ShapeDtypeStruct((m, RET_PROJ - RET_QK), BF16),
                   jax.ShapeDtypeStruct((m // RET_CHUNK, RET_QK, RET_CHUNK), BF16)),
        grid=(m // TM,),
        in_specs=[_rows(D_MODEL), _resident((1, D_MODEL)), _resident((D_MODEL, RET_PROJ)),
                  _rows(LANES), _rows(LANES)],
        out_specs=(_rows(RET_PROJ - RET_QK),
                   pl.BlockSpec((chunks, RET_QK, RET_CHUNK), lambda i: (i, 0, 0))),
        compiler_params=_params("parallel"),
        name="ret_proj",
    )(x, g, w, cos, sin)


def _retention_kernel(q_ref, kt_ref, v_ref, gate_ref, gn_ref, o_ref, state_ref):
    c = RET_CHUNK
    seq = q_ref.shape[0]

    def index(shape, axis):
        return lax.broadcasted_iota(jnp.int32, shape, axis).astype(F32)

    def decays(sub):
        hd = (pl.program_id(1) * RET_HEADS_PER_STEP + sub).astype(F32)

        def log_decay(shape):
            return jnp.log1p(-jnp.exp2(jnp.full(shape, -5.0, F32) - hd))

        diff = index((c, c), 0) - index((c, c), 1)
        d_intra = jnp.where(diff >= 0, jnp.exp(log_decay((c, c)) * jnp.maximum(diff, 0.0)), 0.0)
        q_decay = jnp.exp(log_decay((c, RET_DV)) * (index((c, RET_DV), 0) + 1.0))
        k_decay = jnp.exp(log_decay((RET_DK, c)) * (c - 1.0 - index((RET_DK, c), 1)))
        chunk_decay = jnp.exp(log_decay((RET_DK, RET_DV)) * c)
        return d_intra, q_decay, k_decay, chunk_decay

    per_head = [decays(sub) for sub in range(RET_HEADS_PER_STEP)]
    state_ref[...] = jnp.zeros_like(state_ref)

    def body(t, carry):
        off = pl.multiple_of(t * c, c)
        for sub, (d_intra, q_decay, k_decay, chunk_decay) in enumerate(per_head):
            qk = slice(sub * RET_DK, (sub + 1) * RET_DK)
            vo = slice(sub * RET_DV, (sub + 1) * RET_DV)
            q = q_ref[pl.ds(off, c), qk]
            kt = kt_ref[t, qk, :]
            v = v_ref[pl.ds(off, c), vo]
            state = state_ref[sub]
            s = _dot(q, kt) * d_intra
            o = _dot(s.astype(BF16), v) + _dot(q, state.astype(BF16)) * q_decay
            ktd = (kt.astype(F32) * k_decay).astype(BF16)
            state_ref[sub] = state * chunk_decay + _dot(ktd, v)
            mu = jnp.mean(o, axis=-1, keepdims=True)
            oc = o - mu
            var = jnp.mean(oc * oc, axis=-1, keepdims=True)
            on = oc * lax.rsqrt(var + EPS) * gn_ref[sub]
            gate = _silu(gate_ref[pl.ds(off, c), vo].astype(F32))
            o_ref[pl.ds(off, c), vo] = (gate * on).astype(BF16)
        return carry

    lax.fori_loop(0, seq // c, body, 0, unroll=4)


def _retention(proj, kt, gn, batch, seq):
    m = proj.shape[0]
    g = RET_HEADS_PER_STEP
    groups = HEADS // g
    vblk = RET_QK // (g * RET_DV)
    gblk = vblk + groups
    return pl.pallas_call(
        _retention_kernel,
        out_shape=jax.ShapeDtypeStruct((m, HEADS * RET_DV), BF16),
        grid=(batch, groups),
        in_specs=[
            pl.BlockSpec((seq, g * RET_DK), lambda b, h: (b, h)),
            pl.BlockSpec((seq // RET_CHUNK, g * RET_DK, RET_CHUNK), lambda b, h: (b, h, 0)),
            pl.BlockSpec((seq, g * RET_DV), lambda b, h: (b, vblk + h)),
            pl.BlockSpec((seq, g * RET_DV), lambda b, h: (b, gblk + h)),
            pl.BlockSpec((g, 1, RET_DV), lambda b, h: (h, 0, 0)),
        ],
        out_specs=pl.BlockSpec((seq, g * RET_DV), lambda b, h: (b, h)),
        scratch_shapes=[pltpu.VMEM((g, RET_DK, RET_DV), F32)],
        compiler_params=_params("parallel", "parallel"),
        name="retention",
    )(proj, kt, proj, proj, gn)


def _rope_tile_for_head(tile, hd):
    lane = lax.broadcasted_iota(jnp.int32, tile.shape, 1)
    keep = (lane < MLA_ROPE) if hd % 2 == 0 else (lane >= MLA_ROPE)
    return jnp.where(keep, tile, jnp.zeros_like(tile))


def _kv_kernel(x_ref, g_ref, wdc_ref, wdr_ref, lat_g_ref, wk_ref, wv_ref, kn_g_ref, kr_g_ref,
               cos_ref, sa_ref, sb_ref, k_out, v_out):
    h = _rms(x_ref[...], g_ref[...], D_MODEL).astype(BF16)
    lat = _rms(_dot(h, wdc_ref[...]), lat_g_ref[...], KV_LORA).astype(BF16)
    pe = _rms(_dot(h, wdr_ref[...]), kr_g_ref[...], LANES)
    pe = _rope64x2(pe, cos_ref[...], sa_ref[...], sb_ref[...]).astype(BF16)
    pe_tiles = (_rope_tile_for_head(pe, 0), _rope_tile_for_head(pe, 1))
    ones = jnp.ones((pe.shape[0], V_PAD - MLA_V), BF16)
    mean_nope = _group_mean_matrix(2 * MLA_NOPE, MLA_NOPE)
    kn_g = kn_g_ref[...]
    for pair in range(HEADS // 2):
        lo = pair * 2 * MLA_NOPE
        kn = _group_rms(_dot(lat, wk_ref[:, lo:lo + 2 * MLA_NOPE]), mean_nope, kn_g)
        kn = kn.astype(BF16)
        vv = _dot(lat, wv_ref[:, pair * 2 * MLA_V:(pair + 1) * 2 * MLA_V]).astype(BF16)
        for sub in range(2):
            hd = 2 * pair + sub
            k_out[:, hd * MLA_QK_PAD:hd * MLA_QK_PAD + MLA_NOPE] = (
                kn[:, sub * MLA_NOPE:(sub + 1) * MLA_NOPE])
            k_out[:, hd * MLA_QK_PAD + MLA_NOPE:(hd + 1) * MLA_QK_PAD] = pe_tiles[sub]
            v_out[:, hd * V_PAD:hd * V_PAD + MLA_V] = vv[:, sub * MLA_V:(sub + 1) * MLA_V]
            v_out[:, hd * V_PAD + MLA_V:(hd + 1) * V_PAD] = ones


def _shared_kv(x, g, wdc, wdr, lat_g, wk, wv, kn_g, kr_g, cos, sa, sb):
    m = x.shape[0]
    return pl.pallas_call(
        _kv_kernel,
        out_shape=(jax.ShapeDtypeStruct((m, HEADS * MLA_QK_PAD), BF16),
                   jax.ShapeDtypeStruct((m, HEADS * V_PAD), BF16)),
        grid=(m // TM_MLA,),
        in_specs=[_rows(D_MODEL, TM_MLA), _resident((1, D_MODEL)), _resident((D_MODEL, KV_LORA)),
                  _resident((D_MODEL, LANES)), _resident((1, KV_LORA)),
                  _resident((KV_LORA, HEADS * MLA_NOPE)), _resident((KV_LORA, HEADS * MLA_V)),
                  _resident((1, 2 * MLA_NOPE)), _resident((1, LANES)),
                  _rows(LANES, TM_MLA), _rows(LANES, TM_MLA), _rows(LANES, TM_MLA)],
        out_specs=(_rows(HEADS * MLA_QK_PAD, TM_MLA), _rows(HEADS * V_PAD, TM_MLA)),
        compiler_params=_params("parallel"),
        name="shared_kv",
    )(x, g, wdc, wdr, lat_g, wk, wv, kn_g, kr_g, cos, sa, sb)


Q_SCALE = (MLA_NOPE + MLA_ROPE) ** -0.5 * math.log2(math.e)
Q_NOPE_COLS = HEADS * MLA_NOPE


def _q_kernel(x_ref, g_ref, wdq_ref, lora_g_ref, wuq_ref, qn_g_ref, qr_g_ref,
              cos_ref, sa_ref, sb_ref, q_out):
    h = _rms(x_ref[...], g_ref[...], D_MODEL).astype(BF16)
    cq = _rms(_dot(h, wdq_ref[...]), lora_g_ref[...], Q_LORA).astype(BF16)
    q = _dot(cq, wuq_ref[...])
    mean_nope = _group_mean_matrix(2 * MLA_NOPE, MLA_NOPE)
    mean_rope = _group_mean_matrix(2 * LANES, MLA_ROPE)
    qn_g = qn_g_ref[...] * Q_SCALE
    qr_g = qr_g_ref[...] * Q_SCALE
    cos = cos_ref[...]
    sa = sa_ref[...]
    sb = sb_ref[...]
    for quad in range(HEADS // 4):
        lo = Q_NOPE_COLS + quad * 2 * LANES
        qp = _group_rms(q[:, lo:lo + 2 * LANES], mean_rope, qr_g)
        for half in range(2):
            pair = 2 * quad + half
            lo = pair * 2 * MLA_NOPE
            qn = _group_rms(q[:, lo:lo + 2 * MLA_NOPE], mean_nope, qn_g).astype(BF16)
            tile = _rope64x2(qp[:, half * LANES:(half + 1) * LANES], cos, sa, sb).astype(BF16)
            for sub in range(2):
                hd = 2 * pair + sub
                q_out[:, hd * MLA_QK_PAD:hd * MLA_QK_PAD + MLA_NOPE] = (
                    qn[:, sub * MLA_NOPE:(sub + 1) * MLA_NOPE])
                q_out[:, hd * MLA_QK_PAD + MLA_NOPE:(hd + 1) * MLA_QK_PAD] = (
                    _rope_tile_for_head(tile, sub))


def _mla_q(x, g, wdq, lora_g, wuq, qn_g, qr_g, cos, sa, sb):
    m = x.shape[0]
    return pl.pallas_call(
        _q_kernel,
        out_shape=jax.ShapeDtypeStruct((m, HEADS * MLA_QK_PAD), BF16),
        grid=(m // TM_MLA,),
        in_specs=[_rows(D_MODEL, TM_MLA), _resident((1, D_MODEL)), _resident((D_MODEL, Q_LORA)),
                  _resident((1, Q_LORA)), _resident(wuq.shape),
                  _resident((1, 2 * MLA_NOPE)), _resident((1, 2 * LANES)),
                  _rows(LANES, TM_MLA), _rows(LANES, TM_MLA), _rows(LANES, TM_MLA)],
        out_specs=_rows(HEADS * MLA_QK_PAD, TM_MLA),
        compiler_params=_params("parallel"),
        name="mla_q",
    )(x, g, wdq, lora_g, wuq, qn_g, qr_g, cos, sa, sb)


def _flash_kernel(q_ref, k_ref, v_ref, o_ref, m_ref, acc_ref):
    tiles = q_ref.shape[0] // TQ
    i = pl.program_id(2)
    pair = (i, tiles - 1 - i)
    m_ref[...] = jnp.full_like(m_ref, NEG_BIG)
    acc_ref[...] = jnp.zeros_like(acc_ref)

    def item(k):
        if k < tiles - 1:
            second = k >= i
            return (jnp.where(second, 1, 0), jnp.where(second, pair[1], pair[0]),
                    jnp.where(second, k - i, k), False)
        which = k - (tiles - 1)
        return which, pair[which], pair[which], True

    def scores(k):
        _, qt, kt, _ = item(k)
        rows = pl.ds(pl.multiple_of(qt * TQ, TQ), TQ)
        keys = pl.ds(pl.multiple_of(kt * TQ, TQ), TQ)
        return [_dot_nt(q_ref[rows, hd * MLA_QK_PAD:(hd + 1) * MLA_QK_PAD],
                        k_ref[keys, hd * MLA_QK_PAD:(hd + 1) * MLA_QK_PAD])
                for hd in range(HEADS_PER_STEP)]

    def consume(k, s_heads):
        which, _, kt, diagonal = item(k)
        keys = pl.ds(pl.multiple_of(kt * TQ, TQ), TQ)
        for hd, s in enumerate(s_heads):
            if diagonal:
                row = lax.broadcasted_iota(jnp.int32, (TQ, TQ), 0)
                col = lax.broadcasted_iota(jnp.int32, (TQ, TQ), 1)
                s = jnp.where(col <= row, s, NEG_BIG)
            m_prev = m_ref[which, hd]
            m_new = jnp.maximum(m_prev, jnp.max(s, axis=1, keepdims=True))
            alpha = jnp.exp2(m_prev - m_new)
            p = jnp.exp2(s - _lane_tile(m_new, TQ // LANES))
            pv = _dot(p.astype(BF16), v_ref[keys, hd * V_PAD:(hd + 1) * V_PAD])
            acc_ref[which, hd] = _lane_tile(alpha, V_PAD // LANES) * acc_ref[which, hd] + pv
            m_ref[which, hd] = m_new

    n_items = tiles + 1
    pending = [scores(k) for k in range(SCORES_AHEAD)]
    for k in range(n_items):
        if k + SCORES_AHEAD < n_items:
            pending.append(scores(k + SCORES_AHEAD))
        consume(k, pending.pop(0))

    for which in range(2):
        rows = pl.ds(pl.multiple_of(pair[which] * TQ, TQ), TQ)
        for hd in range(HEADS_PER_STEP):
            acc = acc_ref[which, hd]
            o_ref[rows, hd * MLA_V:(hd + 1) * MLA_V] = (acc[:, :MLA_V] / acc[:, MLA_V:]).astype(BF16)


def _flash(q, k, v, batch, seq):
    m = q.shape[0]
    g = HEADS_PER_STEP
    tiles = seq // TQ
    assert tiles % 2 == 0, "query tiles are processed in (i, tiles-1-i) pairs"
    return pl.pallas_call(
        _flash_kernel,
        out_shape=jax.ShapeDtypeStruct((m, HEADS * MLA_V), BF16),
        grid=(batch, HEADS // g, tiles // 2),
        in_specs=[
            pl.BlockSpec((seq, g * MLA_QK_PAD), lambda b, h, i: (b, h)),
            pl.BlockSpec((seq, g * MLA_QK_PAD), lambda b, h, i: (b, h)),
            pl.BlockSpec((seq, g * V_PAD), lambda b, h, i: (b, h)),
        ],
        out_specs=pl.BlockSpec((seq, g * MLA_V), lambda b, h, i: (b, h)),
        scratch_shapes=[pltpu.VMEM((2, g, TQ, LANES), F32),
                        pltpu.VMEM((2, g, TQ, V_PAD), F32)],
        compiler_params=_params("parallel", "parallel", "arbitrary"),
        name="flash_attention",
    )(q, k, v)


def _row(g, repeat=1):
    return jnp.tile(g.reshape(1, -1).astype(F32), (1, repeat))


def kernel(x, positions, norm_g, ffn_w_gate, ffn_w_up, ffn_w_down, ret_w_in, ret_gn_g, ret_w_o,
           kv_norm_g, kv_w_down, kv_latent_norm_g, kv_w_up, k_nope_norm_g, k_rope_norm_g,
           mla_w_dq, mla_q_lora_norm_g, mla_w_uq, mla_q_nope_norm_g, mla_q_rope_norm_g, mla_w_o):
    batch, seq, d = x.shape
    depth = norm_g.shape[0]
    n_self = ret_w_in.shape[0]
    m = batch * seq
    x = x.reshape(m, d)
    rcos, rsin, mcos, msa, msb = _rope_tables(positions.reshape(m, 1))
    wg_all = ffn_w_gate.astype(BF16)
    wu_all = ffn_w_up.astype(BF16)
    wd_all = ffn_w_down.astype(BF16)

    def ffn(x, layer, i, mix=None):
        return _ffn(x, _row(norm_g[layer, 2 * i]), wg_all, wu_all, wd_all, layer, i, mix)

    k_shared = v_shared = None
    for layer in range(depth):
        x = ffn(x, layer, 0)
        g_mix = _row(norm_g[layer, 1])
        if layer < n_self:
            proj, kt = _ret_proj(x, g_mix, ret_w_in[layer].astype(BF16), rcos, rsin)
            mixed = _retention(proj, kt, ret_gn_g[layer].reshape(HEADS, 1, RET_DV), batch, seq)
            w_o = ret_w_o[layer].astype(BF16)
        else:
            j = layer - n_self
            wuq = mla_w_uq[j].reshape(Q_LORA, HEADS, MLA_NOPE + MLA_ROPE)
            wuq = jnp.concatenate([wuq[:, :, :MLA_NOPE].reshape(Q_LORA, Q_NOPE_COLS),
                                   wuq[:, :, MLA_NOPE:].reshape(Q_LORA, HEADS * MLA_ROPE)], axis=1)
            q = _mla_q(x, g_mix, mla_w_dq[j].astype(BF16), _row(mla_q_lora_norm_g[j]),
                       wuq.astype(BF16), _row(mla_q_nope_norm_g[j], 2), _row(mla_q_rope_norm_g[j], 4),
                       mcos, msa, msb)
            mixed = _flash(q, k_shared, v_shared, batch, seq)
            w_o = mla_w_o[j].astype(BF16)
        x = ffn(x, layer, 1, mix=(mixed, w_o))
        if layer == n_self - 1:
            wup = kv_w_up.reshape(KV_LORA, HEADS, MLA_NOPE + MLA_V)
            wk = wup[:, :, :MLA_NOPE].reshape(KV_LORA, HEADS * MLA_NOPE).astype(BF16)
            wv = wup[:, :, MLA_NOPE:].reshape(KV_LORA, HEADS * MLA_V).astype(BF16)
            wdr = jnp.tile(kv_w_down[:, KV_LORA:], (1, 2)).astype(BF16)
            k_shared, v_shared = _shared_kv(
                x, _row(kv_norm_g), kv_w_down[:, :KV_LORA].astype(BF16), wdr,
                _row(kv_latent_norm_g), wk, wv, _row(k_nope_norm_g, 2), _row(k_rope_norm_g, 2),
                mcos, msa, msb)
    return x.reshape(batch, seq, d)
```

```python
import math

import jax
import jax.numpy as jnp
from jax import lax
from jax.experimental import pallas as pl
from jax.experimental.pallas import tpu as pltpu

F32 = jnp.float32
BF16 = jnp.bfloat16

LANES = 128

D_MODEL = 1024
D_FF = 2816
HEADS = 8
RET_DK = 128
RET_DV = 256
RET_QK = HEADS * RET_DK
RET_PROJ = 2 * RET_QK + 2 * HEADS * RET_DV
RET_V0 = HEADS * RET_DV
RET_Q0 = 2 * HEADS * RET_DV
MLA_NOPE = 128
MLA_ROPE = 64
MLA_V = 128
MLA_QK_PAD = 2 * LANES
V_PAD = 2 * LANES
HEADS_PER_STEP = 2
Q_LORA = 384
KV_LORA = 256
ROPE_BASE = 10000.0
EPS = 1e-6

VMEM_LIMIT = 56 * 1024 * 1024

TM = 512
TM_FFN = 1024
TM_MLA = 1024
FF_CHUNK = 256
RET_CHUNK = 256
RET_HEADS_PER_STEP = 2
TQ = 512
SCORES_AHEAD = 2
NEG_BIG = -1e30


def _dot(a, b):
    return jnp.dot(a, b, preferred_element_type=F32)


def _dot_nt(a, b):
    return lax.dot_general(a, b, (((1,), (1,)), ((), ())), preferred_element_type=F32)


def _dot_tn(a, b):
    return lax.dot_general(a, b, (((0,), (0,)), ((), ())), preferred_element_type=F32)


def _rms(x, g, n):
    ms = jnp.sum(x * x, axis=-1, keepdims=True) * (1.0 / n)
    return x * lax.rsqrt(ms + EPS) * g


def _silu(x):
    return x / (1.0 + jnp.exp2(x * -math.log2(math.e)))


def _lane_tile(x, n):
    return jnp.concatenate([x] * n, axis=1)


def _group_mean_matrix(width, group):
    shift = group.bit_length() - 1
    r = lax.broadcasted_iota(jnp.int32, (width, width), 0) >> shift
    c = lax.broadcasted_iota(jnp.int32, (width, width), 1) >> shift
    return jnp.where(r == c, 1.0 / group, 0.0).astype(BF16)


def _group_rms(x, mean_matrix, g):
    ms = _dot((x * x).astype(BF16), mean_matrix)
    return x * lax.rsqrt(ms + EPS) * g


def _params(*sem):
    return pltpu.CompilerParams(dimension_semantics=sem, vmem_limit_bytes=VMEM_LIMIT)


def _resident(shape):
    nd = len(shape)
    return pl.BlockSpec(shape, lambda *_: (0,) * nd, pipeline_mode=pl.Buffered(1))


def _rows(width, tm=TM):
    return pl.BlockSpec((tm, width), lambda i: (i, 0))


def _rope_tables_kernel(pos_ref, rcos_ref, rsin_ref, mcos_ref, msa_ref, msb_ref):
    pos = pos_ref[...].astype(F32)
    lane = lax.broadcasted_iota(jnp.int32, (1, LANES), 1)
    is_ret = lane < 64
    freq = jnp.where(is_ret, lane, lane & 31).astype(F32)
    step = jnp.where(is_ret, -2.0 / RET_DK, -2.0 / MLA_ROPE) * math.log(ROPE_BASE)
    ang = pos * jnp.exp(freq * step)
    c = jnp.cos(ang)
    s = jnp.sin(ang)
    rcos_ref[...] = jnp.where(is_ret, c, pltpu.roll(c, 64, 1))
    rsin_ref[...] = jnp.where(is_ret, -s, pltpu.roll(s, 64, 1))

    def spread(t):
        quarter = lane >> 5
        return jnp.where(quarter == 0, pltpu.roll(t, 64, 1),
                         jnp.where(quarter == 1, pltpu.roll(t, 96, 1),
                                   jnp.where(quarter == 2, t, pltpu.roll(t, 32, 1))))

    s = spread(s)
    first_half = (lane & 63) < 32
    mcos_ref[...] = spread(c)
    msa_ref[...] = jnp.where(first_half, -s, 0.0)
    msb_ref[...] = jnp.where(first_half, 0.0, s)


def _rope_tables(pos):
    m = pos.shape[0]
    tm = 1024
    tab = jax.ShapeDtypeStruct((m, LANES), F32)
    spec = pl.BlockSpec((tm, LANES), lambda i: (i, 0))
    return pl.pallas_call(
        _rope_tables_kernel,
        out_shape=(tab,) * 5,
        grid=(m // tm,),
        in_specs=[pl.BlockSpec((tm, 1), lambda i: (i, 0))],
        out_specs=(spec,) * 5,
        compiler_params=_params("parallel"),
        name="rope_tables",
    )(pos)


def _rope128(x, cos, sin):
    return x * cos + pltpu.roll(x, 64, 1) * sin


def _rope64x2(x, cos, sa, sb):
    return x * cos + pltpu.roll(x, 96, 1) * sa + pltpu.roll(x, 32, 1) * sb


def _ffn_body(x, g_ref, wg_ref, wu_ref, wd_ref, o_ref):
    h = _rms(x, g_ref[...], D_MODEL).astype(BF16)
    acc = None
    for c in range(D_FF // FF_CHUNK):
        sl = slice(c * FF_CHUNK, (c + 1) * FF_CHUNK)
        a = (_silu(_dot(h, wg_ref[:, sl])) * _dot(h, wu_ref[:, sl])).astype(BF16)
        d = _dot(a, wd_ref[sl, :])
        acc = d if acc is None else acc + d
    o_ref[...] = x + 0.5 * acc


def _ffn_kernel(x_ref, g_ref, wg_ref, wu_ref, wd_ref, o_ref):
    _ffn_body(x_ref[...], g_ref, wg_ref, wu_ref, wd_ref, o_ref)


def _mix_ffn_kernel(x_ref, a_ref, wo_ref, g_ref, wg_ref, wu_ref, wd_ref, o_ref):
    _ffn_body(x_ref[...] + _dot(a_ref[...], wo_ref[...]), g_ref, wg_ref, wu_ref, wd_ref, o_ref)


def _ret_mix_ffn_kernel(x_ref, ret_ref, gate_ref, gn_ref, wo_ref, g_ref, wg_ref, wu_ref, wd_ref,
                        o_ref):
    x = x_ref[...]
    for hd in range(HEADS):
        cols = slice(hd * RET_DV, (hd + 1) * RET_DV)
        o = ret_ref[:, cols].astype(F32)
        oc = o - jnp.mean(o, axis=-1, keepdims=True)
        var = jnp.mean(oc * oc, axis=-1, keepdims=True)
        on = oc * lax.rsqrt(var + EPS) * gn_ref[:, cols]
        a = (_silu(gate_ref[:, cols].astype(F32)) * on).astype(BF16)
        x = x + _dot(a, wo_ref[cols, :])
    _ffn_body(x, g_ref, wg_ref, wu_ref, wd_ref, o_ref)


def _ffn(x, g, wg, wu, wd, layer, half, mix=None, ret_mix=None):
    m = x.shape[0]

    def stacked(rows, cols):
        return pl.BlockSpec((None, None, rows, cols), lambda i: (layer, half, 0, 0),
                            pipeline_mode=pl.Buffered(1))

    w_specs = [_resident((1, D_MODEL)), stacked(D_MODEL, D_FF), stacked(D_MODEL, D_FF),
               stacked(D_FF, D_MODEL)]
    if ret_mix is not None:
        o, proj, gn, wo = ret_mix
        tm = TM
        name, body, args = "ret_mix_ffn", _ret_mix_ffn_kernel, (x, o, proj, gn, wo)
        specs = [_rows(D_MODEL, tm), _rows(HEADS * RET_DV, tm), _rows(HEADS * RET_DV, tm),
                 _resident(gn.shape), _resident(wo.shape)]
    elif mix is not None:
        a, wo = mix
        tm = TM_FFN
        name, body, args = "mix_ffn", _mix_ffn_kernel, (x, a, wo)
        specs = [_rows(D_MODEL, tm), _rows(a.shape[1], tm), _resident(wo.shape)]
    else:
        tm = TM_FFN
        name, body, args, specs = "ffn", _ffn_kernel, (x,), [_rows(D_MODEL, tm)]
    return pl.pallas_call(
        body,
        out_shape=jax.ShapeDtypeStruct((m, D_MODEL), F32),
        grid=(m // tm,),
        in_specs=specs + w_specs,
        out_specs=_rows(D_MODEL, tm),
        compiler_params=_params("parallel"),
        name=name,
    )(*args, g, wg, wu, wd)


def _ret_proj_kernel(x_ref, g_ref, w_ref, cos_ref, sin_ref, o_ref, kt_ref):
    h = _rms(x_ref[...], g_ref[...], D_MODEL).astype(BF16)
    cos = cos_ref[...]
    sin = sin_ref[...]
    q = _dot(h, w_ref[:, :RET_QK])
    k = _dot(h, w_ref[:, RET_QK:2 * RET_QK])
    for hd in range(HEADS):
        head = slice(hd * RET_DK, (hd + 1) * RET_DK)
        o_ref[:, RET_Q0 + hd * RET_DK:RET_Q0 + (hd + 1) * RET_DK] = (
            _rope128(q[:, head], cos, sin).astype(BF16))
        kh = _rope128(k[:, head], cos, sin) * RET_DK ** -0.5
        for c in range(TM // RET_CHUNK):
            kt_ref[c, head, :] = kh[c * RET_CHUNK:(c + 1) * RET_CHUNK, :].T.astype(BF16)
    v0 = 2 * RET_QK
    g0 = v0 + HEADS * RET_DV
    o_ref[:, RET_V0:RET_Q0] = _dot(h, w_ref[:, v0:g0]).astype(BF16)
    o_ref[:, :RET_V0] = _dot(h, w_ref[:, g0:]).astype(BF16)


def _ret_proj(x, g, w, cos, sin):
    m = x.shape[0]
    chunks = TM // RET_CHUNK
    return pl.pallas_call(
        _ret_proj_kernel,
        out_shape=(jax.ShapeDtypeStruct((m, RET_PROJ - RET_QK), BF16),
                   jax.ShapeDtypeStruct((m // RET_CHUNK, RET_QK, RET_CHUNK), BF16)),
        grid=(m // TM,),
        in_specs=[_rows(D_MODEL), _resident((1, D_MODEL)), _resident((D_MODEL, RET_PROJ)),
                  _rows(LANES), _rows(LANES)],
        out_specs=(_rows(RET_PROJ - RET_QK),
                   pl.BlockSpec((chunks, RET_QK, RET_CHUNK), lambda i: (i, 0, 0))),
        compiler_params=_params("parallel"),
        name="ret_proj",
    )(x, g, w, cos, sin)


def _retention_kernel(q_ref, kt_ref, v_ref, o_ref, state_ref):
    c = RET_CHUNK
    seq = q_ref.shape[0]

    def index(shape, axis):
        return lax.broadcasted_iota(jnp.int32, shape, axis).astype(F32)

    def decays(sub):
        hd = (pl.program_id(1) * RET_HEADS_PER_STEP + sub).astype(F32)

        def log_decay(shape):
            return jnp.log1p(-jnp.exp2(jnp.full(shape, -5.0, F32) - hd))

        diff = index((c, c), 0) - index((c, c), 1)
        d_intra = jnp.where(diff >= 0, jnp.exp(log_decay((c, c)) * jnp.maximum(diff, 0.0)), 0.0)
        q_decay = jnp.exp(log_decay((c, RET_DV)) * (index((c, RET_DV), 0) + 1.0))
        k_decay = jnp.exp(log_decay((RET_DK, c)) * (c - 1.0 - index((RET_DK, c), 1)))
        chunk_decay = jnp.exp(log_decay((RET_DK, RET_DV)) * c)
        return d_intra, q_decay, k_decay, chunk_decay

    per_head = [decays(sub) for sub in range(RET_HEADS_PER_STEP)]
    state_ref[...] = jnp.zeros_like(state_ref)

    def body(t, carry):
        off = pl.multiple_of(t * c, c)
        for sub, (d_intra, q_decay, k_decay, chunk_decay) in enumerate(per_head):
            qk = slice(sub * RET_DK, (sub + 1) * RET_DK)
            vo = slice(sub * RET_DV, (sub + 1) * RET_DV)
            q = q_ref[pl.ds(off, c), qk]
            kt = kt_ref[t, qk, :]
            v = v_ref[pl.ds(off, c), vo]
            state = state_ref[sub]
            s = _dot(q, kt) * d_intra
            o = _dot(s.astype(BF16), v) + _dot(q, state.astype(BF16)) * q_decay
            ktd = (kt.astype(F32) * k_decay).astype(BF16)
            state_ref[sub] = state * chunk_decay + _dot(ktd, v)
            o_ref[pl.ds(off, c), vo] = o.astype(BF16)
        return carry

    lax.fori_loop(0, seq // c, body, 0, unroll=4)


def _retention(proj, kt, batch, seq):
    m = proj.shape[0]
    g = RET_HEADS_PER_STEP
    qblk = RET_Q0 // (g * RET_DK)
    vblk = RET_V0 // (g * RET_DV)
    return pl.pallas_call(
        _retention_kernel,
        out_shape=jax.ShapeDtypeStruct((m, HEADS * RET_DV), BF16),
        grid=(batch, HEADS // g),
        in_specs=[
            pl.BlockSpec((seq, g * RET_DK), lambda b, h: (b, qblk + h)),
            pl.BlockSpec((seq // RET_CHUNK, g * RET_DK, RET_CHUNK), lambda b, h: (b, h, 0)),
            pl.BlockSpec((seq, g * RET_DV), lambda b, h: (b, vblk + h)),
        ],
        out_specs=pl.BlockSpec((seq, g * RET_DV), lambda b, h: (b, h)),
        scratch_shapes=[pltpu.VMEM((g, RET_DK, RET_DV), F32)],
        compiler_params=_params("parallel", "parallel"),
        name="retention",
    )(proj, kt, proj)


def _rope_tile_for_head(tile, hd):
    lane = lax.broadcasted_iota(jnp.int32, tile.shape, 1)
    keep = (lane < MLA_ROPE) if hd % 2 == 0 else (lane >= MLA_ROPE)
    return jnp.where(keep, tile, jnp.zeros_like(tile))


def _kv_kernel(x_ref, g_ref, wdc_ref, wdr_ref, lat_g_ref, wk_ref, wv_ref, kn_g_ref, kr_g_ref,
               cos_ref, sa_ref, sb_ref, k_out, v_out):
    h = _rms(x_ref[...], g_ref[...], D_MODEL).astype(BF16)
    lat = _rms(_dot(h, wdc_ref[...]), lat_g_ref[...], KV_LORA).astype(BF16)
    pe = _rms(_dot(h, wdr_ref[...]), kr_g_ref[...], LANES)
    pe = _rope64x2(pe, cos_ref[...], sa_ref[...], sb_ref[...]).astype(BF16)
    pe_tiles = (_rope_tile_for_head(pe, 0), _rope_tile_for_head(pe, 1))
    ones = jnp.ones((pe.shape[0], V_PAD - MLA_V), BF16)
    mean_nope = _group_mean_matrix(2 * MLA_NOPE, MLA_NOPE)
    kn_g = kn_g_ref[...]
    for pair in range(HEADS // 2):
        lo = pair * 2 * MLA_NOPE
        kn = _group_rms(_dot(lat, wk_ref[:, lo:lo + 2 * MLA_NOPE]), mean_nope, kn_g)
        kn = kn.astype(BF16)
        vv = _dot(lat, wv_ref[:, pair * 2 * MLA_V:(pair + 1) * 2 * MLA_V]).astype(BF16)
        for sub in range(2):
            hd = 2 * pair + sub
            k_out[:, hd * MLA_QK_PAD:hd * MLA_QK_PAD + MLA_NOPE] = (
                kn[:, sub * MLA_NOPE:(sub + 1) * MLA_NOPE])
            k_out[:, hd * MLA_QK_PAD + MLA_NOPE:(hd + 1) * MLA_QK_PAD] = pe_tiles[sub]
            v_out[:, hd * V_PAD:hd * V_PAD + MLA_V] = vv[:, sub * MLA_V:(sub + 1) * MLA_V]
            v_out[:, hd * V_PAD + MLA_V:(hd + 1) * V_PAD] = ones


def _shared_kv(x, g, wdc, wdr, lat_g, wk, wv, kn_g, kr_g, cos, sa, sb):
    m = x.shape[0]
    return pl.pallas_call(
        _kv_kernel,
        out_shape=(jax.ShapeDtypeStruct((m, HEADS * MLA_QK_PAD), BF16),
                   jax.ShapeDtypeStruct((m, HEADS * V_PAD), BF16)),
        grid=(m // TM_MLA,),
        in_specs=[_rows(D_MODEL, TM_MLA), _resident((1, D_MODEL)), _resident((D_MODEL, KV_LORA)),
                  _resident((D_MODEL, LANES)), _resident((1, KV_LORA)),
                  _resident((KV_LORA, HEADS * MLA_NOPE)), _resident((KV_LORA, HEADS * MLA_V)),
                  _resident((1, 2 * MLA_NOPE)), _resident((1, LANES)),
                  _rows(LANES, TM_MLA), _rows(LANES, TM_MLA), _rows(LANES, TM_MLA)],
        out_specs=(_rows(HEADS * MLA_QK_PAD, TM_MLA), _rows(HEADS * V_PAD, TM_MLA)),
        compiler_params=_params("parallel"),
        name="shared_kv",
    )(x, g, wdc, wdr, lat_g, wk, wv, kn_g, kr_g, cos, sa, sb)


Q_SCALE = (MLA_NOPE + MLA_ROPE) ** -0.5 * math.log2(math.e)
Q_NOPE_COLS = HEADS * MLA_NOPE


def _q_kernel(x_ref, g_ref, wdq_ref, lora_g_ref, wuq_ref, qn_g_ref, qr_g_ref,
              cos_ref, sa_ref, sb_ref, q_out):
    h = _rms(x_ref[...], g_ref[...], D_MODEL).astype(BF16)
    cq = _rms(_dot(h, wdq_ref[...]), lora_g_ref[...], Q_LORA).astype(BF16)
    q = _dot(cq, wuq_ref[...])
    mean_nope = _group_mean_matrix(2 * MLA_NOPE, MLA_NOPE)
    mean_rope = _group_mean_matrix(2 * LANES, MLA_ROPE)
    qn_g = qn_g_ref[...] * Q_SCALE
    qr_g = qr_g_ref[...] * Q_SCALE
    cos = cos_ref[...]
    sa = sa_ref[...]
    sb = sb_ref[...]
    for quad in range(HEADS // 4):
        lo = Q_NOPE_COLS + quad * 2 * LANES
        qp = _group_rms(q[:, lo:lo + 2 * LANES], mean_rope, qr_g)
        for half in range(2):
            pair = 2 * quad + half
            lo = pair * 2 * MLA_NOPE
            qn = _group_rms(q[:, lo:lo + 2 * MLA_NOPE], mean_nope, qn_g).astype(BF16)
            tile = _rope64x2(qp[:, half * LANES:(half + 1) * LANES], cos, sa, sb).astype(BF16)
            for sub in range(2):
                hd = 2 * pair + sub
                q_out[:, hd * MLA_QK_PAD:hd * MLA_QK_PAD + MLA_NOPE] = (
                    qn[:, sub * MLA_NOPE:(sub + 1) * MLA_NOPE])
                q_out[:, hd * MLA_QK_PAD + MLA_NOPE:(hd + 1) * MLA_QK_PAD] = (
                    _rope_tile_for_head(tile, sub))


def _mla_q(x, g, wdq, lora_g, wuq, qn_g, qr_g, cos, sa, sb):
    m = x.shape[0]
    return pl.pallas_call(
        _q_kernel,
        out_shape=jax.ShapeDtypeStruct((m, HEADS * MLA_QK_PAD), BF16),
        grid=(m // TM_MLA,),
        in_specs=[_rows(D_MODEL, TM_MLA), _resident((1, D_MODEL)), _resident((D_MODEL, Q_LORA)),
                  _resident((1, Q_LORA)), _resident(wuq.shape),
                  _resident((1, 2 * MLA_NOPE)), _resident((1, 2 * LANES)),
                  _rows(LANES, TM_MLA), _rows(LANES, TM_MLA), _rows(LANES, TM_MLA)],
        out_specs=_rows(HEADS * MLA_QK_PAD, TM_MLA),
        compiler_params=_params("parallel"),
        name="mla_q",
    )(x, g, wdq, lora_g, wuq, qn_g, qr_g, cos, sa, sb)


def _flash_kernel(q_ref, k_ref, v_ref, o_ref, m_ref, acc_ref):
    tiles = q_ref.shape[0] // TQ
    i = pl.program_id(2)
    pair = (i, tiles - 1 - i)
    m_ref[...] = jnp.full_like(m_ref, NEG_BIG)
    acc_ref[...] = jnp.zeros_like(acc_ref)

    def item(k):
        if k < tiles - 1:
            second = k >= i
            return (jnp.where(second, 1, 0), jnp.where(second, pair[1], pair[0]),
                    jnp.where(second, k - i, k), False)
        which = k - (tiles - 1)
        return which, pair[which], pair[which], True

    def scores(k):
        _, qt, kt, _ = item(k)
        rows = pl.ds(pl.multiple_of(qt * TQ, TQ), TQ)
        keys = pl.ds(pl.multiple_of(kt * TQ, TQ), TQ)
        return [_dot_nt(q_ref[rows, hd * MLA_QK_PAD:(hd + 1) * MLA_QK_PAD],
                        k_ref[keys, hd * MLA_QK_PAD:(hd + 1) * MLA_QK_PAD])
                for hd in range(HEADS_PER_STEP)]

    def consume(k, s_heads):
        which, _, kt, diagonal = item(k)
        keys = pl.ds(pl.multiple_of(kt * TQ, TQ), TQ)
        for hd, s in enumerate(s_heads):
            if diagonal:
                row = lax.broadcasted_iota(jnp.int32, (TQ, TQ), 0)
                col = lax.broadcasted_iota(jnp.int32, (TQ, TQ), 1)
                s = jnp.where(col <= row, s, NEG_BIG)
            m_prev = m_ref[which, hd]
            m_new = jnp.maximum(m_prev, jnp.max(s, axis=1, keepdims=True))
            alpha = jnp.exp2(m_prev - m_new)
            p = jnp.exp2(s - _lane_tile(m_new, TQ // LANES))
            pv = _dot(p.astype(BF16), v_ref[keys, hd * V_PAD:(hd + 1) * V_PAD])
            acc_ref[which, hd] = _lane_tile(alpha, V_PAD // LANES) * acc_ref[which, hd] + pv
            m_ref[which, hd] = m_new

    n_items = tiles + 1
    pending = [scores(k) for k in range(SCORES_AHEAD)]
    for k in range(n_items):
        if k + SCORES_AHEAD < n_items:
            pending.append(scores(k + SCORES_AHEAD))
        consume(k, pending.pop(0))

    for which in range(2):
        rows = pl.ds(pl.multiple_of(pair[which] * TQ, TQ), TQ)
        for hd in range(HEADS_PER_STEP):
            acc = acc_ref[which, hd]
            o_ref[rows, hd * MLA_V:(hd + 1) * MLA_V] = (acc[:, :MLA_V] / acc[:, MLA_V:]).astype(BF16)


def _flash(q, k, v, batch, seq):
    m = q.shape[0]
    g = HEADS_PER_STEP
    tiles = seq // TQ
    assert tiles % 2 == 0, "query tiles are processed in (i, tiles-1-i) pairs"
    return pl.pallas_call(
        _flash_kernel,
        out_shape=jax.ShapeDtypeStruct((m, HEADS * MLA_V), BF16),
        grid=(batch, HEADS // g, tiles // 2),
        in_specs=[
            pl.BlockSpec((seq, g * MLA_QK_PAD), lambda b, h, i: (b, h)),
            pl.BlockSpec((seq, g * MLA_QK_PAD), lambda b, h, i: (b, h)),
            pl.BlockSpec((seq, g * V_PAD), lambda b, h, i: (b, h)),
        ],
        out_specs=pl.BlockSpec((seq, g * MLA_V), lambda b, h, i: (b, h)),
        scratch_shapes=[pltpu.VMEM((2, g, TQ, LANES), F32),
                        pltpu.VMEM((2, g, TQ, V_PAD), F32)],
        compiler_params=_params("parallel", "parallel", "arbitrary"),
        name="flash_attention",
    )(q, k, v)


def _row(g, repeat=1):
    return jnp.tile(g.reshape(1, -1).astype(F32), (1, repeat))


def kernel(x, positions, norm_g, ffn_w_gate, ffn_w_up, ffn_w_down, ret_w_in, ret_gn_g, ret_w_o,
           kv_norm_g, kv_w_down, kv_latent_norm_g, kv_w_up, k_nope_norm_g, k_rope_norm_g,
           mla_w_dq, mla_q_lora_norm_g, mla_w_uq, mla_q_nope_norm_g, mla_q_rope_norm_g, mla_w_o):
    batch, seq, d = x.shape
    depth = norm_g.shape[0]
    n_self = ret_w_in.shape[0]
    m = batch * seq
    x = x.reshape(m, d)
    rcos, rsin, mcos, msa, msb = _rope_tables(positions.reshape(m, 1))
    wg_all = ffn_w_gate.astype(BF16)
    wu_all = ffn_w_up.astype(BF16)
    wd_all = ffn_w_down.astype(BF16)

    def ffn(x, layer, i, **mix):
        return _ffn(x, _row(norm_g[layer, 2 * i]), wg_all, wu_all, wd_all, layer, i, **mix)

    k_shared = v_shared = None
    for layer in range(depth):
        x = ffn(x, layer, 0)
        g_mix = _row(norm_g[layer, 1])
        if layer < n_self:
            proj, kt = _ret_proj(x, g_mix, ret_w_in[layer].astype(BF16), rcos, rsin)
            mix = dict(ret_mix=(_retention(proj, kt, batch, seq), proj, _row(ret_gn_g[layer]),
                                ret_w_o[layer].astype(BF16)))
        else:
            j = layer - n_self
            wuq = mla_w_uq[j].reshape(Q_LORA, HEADS, MLA_NOPE + MLA_ROPE)
            wuq = jnp.concatenate([wuq[:, :, :MLA_NOPE].reshape(Q_LORA, Q_NOPE_COLS),
                                   wuq[:, :, MLA_NOPE:].reshape(Q_LORA, HEADS * MLA_ROPE)], axis=1)
            q = _mla_q(x, g_mix, mla_w_dq[j].astype(BF16), _row(mla_q_lora_norm_g[j]),
                       wuq.astype(BF16), _row(mla_q_nope_norm_g[j], 2), _row(mla_q_rope_norm_g[j], 4),
                       mcos, msa, msb)
            mix = dict(mix=(_flash(q, k_shared, v_shared, batch, seq), mla_w_o[j].astype(BF16)))
        x = ffn(x, layer, 1, **mix)
        if layer == n_self - 1:
            wup = kv_w_up.reshape(KV_LORA, HEADS, MLA_NOPE + MLA_V)
            wk = wup[:, :, :MLA_NOPE].reshape(KV_LORA, HEADS * MLA_NOPE).astype(BF16)
            wv = wup[:, :, MLA_NOPE:].reshape(KV_LORA, HEADS * MLA_V).astype(BF16)
            wdr = jnp.tile(kv_w_down[:, KV_LORA:], (1, 2)).astype(BF16)
            k_shared, v_shared = _shared_kv(
                x, _row(kv_norm_g), kv_w_down[:, :KV_LORA].astype(BF16), wdr,
                _row(kv_latent_norm_g), wk, wv, _row(k_nope_norm_g, 2), _row(k_rope_norm_g, 2),
                mcos, msa, msb)
    return x.reshape(batch, seq, d)
```

```python
import math

import jax
import jax.numpy as jnp
from jax import lax
from jax.experimental import pallas as pl
from jax.experimental.pallas import tpu as pltpu

F32 = jnp.float32
BF16 = jnp.bfloat16

LANES = 128

D_MODEL = 1024
D_FF = 2816
HEADS = 8
RET_DK = 128
RET_DV = 256
RET_QK = HEADS * RET_DK
RET_PROJ = 2 * RET_QK + 2 * HEADS * RET_DV
RET_V0 = HEADS * RET_DV
RET_Q0 = 2 * HEADS * RET_DV
MLA_NOPE = 128
MLA_ROPE = 64
MLA_V = 128
MLA_QK_PAD = 2 * LANES
V_PAD = 2 * LANES
HEADS_PER_STEP = 2
Q_LORA = 384
KV_LORA = 256
ROPE_BASE = 10000.0
EPS = 1e-6

VMEM_LIMIT = 56 * 1024 * 1024

TM = 512
TM_FFN = 1024
TM_MLA = 1024
FF_CHUNK = 256
RET_CHUNK = 256
RET_HEADS_PER_STEP = 2
TQ = 512
SCORES_AHEAD = 2
NEG_BIG = -1e30


def _dot(a, b):
    return jnp.dot(a, b, preferred_element_type=F32)


def _dot_nt(a, b):
    return lax.dot_general(a, b, (((1,), (1,)), ((), ())), preferred_element_type=F32)


def _dot_tn(a, b):
    return lax.dot_general(a, b, (((0,), (0,)), ((), ())), preferred_element_type=F32)


def _rms(x, g, n):
    ms = jnp.sum(x * x, axis=-1, keepdims=True) * (1.0 / n)
    return x * lax.rsqrt(ms + EPS) * g


def _silu(x):
    return x / (1.0 + jnp.exp2(x * -math.log2(math.e)))


def _lane_tile(x, n):
    return jnp.concatenate([x] * n, axis=1)


def _group_mean_matrix(width, group):
    shift = group.bit_length() - 1
    r = lax.broadcasted_iota(jnp.int32, (width, width), 0) >> shift
    c = lax.broadcasted_iota(jnp.int32, (width, width), 1) >> shift
    return jnp.where(r == c, 1.0 / group, 0.0).astype(BF16)


def _group_rms(x, mean_matrix, g):
    ms = _dot((x * x).astype(BF16), mean_matrix)
    return x * lax.rsqrt(ms + EPS) * g


def _params(*sem):
    return pltpu.CompilerParams(dimension_semantics=sem, vmem_limit_bytes=VMEM_LIMIT)


def _resident(shape):
    nd = len(shape)
    return pl.BlockSpec(shape, lambda *_: (0,) * nd, pipeline_mode=pl.Buffered(1))


def _rows(width, tm=TM):
    return pl.BlockSpec((tm, width), lambda i: (i, 0))


def _rope_tables_kernel(pos_ref, rcos_ref, rsin_ref, mcos_ref, msa_ref, msb_ref):
    pos = pos_ref[...].astype(F32)
    lane = lax.broadcasted_iota(jnp.int32, (1, LANES), 1)
    is_ret = lane < 64
    freq = jnp.where(is_ret, lane, lane & 31).astype(F32)
    step = jnp.where(is_ret, -2.0 / RET_DK, -2.0 / MLA_ROPE) * math.log(ROPE_BASE)
    ang = pos * jnp.exp(freq * step)
    c = jnp.cos(ang)
    s = jnp.sin(ang)
    rcos_ref[...] = jnp.where(is_ret, c, pltpu.roll(c, 64, 1))
    rsin_ref[...] = jnp.where(is_ret, -s, pltpu.roll(s, 64, 1))

    def spread(t):
        quarter = lane >> 5
        return jnp.where(quarter == 0, pltpu.roll(t, 64, 1),
                         jnp.where(quarter == 1, pltpu.roll(t, 96, 1),
                                   jnp.where(quarter == 2, t, pltpu.roll(t, 32, 1))))

    s = spread(s)
    first_half = (lane & 63) < 32
    mcos_ref[...] = spread(c)
    msa_ref[...] = jnp.where(first_half, -s, 0.0)
    msb_ref[...] = jnp.where(first_half, 0.0, s)


def _rope_tables(pos):
    m = pos.shape[0]
    tm = 1024
    tab = jax.ShapeDtypeStruct((m, LANES), F32)
    spec = pl.BlockSpec((tm, LANES), lambda i: (i, 0))
    return pl.pallas_call(
        _rope_tables_kernel,
        out_shape=(tab,) * 5,
        grid=(m // tm,),
        in_specs=[pl.BlockSpec((tm, 1), lambda i: (i, 0))],
        out_specs=(spec,) * 5,
        compiler_params=_params("parallel"),
        name="rope_tables",
    )(pos)


def _rope128(x, cos, sin):
    return x * cos + pltpu.roll(x, 64, 1) * sin


def _rope64x2(x, cos, sa, sb):
    return x * cos + pltpu.roll(x, 96, 1) * sa + pltpu.roll(x, 32, 1) * sb


def _ffn_body(x, g_ref, wg_ref, wu_ref, wd_ref, o_ref):
    h = _rms(x, g_ref[...], D_MODEL).astype(BF16)
    acc = None
    for c in range(D_FF // FF_CHUNK):
        sl = slice(c * FF_CHUNK, (c + 1) * FF_CHUNK)
        a = (_silu(_dot(h, wg_ref[:, sl])) * _dot(h, wu_ref[:, sl])).astype(BF16)
        d = _dot(a, wd_ref[sl, :])
        acc = d if acc is None else acc + d
    o_ref[...] = x + 0.5 * acc


def _ffn_kernel(x_ref, g_ref, wg_ref, wu_ref, wd_ref, o_ref):
    _ffn_body(x_ref[...], g_ref, wg_ref, wu_ref, wd_ref, o_ref)


def _mix_ffn_kernel(x_ref, a_ref, wo_ref, g_ref, wg_ref, wu_ref, wd_ref, o_ref):
    _ffn_body(x_ref[...] + _dot(a_ref[...], wo_ref[...]), g_ref, wg_ref, wu_ref, wd_ref, o_ref)


def _ret_mix_ffn_kernel(x_ref, ret_ref, gate_ref, gn_ref, wo_ref, g_ref, wg_ref, wu_ref, wd_ref,
                        o_ref):
    x = x_ref[...]
    for hd in range(HEADS):
        cols = slice(hd * RET_DV, (hd + 1) * RET_DV)
        o = ret_ref[:, cols].astype(F32)
        oc = o - jnp.mean(o, axis=-1, keepdims=True)
        var = jnp.mean(oc * oc, axis=-1, keepdims=True)
        on = oc * lax.rsqrt(var + EPS) * gn_ref[:, cols]
        a = (_silu(gate_ref[:, cols].astype(F32)) * on).astype(BF16)
        x = x + _dot(a, wo_ref[cols, :])
    _ffn_body(x, g_ref, wg_ref, wu_ref, wd_ref, o_ref)


def _casting_next_weights(body, n_in):
    def kernel(*refs):
        for src, dst in zip(refs[n_in:n_in + 3], refs[n_in + 4:]):
            dst[...] = src[...].astype(BF16)
        body(*refs[:n_in], refs[n_in + 3])
    return kernel


def _ffn(x, g, wg, wu, wd, mix=None, ret_mix=None, cast_next=None):
    m = x.shape[0]
    w_specs = [_resident((1, D_MODEL)), _resident((D_MODEL, D_FF)), _resident((D_MODEL, D_FF)),
               _resident((D_FF, D_MODEL))]
    if ret_mix is not None:
        o, proj, gn, wo = ret_mix
        tm = TM
        name, body, args = "ret_mix_ffn", _ret_mix_ffn_kernel, (x, o, proj, gn, wo)
        specs = [_rows(D_MODEL, tm), _rows(HEADS * RET_DV, tm), _rows(HEADS * RET_DV, tm),
                 _resident(gn.shape), _resident(wo.shape)]
    elif mix is not None:
        a, wo = mix
        tm = TM_FFN
        name, body, args = "mix_ffn", _mix_ffn_kernel, (x, a, wo)
        specs = [_rows(D_MODEL, tm), _rows(a.shape[1], tm), _resident(wo.shape)]
    else:
        tm = TM_FFN
        name, body, args, specs = "ffn", _ffn_kernel, (x,), [_rows(D_MODEL, tm)]
    steps = m // tm
    args = (*args, g, wg, wu, wd)
    in_specs = specs + w_specs
    out_shape = jax.ShapeDtypeStruct((m, D_MODEL), F32)
    out_specs = _rows(D_MODEL, tm)
    if cast_next is not None:
        *stacks, layer, half = cast_next
        slab = D_MODEL // steps
        body = _casting_next_weights(body, len(args))
        args = (*args, *stacks)
        in_specs = in_specs + [pl.BlockSpec((None, None, slab, D_FF),
                                            lambda i: (layer, half, i, 0))] * 3
        out_shape = (out_shape,) + (jax.ShapeDtypeStruct((D_MODEL, D_FF), BF16),) * 3
        out_specs = (out_specs,) + (pl.BlockSpec((slab, D_FF), lambda i: (i, 0)),) * 3
    out = pl.pallas_call(
        body,
        out_shape=out_shape,
        grid=(steps,),
        in_specs=in_specs,
        out_specs=out_specs,
        compiler_params=_params("parallel"),
        name=name,
    )(*args)
    if cast_next is None:
        return out, None
    return out[0], out[1:]


def _ret_proj_kernel(x_ref, g_ref, w_ref, cos_ref, sin_ref, o_ref, kt_ref):
    h = _rms(x_ref[...], g_ref[...], D_MODEL).astype(BF16)
    cos = cos_ref[...]
    sin = sin_ref[...]
    q = _dot(h, w_ref[:, :RET_QK])
    k = _dot(h, w_ref[:, RET_QK:2 * RET_QK])
    for hd in range(HEADS):
        head = slice(hd * RET_DK, (hd + 1) * RET_DK)
        o_ref[:, RET_Q0 + hd * RET_DK:RET_Q0 + (hd + 1) * RET_DK] = (
            _rope128(q[:, head], cos, sin).astype(BF16))
        kh = _rope128(k[:, head], cos, sin) * RET_DK ** -0.5
        for c in range(TM // RET_CHUNK):
            kt_ref[c, head, :] = kh[c * RET_CHUNK:(c + 1) * RET_CHUNK, :].T.astype(BF16)
    v0 = 2 * RET_QK
    g0 = v0 + HEADS * RET_DV
    o_ref[:, RET_V0:RET_Q0] = _dot(h, w_ref[:, v0:g0]).astype(BF16)
    o_ref[:, :RET_V0] = _dot(h, w_ref[:, g0:]).astype(BF16)


def _ret_proj(x, g, w, cos, sin):
    m = x.shape[0]
    chunks = TM // RET_CHUNK
    return pl.pallas_call(
        _ret_proj_kernel,
        out_shape=(jax.ShapeDtypeStruct((m, RET_PROJ - RET_QK), BF16),
                   jax.ShapeDtypeStruct((m // RET_CHUNK, RET_QK, RET_CHUNK), BF16)),
        grid=(m // TM,),
        in_specs=[_rows(D_MODEL), _resident((1, D_MODEL)), _resident((D_MODEL, RET_PROJ)),
                  _rows(LANES), _rows(LANES)],
        out_specs=(_rows(RET_PROJ - RET_QK),
                   pl.BlockSpec((chunks, RET_QK, RET_CHUNK), lambda i: (i, 0, 0))),
        compiler_params=_params("parallel"),
        name="ret_proj",
    )(x, g, w, cos, sin)


def _retention_kernel(q_ref, kt_ref, v_ref, o_ref, state_ref):
    c = RET_CHUNK
    seq = q_ref.shape[0]

    def index(shape, axis):
        return lax.broadcasted_iota(jnp.int32, shape, axis).astype(F32)

    def decays(sub):
        hd = (pl.program_id(1) * RET_HEADS_PER_STEP + sub).astype(F32)

        def log_decay(shape):
            return jnp.log1p(-jnp.exp2(jnp.full(shape, -5.0, F32) - hd))

        diff = index((c, c), 0) - index((c, c), 1)
        d_intra = jnp.where(diff >= 0, jnp.exp(log_decay((c, c)) * jnp.maximum(diff, 0.0)), 0.0)
        q_decay = jnp.exp(log_decay((c, RET_DV)) * (index((c, RET_DV), 0) + 1.0))
        k_decay = jnp.exp(log_decay((RET_DK, c)) * (c - 1.0 - index((RET_DK, c), 1)))
        chunk_decay = jnp.exp(log_decay((RET_DK, RET_DV)) * c)
        return d_intra, q_decay, k_decay, chunk_decay

    per_head = [decays(sub) for sub in range(RET_HEADS_PER_STEP)]
    state_ref[...] = jnp.zeros_like(state_ref)

    def body(t, carry):
        off = pl.multiple_of(t * c, c)
        for sub, (d_intra, q_decay, k_decay, chunk_decay) in enumerate(per_head):
            qk = slice(sub * RET_DK, (sub + 1) * RET_DK)
            vo = slice(sub * RET_DV, (sub + 1) * RET_DV)
            q = q_ref[pl.ds(off, c), qk]
            kt = kt_ref[t, qk, :]
            v = v_ref[pl.ds(off, c), vo]
            state = state_ref[sub]
            s = _dot(q, kt) * d_intra
            o = _dot(s.astype(BF16), v) + _dot(q, state.astype(BF16)) * q_decay
            ktd = (kt.astype(F32) * k_decay).astype(BF16)
            state_ref[sub] = state * chunk_decay + _dot(ktd, v)
            o_ref[pl.ds(off, c), vo] = o.astype(BF16)
        return carry

    lax.fori_loop(0, seq // c, body, 0, unroll=4)


def _retention(proj, kt, batch, seq):
    m = proj.shape[0]
    g = RET_HEADS_PER_STEP
    qblk = RET_Q0 // (g * RET_DK)
    vblk = RET_V0 // (g * RET_DV)
    return pl.pallas_call(
        _retention_kernel,
        out_shape=jax.ShapeDtypeStruct((m, HEADS * RET_DV), BF16),
        grid=(batch, HEADS // g),
        in_specs=[
            pl.BlockSpec((seq, g * RET_DK), lambda b, h: (b, qblk + h)),
            pl.BlockSpec((seq // RET_CHUNK, g * RET_DK, RET_CHUNK), lambda b, h: (b, h, 0)),
            pl.BlockSpec((seq, g * RET_DV), lambda b, h: (b, vblk + h)),
        ],
        out_specs=pl.BlockSpec((seq, g * RET_DV), lambda b, h: (b, h)),
        scratch_shapes=[pltpu.VMEM((g, RET_DK, RET_DV), F32)],
        compiler_params=_params("parallel", "parallel"),
        name="retention",
    )(proj, kt, proj)


def _rope_tile_for_head(tile, hd):
    lane = lax.broadcasted_iota(jnp.int32, tile.shape, 1)
    keep = (lane < MLA_ROPE) if hd % 2 == 0 else (lane >= MLA_ROPE)
    return jnp.where(keep, tile, jnp.zeros_like(tile))


def _kv_kernel(x_ref, g_ref, wdc_ref, wdr_ref, lat_g_ref, wk_ref, wv_ref, kn_g_ref, kr_g_ref,
               cos_ref, sa_ref, sb_ref, k_out, v_out):
    h = _rms(x_ref[...], g_ref[...], D_MODEL).astype(BF16)
    lat = _rms(_dot(h, wdc_ref[...]), lat_g_ref[...], KV_LORA).astype(BF16)
    pe = _rms(_dot(h, wdr_ref[...]), kr_g_ref[...], LANES)
    pe = _rope64x2(pe, cos_ref[...], sa_ref[...], sb_ref[...]).astype(BF16)
    pe_tiles = (_rope_tile_for_head(pe, 0), _rope_tile_for_head(pe, 1))
    ones = jnp.ones((pe.shape[0], V_PAD - MLA_V), BF16)
    mean_nope = _group_mean_matrix(2 * MLA_NOPE, MLA_NOPE)
    kn_g = kn_g_ref[...]
    for pair in range(HEADS // 2):
        lo = pair * 2 * MLA_NOPE
        kn = _group_rms(_dot(lat, wk_ref[:, lo:lo + 2 * MLA_NOPE]), mean_nope, kn_g)
        kn = kn.astype(BF16)
        vv = _dot(lat, wv_ref[:, pair * 2 * MLA_V:(pair + 1) * 2 * MLA_V]).astype(BF16)
        for sub in range(2):
            hd = 2 * pair + sub
            k_out[:, hd * MLA_QK_PAD:hd * MLA_QK_PAD + MLA_NOPE] = (
                kn[:, sub * MLA_NOPE:(sub + 1) * MLA_NOPE])
            k_out[:, hd * MLA_QK_PAD + MLA_NOPE:(hd + 1) * MLA_QK_PAD] = pe_tiles[sub]
            v_out[:, hd * V_PAD:hd * V_PAD + MLA_V] = vv[:, sub * MLA_V:(sub + 1) * MLA_V]
            v_out[:, hd * V_PAD + MLA_V:(hd + 1) * V_PAD] = ones


def _shared_kv(x, g, wdc, wdr, lat_g, wk, wv, kn_g, kr_g, cos, sa, sb):
    m = x.shape[0]
    return pl.pallas_call(
        _kv_kernel,
        out_shape=(jax.ShapeDtypeStruct((m, HEADS * MLA_QK_PAD), BF16),
                   jax.ShapeDtypeStruct((m, HEADS * V_PAD), BF16)),
        grid=(m // TM_MLA,),
        in_specs=[_rows(D_MODEL, TM_MLA), _resident((1, D_MODEL)), _resident((D_MODEL, KV_LORA)),
                  _resident((D_MODEL, LANES)), _resident((1, KV_LORA)),
                  _resident((KV_LORA, HEADS * MLA_NOPE)), _resident((KV_LORA, HEADS * MLA_V)),
                  _resident((1, 2 * MLA_NOPE)), _resident((1, LANES)),
                  _rows(LANES, TM_MLA), _rows(LANES, TM_MLA), _rows(LANES, TM_MLA)],
        out_specs=(_rows(HEADS * MLA_QK_PAD, TM_MLA), _rows(HEADS * V_PAD, TM_MLA)),
        compiler_params=_params("parallel"),
        name="shared_kv",
    )(x, g, wdc, wdr, lat_g, wk, wv, kn_g, kr_g, cos, sa, sb)


Q_SCALE = (MLA_NOPE + MLA_ROPE) ** -0.5 * math.log2(math.e)
Q_NOPE_COLS = HEADS * MLA_NOPE


def _q_kernel(x_ref, g_ref, wdq_ref, lora_g_ref, wuq_ref, qn_g_ref, qr_g_ref,
              cos_ref, sa_ref, sb_ref, q_out):
    h = _rms(x_ref[...], g_ref[...], D_MODEL).astype(BF16)
    cq = _rms(_dot(h, wdq_ref[...]), lora_g_ref[...], Q_LORA).astype(BF16)
    q = _dot(cq, wuq_ref[...])
    mean_nope = _group_mean_matrix(2 * MLA_NOPE, MLA_NOPE)
    mean_rope = _group_mean_matrix(2 * LANES, MLA_ROPE)
    qn_g = qn_g_ref[...] * Q_SCALE
    qr_g = qr_g_ref[...] * Q_SCALE
    cos = cos_ref[...]
    sa = sa_ref[...]
    sb = sb_ref[...]
    for quad in range(HEADS // 4):
        lo = Q_NOPE_COLS + quad * 2 * LANES
        qp = _group_rms(q[:, lo:lo + 2 * LANES], mean_rope, qr_g)
        for half in range(2):
            pair = 2 * quad + half
            lo = pair * 2 * MLA_NOPE
            qn = _group_rms(q[:, lo:lo + 2 * MLA_NOPE], mean_nope, qn_g).astype(BF16)
            tile = _rope64x2(qp[:, half * LANES:(half + 1) * LANES], cos, sa, sb).astype(BF16)
            for sub in range(2):
                hd = 2 * pair + sub
                q_out[:, hd * MLA_QK_PAD:hd * MLA_QK_PAD + MLA_NOPE] = (
                    qn[:, sub * MLA_NOPE:(sub + 1) * MLA_NOPE])
                q_out[:, hd * MLA_QK_PAD + MLA_NOPE:(hd + 1) * MLA_QK_PAD] = (
                    _rope_tile_for_head(tile, sub))


def _mla_q(x, g, wdq, lora_g, wuq, qn_g, qr_g, cos, sa, sb):
    m = x.shape[0]
    return pl.pallas_call(
        _q_kernel,
        out_shape=jax.ShapeDtypeStruct((m, HEADS * MLA_QK_PAD), BF16),
        grid=(m // TM_MLA,),
        in_specs=[_rows(D_MODEL, TM_MLA), _resident((1, D_MODEL)), _resident((D_MODEL, Q_LORA)),
                  _resident((1, Q_LORA)), _resident(wuq.shape),
                  _resident((1, 2 * MLA_NOPE)), _resident((1, 2 * LANES)),
                  _rows(LANES, TM_MLA), _rows(LANES, TM_MLA), _rows(LANES, TM_MLA)],
        out_specs=_rows(HEADS * MLA_QK_PAD, TM_MLA),
        compiler_params=_params("parallel"),
        name="mla_q",
    )(x, g, wdq, lora_g, wuq, qn_g, qr_g, cos, sa, sb)


def _flash_kernel(q_ref, k_ref, v_ref, o_ref, m_ref, acc_ref):
    tiles = q_ref.shape[0] // TQ
    i = pl.program_id(2)
    pair = (i, tiles - 1 - i)
    m_ref[...] = jnp.full_like(m_ref, NEG_BIG)
    acc_ref[...] = jnp.zeros_like(acc_ref)

    def item(k):
        if k < tiles - 1:
            second = k >= i
            return (jnp.where(second, 1, 0), jnp.where(second, pair[1], pair[0]),
                    jnp.where(second, k - i, k), False)
        which = k - (tiles - 1)
        return which, pair[which], pair[which], True

    def scores(k):
        _, qt, kt, _ = item(k)
        rows = pl.ds(pl.multiple_of(qt * TQ, TQ), TQ)
        keys = pl.ds(pl.multiple_of(kt * TQ, TQ), TQ)
        return [_dot_nt(q_ref[rows, hd * MLA_QK_PAD:(hd + 1) * MLA_QK_PAD],
                        k_ref[keys, hd * MLA_QK_PAD:(hd + 1) * MLA_QK_PAD])
                for hd in range(HEADS_PER_STEP)]

    def consume(k, s_heads):
        which, _, kt, diagonal = item(k)
        keys = pl.ds(pl.multiple_of(kt * TQ, TQ), TQ)
        for hd, s in enumerate(s_heads):
            if diagonal:
                row = lax.broadcasted_iota(jnp.int32, (TQ, TQ), 0)
                col = lax.broadcasted_iota(jnp.int32, (TQ, TQ), 1)
                s = jnp.where(col <= row, s, NEG_BIG)
            m_prev = m_ref[which, hd]
            m_new = jnp.maximum(m_prev, jnp.max(s, axis=1, keepdims=True))
            alpha = jnp.exp2(m_prev - m_new)
            p = jnp.exp2(s - _lane_tile(m_new, TQ // LANES))
            pv = _dot(p.astype(BF16), v_ref[keys, hd * V_PAD:(hd + 1) * V_PAD])
            acc_ref[which, hd] = _lane_tile(alpha, V_PAD // LANES) * acc_ref[which, hd] + pv
            m_ref[which, hd] = m_new

    n_items = tiles + 1
    pending = [scores(k) for k in range(SCORES_AHEAD)]
    for k in range(n_items):
        if k + SCORES_AHEAD < n_items:
            pending.append(scores(k + SCORES_AHEAD))
        consume(k, pending.pop(0))

    for which in range(2):
        rows = pl.ds(pl.multiple_of(pair[which] * TQ, TQ), TQ)
        for hd in range(HEADS_PER_STEP):
            acc = acc_ref[which, hd]
            o_ref[rows, hd * MLA_V:(hd + 1) * MLA_V] = (acc[:, :MLA_V] / acc[:, MLA_V:]).astype(BF16)


def _flash(q, k, v, batch, seq):
    m = q.shape[0]
    g = HEADS_PER_STEP
    tiles = seq // TQ
    assert tiles % 2 == 0, "query tiles are processed in (i, tiles-1-i) pairs"
    return pl.pallas_call(
        _flash_kernel,
        out_shape=jax.ShapeDtypeStruct((m, HEADS * MLA_V), BF16),
        grid=(batch, HEADS // g, tiles // 2),
        in_specs=[
            pl.BlockSpec((seq, g * MLA_QK_PAD), lambda b, h, i: (b, h)),
            pl.BlockSpec((seq, g * MLA_QK_PAD), lambda b, h, i: (b, h)),
            pl.BlockSpec((seq, g * V_PAD), lambda b, h, i: (b, h)),
        ],
        out_specs=pl.BlockSpec((seq, g * MLA_V), lambda b, h, i: (b, h)),
        scratch_shapes=[pltpu.VMEM((2, g, TQ, LANES), F32),
                        pltpu.VMEM((2, g, TQ, V_PAD), F32)],
        compiler_params=_params("parallel", "parallel", "arbitrary"),
        name="flash_attention",
    )(q, k, v)


def _row(g, repeat=1):
    return jnp.tile(g.reshape(1, -1).astype(F32), (1, repeat))


def kernel(x, positions, norm_g, ffn_w_gate, ffn_w_up, ffn_w_down, ret_w_in, ret_gn_g, ret_w_o,
           kv_norm_g, kv_w_down, kv_latent_norm_g, kv_w_up, k_nope_norm_g, k_rope_norm_g,
           mla_w_dq, mla_q_lora_norm_g, mla_w_uq, mla_q_nope_norm_g, mla_q_rope_norm_g, mla_w_o):
    batch, seq, d = x.shape
    depth = norm_g.shape[0]
    n_self = ret_w_in.shape[0]
    m = batch * seq
    x = x.reshape(m, d)
    rcos, rsin, mcos, msa, msb = _rope_tables(positions.reshape(m, 1))
    stacks = (ffn_w_gate, ffn_w_up, ffn_w_down.reshape(ffn_w_gate.shape))
    ffn_weights = [tuple(w[0, 0].astype(BF16) for w in stacks)]

    def ffn(x, layer, i, **mix):
        nxt = (layer, i + 1) if i == 0 else (layer + 1, 0)
        cast_next = (*stacks, *nxt) if nxt[0] < depth else None
        wg, wu, wd = ffn_weights.pop()
        x, cast = _ffn(x, _row(norm_g[layer, 2 * i]), wg, wu, wd.reshape(D_FF, D_MODEL),
                       cast_next=cast_next, **mix)
        ffn_weights.append(cast)
        return x

    k_shared = v_shared = None
    for layer in range(depth):
        x = ffn(x, layer, 0)
        g_mix = _row(norm_g[layer, 1])
        if layer < n_self:
            proj, kt = _ret_proj(x, g_mix, ret_w_in[layer].astype(BF16), rcos, rsin)
            mix = dict(ret_mix=(_retention(proj, kt, batch, seq), proj, _row(ret_gn_g[layer]),
                                ret_w_o[layer].astype(BF16)))
        else:
            j = layer - n_self
            wuq = mla_w_uq[j].reshape(Q_LORA, HEADS, MLA_NOPE + MLA_ROPE)
            wuq = jnp.concatenate([wuq[:, :, :MLA_NOPE].reshape(Q_LORA, Q_NOPE_COLS),
                                   wuq[:, :, MLA_NOPE:].reshape(Q_LORA, HEADS * MLA_ROPE)], axis=1)
            q = _mla_q(x, g_mix, mla_w_dq[j].astype(BF16), _row(mla_q_lora_norm_g[j]),
                       wuq.astype(BF16), _row(mla_q_nope_norm_g[j], 2), _row(mla_q_rope_norm_g[j], 4),
                       mcos, msa, msb)
            mix = dict(mix=(_flash(q, k_shared, v_shared, batch, seq), mla_w_o[j].astype(BF16)))
        x = ffn(x, layer, 1, **mix)
        if layer == n_self - 1:
            wup = kv_w_up.reshape(KV_LORA, HEADS, MLA_NOPE + MLA_V)
            wk = wup[:, :, :MLA_NOPE].reshape(KV_LORA, HEADS * MLA_NOPE).astype(BF16)
            wv = wup[:, :, MLA_NOPE:].reshape(KV_LORA, HEADS * MLA_V).astype(BF16)
            wdr = jnp.tile(kv_w_down[:, KV_LORA:], (1, 2)).astype(BF16)
            k_shared, v_shared = _shared_kv(
                x, _row(kv_norm_g), kv_w_down[:, :KV_LORA].astype(BF16), wdr,
                _row(kv_latent_norm_g), wk, wv, _row(k_nope_norm_g, 2), _row(k_rope_norm_g, 2),
                mcos, msa, msb)
    return x.reshape(batch, seq, d)
```

```python
import math

import jax
import jax.numpy as jnp
from jax import lax
from jax.experimental import pallas as pl
from jax.experimental.pallas import tpu as pltpu

F32 = jnp.float32
BF16 = jnp.bfloat16

LANES = 128

D_MODEL = 1024
D_FF = 2816
HEADS = 8
RET_DK = 128
RET_DV = 256
RET_QK = HEADS * RET_DK
RET_PROJ = 2 * RET_QK + 2 * HEADS * RET_DV
RET_V0 = HEADS * RET_DV
RET_Q0 = 2 * HEADS * RET_DV
MLA_NOPE = 128
MLA_ROPE = 64
MLA_V = 128
MLA_QK_PAD = 2 * LANES
V_PAD = 2 * LANES
HEADS_PER_STEP = 2
Q_LORA = 384
KV_LORA = 256
ROPE_BASE = 10000.0
EPS = 1e-6

VMEM_LIMIT = 56 * 1024 * 1024

TM = 512
TM_FFN = 1024
TM_MLA = 1024
FF_CHUNK = 256
RET_CHUNK = 256
RET_HEADS_PER_STEP = 2
TQ = 512
SCORES_AHEAD = 2
NEG_BIG = -1e30


def _dot(a, b):
    return jnp.dot(a, b, preferred_element_type=F32)


def _dot_nt(a, b):
    return lax.dot_general(a, b, (((1,), (1,)), ((), ())), preferred_element_type=F32)


def _dot_tn(a, b):
    return lax.dot_general(a, b, (((0,), (0,)), ((), ())), preferred_element_type=F32)


def _rms(x, g, n):
    ms = jnp.sum(x * x, axis=-1, keepdims=True) * (1.0 / n)
    return x * lax.rsqrt(ms + EPS) * g


def _silu(x):
    return x / (1.0 + jnp.exp2(x * -math.log2(math.e)))


def _lane_tile(x, n):
    return jnp.concatenate([x] * n, axis=1)


def _group_mean_matrix(width, group):
    shift = group.bit_length() - 1
    r = lax.broadcasted_iota(jnp.int32, (width, width), 0) >> shift
    c = lax.broadcasted_iota(jnp.int32, (width, width), 1) >> shift
    return jnp.where(r == c, 1.0 / group, 0.0).astype(BF16)


def _group_rms(x, mean_matrix, g):
    ms = _dot((x * x).astype(BF16), mean_matrix)
    return x * lax.rsqrt(ms + EPS) * g


def _params(*sem):
    return pltpu.CompilerParams(dimension_semantics=sem, vmem_limit_bytes=VMEM_LIMIT)


def _resident(shape):
    nd = len(shape)
    return pl.BlockSpec(shape, lambda *_: (0,) * nd, pipeline_mode=pl.Buffered(1))


def _rows(width, tm=TM):
    return pl.BlockSpec((tm, width), lambda i: (i, 0))


def _rope_tables_kernel(pos_ref, rcos_ref, rsin_ref, mcos_ref, msa_ref, msb_ref):
    pos = pos_ref[...].astype(F32)
    lane = lax.broadcasted_iota(jnp.int32, (1, LANES), 1)
    is_ret = lane < 64
    freq = jnp.where(is_ret, lane, lane & 31).astype(F32)
    step = jnp.where(is_ret, -2.0 / RET_DK, -2.0 / MLA_ROPE) * math.log(ROPE_BASE)
    ang = pos * jnp.exp(freq * step)
    c = jnp.cos(ang)
    s = jnp.sin(ang)
    rcos_ref[...] = jnp.where(is_ret, c, pltpu.roll(c, 64, 1))
    rsin_ref[...] = jnp.where(is_ret, -s, pltpu.roll(s, 64, 1))

    def spread(t):
        quarter = lane >> 5
        return jnp.where(quarter == 0, pltpu.roll(t, 64, 1),
                         jnp.where(quarter == 1, pltpu.roll(t, 96, 1),
                                   jnp.where(quarter == 2, t, pltpu.roll(t, 32, 1))))

    s = spread(s)
    first_half = (lane & 63) < 32
    mcos_ref[...] = spread(c)
    msa_ref[...] = jnp.where(first_half, -s, 0.0)
    msb_ref[...] = jnp.where(first_half, 0.0, s)


def _rope_tables(pos):
    m = pos.shape[0]
    tm = 1024
    tab = jax.ShapeDtypeStruct((m, LANES), F32)
    spec = pl.BlockSpec((tm, LANES), lambda i: (i, 0))
    return pl.pallas_call(
        _rope_tables_kernel,
        out_shape=(tab,) * 5,
        grid=(m // tm,),
        in_specs=[pl.BlockSpec((tm, 1), lambda i: (i, 0))],
        out_specs=(spec,) * 5,
        compiler_params=_params("parallel"),
        name="rope_tables",
    )(pos)


def _rope128(x, cos, sin):
    return x * cos + pltpu.roll(x, 64, 1) * sin


def _rope64x2(x, cos, sa, sb):
    return x * cos + pltpu.roll(x, 96, 1) * sa + pltpu.roll(x, 32, 1) * sb


def _ffn_body(x, g_ref, wg_ref, wu_ref, wd_ref, o_ref):
    h = _rms(x, g_ref[...], D_MODEL).astype(BF16)
    acc = None
    for c in range(D_FF // FF_CHUNK):
        sl = slice(c * FF_CHUNK, (c + 1) * FF_CHUNK)
        a = (_silu(_dot(h, wg_ref[:, sl])) * _dot(h, wu_ref[:, sl])).astype(BF16)
        d = _dot(a, wd_ref[sl, :])
        acc = d if acc is None else acc + d
    o_ref[...] = x + 0.5 * acc


def _ffn_kernel(x_ref, g_ref, wg_ref, wu_ref, wd_ref, o_ref):
    _ffn_body(x_ref[...], g_ref, wg_ref, wu_ref, wd_ref, o_ref)


def _mix_ffn_kernel(x_ref, a_ref, wo_ref, g_ref, wg_ref, wu_ref, wd_ref, o_ref):
    _ffn_body(x_ref[...] + _dot(a_ref[...], wo_ref[...]), g_ref, wg_ref, wu_ref, wd_ref, o_ref)


def _ret_mix_ffn_kernel(x_ref, ret_ref, gate_ref, gn_ref, wo_ref, g_ref, wg_ref, wu_ref, wd_ref,
                        o_ref):
    x = x_ref[...]
    for hd in range(HEADS):
        cols = slice(hd * RET_DV, (hd + 1) * RET_DV)
        o = ret_ref[:, cols].astype(F32)
        oc = o - jnp.mean(o, axis=-1, keepdims=True)
        var = jnp.mean(oc * oc, axis=-1, keepdims=True)
        on = oc * lax.rsqrt(var + EPS) * gn_ref[:, cols]
        a = (_silu(gate_ref[:, cols].astype(F32)) * on).astype(BF16)
        x = x + _dot(a, wo_ref[cols, :])
    _ffn_body(x, g_ref, wg_ref, wu_ref, wd_ref, o_ref)


def _casting_next_weights(body, n_in):
    def kernel(*refs):
        for src, dst in zip(refs[n_in:n_in + 3], refs[n_in + 4:]):
            dst[...] = src[...].astype(BF16)
        body(*refs[:n_in], refs[n_in + 3])
    return kernel


def _ffn(x, g, wg, wu, wd, mix=None, ret_mix=None, cast_next=None):
    m = x.shape[0]
    w_specs = [_resident((1, D_MODEL)), _resident((D_MODEL, D_FF)), _resident((D_MODEL, D_FF)),
               _resident((D_FF, D_MODEL))]
    if ret_mix is not None:
        o, proj, gn, wo = ret_mix
        tm = TM
        name, body, args = "ret_mix_ffn", _ret_mix_ffn_kernel, (x, o, proj, gn, wo)
        specs = [_rows(D_MODEL, tm), _rows(HEADS * RET_DV, tm), _rows(HEADS * RET_DV, tm),
                 _resident(gn.shape), _resident(wo.shape)]
    elif mix is not None:
        a, wo = mix
        tm = TM_FFN
        name, body, args = "mix_ffn", _mix_ffn_kernel, (x, a, wo)
        specs = [_rows(D_MODEL, tm), _rows(a.shape[1], tm), _resident(wo.shape)]
    else:
        tm = TM_FFN
        name, body, args, specs = "ffn", _ffn_kernel, (x,), [_rows(D_MODEL, tm)]
    steps = m // tm
    args = (*args, g, wg, wu, wd)
    in_specs = specs + w_specs
    out_shape = jax.ShapeDtypeStruct((m, D_MODEL), F32)
    out_specs = _rows(D_MODEL, tm)
    if cast_next is not None:
        *stacks, layer, half = cast_next
        body = _casting_next_weights(body, len(args))
        args = (*args, *stacks)
        slab = D_MODEL // steps
        last = D_FF // FF_CHUNK - 1
        assert steps > last
        in_specs = in_specs + [
            pl.BlockSpec((None, None, slab, D_FF), lambda i: (layer, half, i, 0)),
            pl.BlockSpec((None, None, slab, D_FF), lambda i: (layer, half, i, 0)),
            pl.BlockSpec((None, None, FF_CHUNK, D_MODEL),
                         lambda i: (layer, half, jnp.minimum(i, last), 0))]
        out_shape = (out_shape, jax.ShapeDtypeStruct((D_MODEL, D_FF), BF16),
                     jax.ShapeDtypeStruct((D_MODEL, D_FF), BF16),
                     jax.ShapeDtypeStruct((D_FF, D_MODEL), BF16))
        out_specs = (out_specs, pl.BlockSpec((slab, D_FF), lambda i: (i, 0)),
                     pl.BlockSpec((slab, D_FF), lambda i: (i, 0)),
                     pl.BlockSpec((FF_CHUNK, D_MODEL), lambda i: (jnp.minimum(i, last), 0)))
    out = pl.pallas_call(
        body,
        out_shape=out_shape,
        grid=(steps,),
        in_specs=in_specs,
        out_specs=out_specs,
        compiler_params=_params("arbitrary"),
        name=name,
    )(*args)
    if cast_next is None:
        return out, None
    return out[0], out[1:]


def _ret_proj_kernel(x_ref, g_ref, w_ref, cos_ref, sin_ref, o_ref, kt_ref):
    h = _rms(x_ref[...], g_ref[...], D_MODEL).astype(BF16)
    cos = cos_ref[...]
    sin = sin_ref[...]
    q = _dot(h, w_ref[:, :RET_QK])
    k = _dot(h, w_ref[:, RET_QK:2 * RET_QK])
    for hd in range(HEADS):
        head = slice(hd * RET_DK, (hd + 1) * RET_DK)
        o_ref[:, RET_Q0 + hd * RET_DK:RET_Q0 + (hd + 1) * RET_DK] = (
            _rope128(q[:, head], cos, sin).astype(BF16))
        kh = _rope128(k[:, head], cos, sin) * RET_DK ** -0.5
        for c in range(TM // RET_CHUNK):
            kt_ref[c, head, :] = kh[c * RET_CHUNK:(c + 1) * RET_CHUNK, :].T.astype(BF16)
    v0 = 2 * RET_QK
    g0 = v0 + HEADS * RET_DV
    o_ref[:, RET_V0:RET_Q0] = _dot(h, w_ref[:, v0:g0]).astype(BF16)
    o_ref[:, :RET_V0] = _dot(h, w_ref[:, g0:]).astype(BF16)


def _ret_proj(x, g, w, cos, sin):
    m = x.shape[0]
    chunks = TM // RET_CHUNK
    return pl.pallas_call(
        _ret_proj_kernel,
        out_shape=(jax.ShapeDtypeStruct((m, RET_PROJ - RET_QK), BF16),
                   jax.ShapeDtypeStruct((m // RET_CHUNK, RET_QK, RET_CHUNK), BF16)),
        grid=(m // TM,),
        in_specs=[_rows(D_MODEL), _resident((1, D_MODEL)), _resident((D_MODEL, RET_PROJ)),
                  _rows(LANES), _rows(LANES)],
        out_specs=(_rows(RET_PROJ - RET_QK),
                   pl.BlockSpec((chunks, RET_QK, RET_CHUNK), lambda i: (i, 0, 0))),
        compiler_params=_params("parallel"),
        name="ret_proj",
    )(x, g, w, cos, sin)


def _retention_kernel(q_ref, kt_ref, v_ref, o_ref, state_ref):
    c = RET_CHUNK
    seq = q_ref.shape[0]

    def index(shape, axis):
        return lax.broadcasted_iota(jnp.int32, shape, axis).astype(F32)

    def decays(sub):
        hd = (pl.program_id(1) * RET_HEADS_PER_STEP + sub).astype(F32)

        def log_decay(shape):
            return jnp.log1p(-jnp.exp2(jnp.full(shape, -5.0, F32) - hd))

        diff = index((c, c), 0) - index((c, c), 1)
        d_intra = jnp.where(diff >= 0, jnp.exp(log_decay((c, c)) * jnp.maximum(diff, 0.0)), 0.0)
        q_decay = jnp.exp(log_decay((c, RET_DV)) * (index((c, RET_DV), 0) + 1.0))
        k_decay = jnp.exp(log_decay((RET_DK, c)) * (c - 1.0 - index((RET_DK, c), 1)))
        chunk_decay = jnp.exp(log_decay((RET_DK, RET_DV)) * c)
        return d_intra, q_decay, k_decay, chunk_decay

    per_head = [decays(sub) for sub in range(RET_HEADS_PER_STEP)]
    state_ref[...] = jnp.zeros_like(state_ref)

    def body(t, carry):
        off = pl.multiple_of(t * c, c)
        for sub, (d_intra, q_decay, k_decay, chunk_decay) in enumerate(per_head):
            qk = slice(sub * RET_DK, (sub + 1) * RET_DK)
            vo = slice(sub * RET_DV, (sub + 1) * RET_DV)
            q = q_ref[pl.ds(off, c), qk]
            kt = kt_ref[t, qk, :]
            v = v_ref[pl.ds(off, c), vo]
            state = state_ref[sub]
            s = _dot(q, kt) * d_intra
            o = _dot(s.astype(BF16), v) + _dot(q, state.astype(BF16)) * q_decay
            ktd = (kt.astype(F32) * k_decay).astype(BF16)
            state_ref[sub] = state * chunk_decay + _dot(ktd, v)
            o_ref[pl.ds(off, c), vo] = o.astype(BF16)
        return carry

    lax.fori_loop(0, seq // c, body, 0, unroll=4)


def _retention(proj, kt, batch, seq):
    m = proj.shape[0]
    g = RET_HEADS_PER_STEP
    qblk = RET_Q0 // (g * RET_DK)
    vblk = RET_V0 // (g * RET_DV)
    return pl.pallas_call(
        _retention_kernel,
        out_shape=jax.ShapeDtypeStruct((m, HEADS * RET_DV), BF16),
        grid=(batch, HEADS // g),
        in_specs=[
            pl.BlockSpec((seq, g * RET_DK), lambda b, h: (b, qblk + h)),
            pl.BlockSpec((seq // RET_CHUNK, g * RET_DK, RET_CHUNK), lambda b, h: (b, h, 0)),
            pl.BlockSpec((seq, g * RET_DV), lambda b, h: (b, vblk + h)),
        ],
        out_specs=pl.BlockSpec((seq, g * RET_DV), lambda b, h: (b, h)),
        scratch_shapes=[pltpu.VMEM((g, RET_DK, RET_DV), F32)],
        compiler_params=_params("parallel", "parallel"),
        name="retention",
    )(proj, kt, proj)


def _rope_tile_for_head(tile, hd):
    lane = lax.broadcasted_iota(jnp.int32, tile.shape, 1)
    keep = (lane < MLA_ROPE) if hd % 2 == 0 else (lane >= MLA_ROPE)
    return jnp.where(keep, tile, jnp.zeros_like(tile))


def _kv_kernel(x_ref, g_ref, wdc_ref, wdr_ref, lat_g_ref, wk_ref, wv_ref, kn_g_ref, kr_g_ref,
               cos_ref, sa_ref, sb_ref, k_out, v_out):
    h = _rms(x_ref[...], g_ref[...], D_MODEL).astype(BF16)
    lat = _rms(_dot(h, wdc_ref[...]), lat_g_ref[...], KV_LORA).astype(BF16)
    pe = _rms(_dot(h, wdr_ref[...]), kr_g_ref[...], LANES)
    pe = _rope64x2(pe, cos_ref[...], sa_ref[...], sb_ref[...]).astype(BF16)
    pe_tiles = (_rope_tile_for_head(pe, 0), _rope_tile_for_head(pe, 1))
    ones = jnp.ones((pe.shape[0], V_PAD - MLA_V), BF16)
    mean_nope = _group_mean_matrix(2 * MLA_NOPE, MLA_NOPE)
    kn_g = kn_g_ref[...]
    for pair in range(HEADS // 2):
        lo = pair * 2 * MLA_NOPE
        kn = _group_rms(_dot(lat, wk_ref[:, lo:lo + 2 * MLA_NOPE]), mean_nope, kn_g)
        kn = kn.astype(BF16)
        vv = _dot(lat, wv_ref[:, pair * 2 * MLA_V:(pair + 1) * 2 * MLA_V]).astype(BF16)
        for sub in range(2):
            hd = 2 * pair + sub
            k_out[:, hd * MLA_QK_PAD:hd * MLA_QK_PAD + MLA_NOPE] = (
                kn[:, sub * MLA_NOPE:(sub + 1) * MLA_NOPE])
            k_out[:, hd * MLA_QK_PAD + MLA_NOPE:(hd + 1) * MLA_QK_PAD] = pe_tiles[sub]
            v_out[:, hd * V_PAD:hd * V_PAD + MLA_V] = vv[:, sub * MLA_V:(sub + 1) * MLA_V]
            v_out[:, hd * V_PAD + MLA_V:(hd + 1) * V_PAD] = ones


def _shared_kv(x, g, wdc, wdr, lat_g, wk, wv, kn_g, kr_g, cos, sa, sb):
    m = x.shape[0]
    return pl.pallas_call(
        _kv_kernel,
        out_shape=(jax.ShapeDtypeStruct((m, HEADS * MLA_QK_PAD), BF16),
                   jax.ShapeDtypeStruct((m, HEADS * V_PAD), BF16)),
        grid=(m // TM_MLA,),
        in_specs=[_rows(D_MODEL, TM_MLA), _resident((1, D_MODEL)), _resident((D_MODEL, KV_LORA)),
                  _resident((D_MODEL, LANES)), _resident((1, KV_LORA)),
                  _resident((KV_LORA, HEADS * MLA_NOPE)), _resident((KV_LORA, HEADS * MLA_V)),
                  _resident((1, 2 * MLA_NOPE)), _resident((1, LANES)),
                  _rows(LANES, TM_MLA), _rows(LANES, TM_MLA), _rows(LANES, TM_MLA)],
        out_specs=(_rows(HEADS * MLA_QK_PAD, TM_MLA), _rows(HEADS * V_PAD, TM_MLA)),
        compiler_params=_params("parallel"),
        name="shared_kv",
    )(x, g, wdc, wdr, lat_g, wk, wv, kn_g, kr_g, cos, sa, sb)


Q_SCALE = (MLA_NOPE + MLA_ROPE) ** -0.5 * math.log2(math.e)
Q_NOPE_COLS = HEADS * MLA_NOPE


def _q_kernel(x_ref, g_ref, wdq_ref, lora_g_ref, wuq_ref, qn_g_ref, qr_g_ref,
              cos_ref, sa_ref, sb_ref, q_out):
    h = _rms(x_ref[...], g_ref[...], D_MODEL).astype(BF16)
    cq = _rms(_dot(h, wdq_ref[...]), lora_g_ref[...], Q_LORA).astype(BF16)
    q = _dot(cq, wuq_ref[...])
    mean_nope = _group_mean_matrix(2 * MLA_NOPE, MLA_NOPE)
    mean_rope = _group_mean_matrix(2 * LANES, MLA_ROPE)
    qn_g = qn_g_ref[...] * Q_SCALE
    qr_g = qr_g_ref[...] * Q_SCALE
    cos = cos_ref[...]
    sa = sa_ref[...]
    sb = sb_ref[...]
    for quad in range(HEADS // 4):
        lo = Q_NOPE_COLS + quad * 2 * LANES
        qp = _group_rms(q[:, lo:lo + 2 * LANES], mean_rope, qr_g)
        for half in range(2):
            pair = 2 * quad + half
            lo = pair * 2 * MLA_NOPE
            qn = _group_rms(q[:, lo:lo + 2 * MLA_NOPE], mean_nope, qn_g).astype(BF16)
            tile = _rope64x2(qp[:, half * LANES:(half + 1) * LANES], cos, sa, sb).astype(BF16)
            for sub in range(2):
                hd = 2 * pair + sub
                q_out[:, hd * MLA_QK_PAD:hd * MLA_QK_PAD + MLA_NOPE] = (
                    qn[:, sub * MLA_NOPE:(sub + 1) * MLA_NOPE])
                q_out[:, hd * MLA_QK_PAD + MLA_NOPE:(hd + 1) * MLA_QK_PAD] = (
                    _rope_tile_for_head(tile, sub))


def _mla_q(x, g, wdq, lora_g, wuq, qn_g, qr_g, cos, sa, sb):
    m = x.shape[0]
    return pl.pallas_call(
        _q_kernel,
        out_shape=jax.ShapeDtypeStruct((m, HEADS * MLA_QK_PAD), BF16),
        grid=(m // TM_MLA,),
        in_specs=[_rows(D_MODEL, TM_MLA), _resident((1, D_MODEL)), _resident((D_MODEL, Q_LORA)),
                  _resident((1, Q_LORA)), _resident(wuq.shape),
                  _resident((1, 2 * MLA_NOPE)), _resident((1, 2 * LANES)),
                  _rows(LANES, TM_MLA), _rows(LANES, TM_MLA), _rows(LANES, TM_MLA)],
        out_specs=_rows(HEADS * MLA_QK_PAD, TM_MLA),
        compiler_params=_params("parallel"),
        name="mla_q",
    )(x, g, wdq, lora_g, wuq, qn_g, qr_g, cos, sa, sb)


def _flash_kernel(q_ref, k_ref, v_ref, o_ref, m_ref, acc_ref):
    tiles = q_ref.shape[0] // TQ
    i = pl.program_id(2)
    pair = (i, tiles - 1 - i)
    m_ref[...] = jnp.full_like(m_ref, NEG_BIG)
    acc_ref[...] = jnp.zeros_like(acc_ref)

    def item(k):
        if k < tiles - 1:
            second = k >= i
            return (jnp.where(second, 1, 0), jnp.where(second, pair[1], pair[0]),
                    jnp.where(second, k - i, k), False)
        which = k - (tiles - 1)
        return which, pair[which], pair[which], True

    def scores(k):
        _, qt, kt, _ = item(k)
        rows = pl.ds(pl.multiple_of(qt * TQ, TQ), TQ)
        keys = pl.ds(pl.multiple_of(kt * TQ, TQ), TQ)
        return [_dot_nt(q_ref[rows, hd * MLA_QK_PAD:(hd + 1) * MLA_QK_PAD],
                        k_ref[keys, hd * MLA_QK_PAD:(hd + 1) * MLA_QK_PAD])
                for hd in range(HEADS_PER_STEP)]

    def consume(k, s_heads):
        which, _, kt, diagonal = item(k)
        keys = pl.ds(pl.multiple_of(kt * TQ, TQ), TQ)
        for hd, s in enumerate(s_heads):
            if diagonal:
                row = lax.broadcasted_iota(jnp.int32, (TQ, TQ), 0)
                col = lax.broadcasted_iota(jnp.int32, (TQ, TQ), 1)
                s = jnp.where(col <= row, s, NEG_BIG)
            m_prev = m_ref[which, hd]
            m_new = jnp.maximum(m_prev, jnp.max(s, axis=1, keepdims=True))
            alpha = jnp.exp2(m_prev - m_new)
            p = jnp.exp2(s - _lane_tile(m_new, TQ // LANES))
            pv = _dot(p.astype(BF16), v_ref[keys, hd * V_PAD:(hd + 1) * V_PAD])
            acc_ref[which, hd] = _lane_tile(alpha, V_PAD // LANES) * acc_ref[which, hd] + pv
            m_ref[which, hd] = m_new

    n_items = tiles + 1
    pending = [scores(k) for k in range(SCORES_AHEAD)]
    for k in range(n_items):
        if k + SCORES_AHEAD < n_items:
            pending.append(scores(k + SCORES_AHEAD))
        consume(k, pending.pop(0))

    for which in range(2):
        rows = pl.ds(pl.multiple_of(pair[which] * TQ, TQ), TQ)
        for hd in range(HEADS_PER_STEP):
            acc = acc_ref[which, hd]
            o_ref[rows, hd * MLA_V:(hd + 1) * MLA_V] = (acc[:, :MLA_V] / acc[:, MLA_V:]).astype(BF16)


def _flash(q, k, v, batch, seq):
    m = q.shape[0]
    g = HEADS_PER_STEP
    tiles = seq // TQ
    assert tiles % 2 == 0, "query tiles are processed in (i, tiles-1-i) pairs"
    return pl.pallas_call(
        _flash_kernel,
        out_shape=jax.ShapeDtypeStruct((m, HEADS * MLA_V), BF16),
        grid=(batch, HEADS // g, tiles // 2),
        in_specs=[
            pl.BlockSpec((seq, g * MLA_QK_PAD), lambda b, h, i: (b, h)),
            pl.BlockSpec((seq, g * MLA_QK_PAD), lambda b, h, i: (b, h)),
            pl.BlockSpec((seq, g * V_PAD), lambda b, h, i: (b, h)),
        ],
        out_specs=pl.BlockSpec((seq, g * MLA_V), lambda b, h, i: (b, h)),
        scratch_shapes=[pltpu.VMEM((2, g, TQ, LANES), F32),
                        pltpu.VMEM((2, g, TQ, V_PAD), F32)],
        compiler_params=_params("parallel", "parallel", "arbitrary"),
        name="flash_attention",
    )(q, k, v)


def _row(g, repeat=1):
    return jnp.tile(g.reshape(1, -1).astype(F32), (1, repeat))


def kernel(x, positions, norm_g, ffn_w_gate, ffn_w_up, ffn_w_down, ret_w_in, ret_gn_g, ret_w_o,
           kv_norm_g, kv_w_down, kv_latent_norm_g, kv_w_up, k_nope_norm_g, k_rope_norm_g,
           mla_w_dq, mla_q_lora_norm_g, mla_w_uq, mla_q_nope_norm_g, mla_q_rope_norm_g, mla_w_o):
    batch, seq, d = x.shape
    depth = norm_g.shape[0]
    n_self = ret_w_in.shape[0]
    m = batch * seq
    x = x.reshape(m, d)
    rcos, rsin, mcos, msa, msb = _rope_tables(positions.reshape(m, 1))
    stacks = (ffn_w_gate, ffn_w_up, ffn_w_down)
    ffn_weights = [tuple(w[0, 0].astype(BF16) for w in stacks)]

    def ffn(x, layer, i, **mix):
        nxt = (layer, i + 1) if i == 0 else (layer + 1, 0)
        cast_next = (*stacks, *nxt) if nxt[0] < depth else None
        wg, wu, wd = ffn_weights.pop()
        x, cast = _ffn(x, _row(norm_g[layer, 2 * i]), wg, wu, wd, cast_next=cast_next, **mix)
        ffn_weights.append(cast)
        return x

    k_shared = v_shared = None
    for layer in range(depth):
        x = ffn(x, layer, 0)
        g_mix = _row(norm_g[layer, 1])
        if layer < n_self:
            proj, kt = _ret_proj(x, g_mix, ret_w_in[layer].astype(BF16), rcos, rsin)
            mix = dict(ret_mix=(_retention(proj, kt, batch, seq), proj, _row(ret_gn_g[layer]),
                                ret_w_o[layer].astype(BF16)))
        else:
            j = layer - n_self
            wuq = mla_w_uq[j].reshape(Q_LORA, HEADS, MLA_NOPE + MLA_ROPE)
            wuq = jnp.concatenate([wuq[:, :, :MLA_NOPE].reshape(Q_LORA, Q_NOPE_COLS),
                                   wuq[:, :, MLA_NOPE:].reshape(Q_LORA, HEADS * MLA_ROPE)], axis=1)
            q = _mla_q(x, g_mix, mla_w_dq[j].astype(BF16), _row(mla_q_lora_norm_g[j]),
                       wuq.astype(BF16), _row(mla_q_nope_norm_g[j], 2), _row(mla_q_rope_norm_g[j], 4),
                       mcos, msa, msb)
            mix = dict(mix=(_flash(q, k_shared, v_shared, batch, seq), mla_w_o[j].astype(BF16)))
        x = ffn(x, layer, 1, **mix)
        if layer == n_self - 1:
            wup = kv_w_up.reshape(KV_LORA, HEADS, MLA_NOPE + MLA_V)
            wk = wup[:, :, :MLA_NOPE].reshape(KV_LORA, HEADS * MLA_NOPE).astype(BF16)
            wv = wup[:, :, MLA_NOPE:].reshape(KV_LORA, HEADS * MLA_V).astype(BF16)
            wdr = jnp.tile(kv_w_down[:, KV_LORA:], (1, 2)).astype(BF16)
            k_shared, v_shared = _shared_kv(
                x, _row(kv_norm_g), kv_w_down[:, :KV_LORA].astype(BF16), wdr,
                _row(kv_latent_norm_g), wk, wv, _row(k_nope_norm_g, 2), _row(k_rope_norm_g, 2),
                mcos, msa, msb)
    return x.reshape(batch, seq, d)
```

```python
import math

import jax
import jax.numpy as jnp
from jax import lax
from jax.experimental import pallas as pl
from jax.experimental.pallas import tpu as pltpu

F32 = jnp.float32
BF16 = jnp.bfloat16

LANES = 128

D_MODEL = 1024
D_FF = 2816
HEADS = 8
RET_DK = 128
RET_DV = 256
RET_QK = HEADS * RET_DK
RET_PROJ = 2 * RET_QK + 2 * HEADS * RET_DV
RET_V0 = HEADS * RET_DV
RET_Q0 = 2 * HEADS * RET_DV
MLA_NOPE = 128
MLA_ROPE = 64
MLA_V = 128
MLA_QK_PAD = 2 * LANES
V_PAD = 2 * LANES
HEADS_PER_STEP = 2
Q_LORA = 384
KV_LORA = 256
ROPE_BASE = 10000.0
EPS = 1e-6

VMEM_LIMIT = 56 * 1024 * 1024

TM = 512
TM_FFN = 1024
TM_MLA = 1024
FF_CHUNK = 256
RET_CHUNK = 256
RET_HEADS_PER_STEP = 2
TQ = 512
SCORES_AHEAD = 2
NEG_BIG = -1e30


def _dot(a, b):
    return jnp.dot(a, b, preferred_element_type=F32)


def _dot_nt(a, b):
    return lax.dot_general(a, b, (((1,), (1,)), ((), ())), preferred_element_type=F32)


def _rms(x, g, n):
    ms = jnp.sum(x * x, axis=-1, keepdims=True) * (1.0 / n)
    return x * lax.rsqrt(ms + EPS) * g


def _silu(x):
    return x / (1.0 + jnp.exp2(x * -math.log2(math.e)))


def _lane_tile(x, n):
    return jnp.concatenate([x] * n, axis=1)


def _group_mean_matrix(width, group):
    shift = group.bit_length() - 1
    r = lax.broadcasted_iota(jnp.int32, (width, width), 0) >> shift
    c = lax.broadcasted_iota(jnp.int32, (width, width), 1) >> shift
    return jnp.where(r == c, 1.0 / group, 0.0).astype(BF16)


def _group_rms(x, mean_matrix, g):
    ms = _dot((x * x).astype(BF16), mean_matrix)
    return x * lax.rsqrt(ms + EPS) * g


def _params(*sem):
    return pltpu.CompilerParams(dimension_semantics=sem, vmem_limit_bytes=VMEM_LIMIT)


def _resident(shape):
    nd = len(shape)
    return pl.BlockSpec(shape, lambda *_: (0,) * nd, pipeline_mode=pl.Buffered(1))


def _rows(width, tm=TM):
    return pl.BlockSpec((tm, width), lambda i: (i, 0))


def _rope_tables_kernel(pos_ref, rcos_ref, rsin_ref, mcos_ref, msa_ref, msb_ref):
    pos = pos_ref[...].astype(F32)
    lane = lax.broadcasted_iota(jnp.int32, (1, LANES), 1)
    is_ret = lane < 64
    freq = jnp.where(is_ret, lane, lane & 31).astype(F32)
    step = jnp.where(is_ret, -2.0 / RET_DK, -2.0 / MLA_ROPE) * math.log(ROPE_BASE)
    ang = pos * jnp.exp(freq * step)
    c = jnp.cos(ang)
    s = jnp.sin(ang)
    rcos_ref[...] = jnp.where(is_ret, c, pltpu.roll(c, 64, 1))
    rsin_ref[...] = jnp.where(is_ret, -s, pltpu.roll(s, 64, 1))

    def spread(t):
        quarter = lane >> 5
        return jnp.where(quarter == 0, pltpu.roll(t, 64, 1),
                         jnp.where(quarter == 1, pltpu.roll(t, 96, 1),
                                   jnp.where(quarter == 2, t, pltpu.roll(t, 32, 1))))

    s = spread(s)
    first_half = (lane & 63) < 32
    mcos_ref[...] = spread(c)
    msa_ref[...] = jnp.where(first_half, -s, 0.0)
    msb_ref[...] = jnp.where(first_half, 0.0, s)


def _rope_tables(pos):
    m = pos.shape[0]
    tm = 1024
    tab = jax.ShapeDtypeStruct((m, LANES), F32)
    spec = pl.BlockSpec((tm, LANES), lambda i: (i, 0))
    return pl.pallas_call(
        _rope_tables_kernel,
        out_shape=(tab,) * 5,
        grid=(m // tm,),
        in_specs=[pl.BlockSpec((tm, 1), lambda i: (i, 0))],
        out_specs=(spec,) * 5,
        compiler_params=_params("parallel"),
        name="rope_tables",
    )(pos)


def _rope128(x, cos, sin):
    return x * cos + pltpu.roll(x, 64, 1) * sin


def _rope64x2(x, cos, sa, sb):
    return x * cos + pltpu.roll(x, 96, 1) * sa + pltpu.roll(x, 32, 1) * sb


def _ffn_body(x, g_ref, wg_ref, wu_ref, wd_ref, o_ref):
    h = _rms(x, g_ref[...], D_MODEL).astype(BF16)
    acc = None
    for c in range(D_FF // FF_CHUNK):
        sl = slice(c * FF_CHUNK, (c + 1) * FF_CHUNK)
        a = (_silu(_dot(h, wg_ref[:, sl])) * _dot(h, wu_ref[:, sl])).astype(BF16)
        d = _dot(a, wd_ref[sl, :])
        acc = d if acc is None else acc + d
    o_ref[...] = x + 0.5 * acc


def _ffn_kernel(x_ref, g_ref, wg_ref, wu_ref, wd_ref, o_ref):
    _ffn_body(x_ref[...], g_ref, wg_ref, wu_ref, wd_ref, o_ref)


def _mix_ffn_kernel(x_ref, a_ref, wo_ref, g_ref, wg_ref, wu_ref, wd_ref, o_ref):
    _ffn_body(x_ref[...] + _dot(a_ref[...], wo_ref[...]), g_ref, wg_ref, wu_ref, wd_ref, o_ref)


def _ret_mix_ffn_kernel(x_ref, ret_ref, gate_ref, gn_ref, wo_ref, g_ref, wg_ref, wu_ref, wd_ref,
                        o_ref):
    x = x_ref[...]
    for hd in range(HEADS):
        cols = slice(hd * RET_DV, (hd + 1) * RET_DV)
        o = ret_ref[:, cols].astype(F32)
        oc = o - jnp.mean(o, axis=-1, keepdims=True)
        var = jnp.mean(oc * oc, axis=-1, keepdims=True)
        on = oc * lax.rsqrt(var + EPS) * gn_ref[:, cols]
        a = (_silu(gate_ref[:, cols].astype(F32)) * on).astype(BF16)
        x = x + _dot(a, wo_ref[cols, :])
    _ffn_body(x, g_ref, wg_ref, wu_ref, wd_ref, o_ref)


def _casting_next_weights(body, n_in):
    def kernel(*refs):
        for src, dst in zip(refs[n_in:n_in + 3], refs[n_in + 4:]):
            dst[...] = src[...].astype(BF16)
        body(*refs[:n_in], refs[n_in + 3])
    return kernel


def _ffn(x, g, wg, wu, wd, mix=None, ret_mix=None, cast_next=None):
    m = x.shape[0]
    w_specs = [_resident((1, D_MODEL)), _resident((D_MODEL, D_FF)), _resident((D_MODEL, D_FF)),
               _resident((D_FF, D_MODEL))]
    if ret_mix is not None:
        o, proj, gn, wo = ret_mix
        tm = TM
        name, body, args = "ret_mix_ffn", _ret_mix_ffn_kernel, (x, o, proj, gn, wo)
        specs = [_rows(D_MODEL, tm), _rows(HEADS * RET_DV, tm), _rows(HEADS * RET_DV, tm),
                 _resident(gn.shape), _resident(wo.shape)]
    elif mix is not None:
        a, wo = mix
        tm = TM_FFN
        name, body, args = "mix_ffn", _mix_ffn_kernel, (x, a, wo)
        specs = [_rows(D_MODEL, tm), _rows(a.shape[1], tm), _resident(wo.shape)]
    else:
        tm = TM_FFN
        name, body, args, specs = "ffn", _ffn_kernel, (x,), [_rows(D_MODEL, tm)]
    steps = m // tm
    args = (*args, g, wg, wu, wd)
    in_specs = specs + w_specs
    out_shape = jax.ShapeDtypeStruct((m, D_MODEL), F32)
    out_specs = _rows(D_MODEL, tm)
    if cast_next is not None:
        *stacks, layer, half = cast_next
        body = _casting_next_weights(body, len(args))
        args = (*args, *stacks)
        slab = D_MODEL // steps
        last = D_FF // FF_CHUNK - 1
        assert steps > last
        in_specs = in_specs + [
            pl.BlockSpec((None, None, slab, D_FF), lambda i: (layer, half, i, 0)),
            pl.BlockSpec((None, None, slab, D_FF), lambda i: (layer, half, i, 0)),
            pl.BlockSpec((None, None, FF_CHUNK, D_MODEL),
                         lambda i: (layer, half, jnp.minimum(i, last), 0))]
        out_shape = (out_shape, jax.ShapeDtypeStruct((D_MODEL, D_FF), BF16),
                     jax.ShapeDtypeStruct((D_MODEL, D_FF), BF16),
                     jax.ShapeDtypeStruct((D_FF, D_MODEL), BF16))
        out_specs = (out_specs, pl.BlockSpec((slab, D_FF), lambda i: (i, 0)),
                     pl.BlockSpec((slab, D_FF), lambda i: (i, 0)),
                     pl.BlockSpec((FF_CHUNK, D_MODEL), lambda i: (jnp.minimum(i, last), 0)))
    out = pl.pallas_call(
        body,
        out_shape=out_shape,
        grid=(steps,),
        in_specs=in_specs,
        out_specs=out_specs,
        compiler_params=_params("arbitrary"),
        name=name,
    )(*args)
    if cast_next is None:
        return out, None
    return out[0], out[1:]


def _ret_proj_kernel(x_ref, g_ref, w_ref, cos_ref, sin_ref, o_ref, kt_ref):
    h = _rms(x_ref[...], g_ref[...], D_MODEL).astype(BF16)
    cos = cos_ref[...]
    sin = sin_ref[...]
    q = _dot(h, w_ref[:, :RET_QK])
    k = _dot(h, w_ref[:, RET_QK:2 * RET_QK])
    for hd in range(HEADS):
        head = slice(hd * RET_DK, (hd + 1) * RET_DK)
        o_ref[:, RET_Q0 + hd * RET_DK:RET_Q0 + (hd + 1) * RET_DK] = (
            _rope128(q[:, head], cos, sin).astype(BF16))
        kh = _rope128(k[:, head], cos, sin) * RET_DK ** -0.5
        for c in range(TM // RET_CHUNK):
            kt_ref[c, head, :] = kh[c * RET_CHUNK:(c + 1) * RET_CHUNK, :].T.astype(BF16)
    v0 = 2 * RET_QK
    g0 = v0 + HEADS * RET_DV
    o_ref[:, RET_V0:RET_Q0] = _dot(h, w_ref[:, v0:g0]).astype(BF16)
    o_ref[:, :RET_V0] = _dot(h, w_ref[:, g0:]).astype(BF16)


def _ret_proj(x, g, w, cos, sin):
    m = x.shape[0]
    chunks = TM // RET_CHUNK
    return pl.pallas_call(
        _ret_proj_kernel,
        out_shape=(jax.ShapeDtypeStruct((m, RET_PROJ - RET_QK), BF16),
                   jax.ShapeDtypeStruct((m // RET_CHUNK, RET_QK, RET_CHUNK), BF16)),
        grid=(m // TM,),
        in_specs=[_rows(D_MODEL), _resident((1, D_MODEL)), _resident((D_MODEL, RET_PROJ)),
                  _rows(LANES), _rows(LANES)],
        out_specs=(_rows(RET_PROJ - RET_QK),
                   pl.BlockSpec((chunks, RET_QK, RET_CHUNK), lambda i: (i, 0, 0))),
        compiler_params=_params("parallel"),
        name="ret_proj",
    )(x, g, w, cos, sin)


def _retention_kernel(q_ref, kt_ref, v_ref, o_ref, state_ref):
    c = RET_CHUNK
    seq = q_ref.shape[0]

    def index(shape, axis):
        return lax.broadcasted_iota(jnp.int32, shape, axis).astype(F32)

    def decays(sub):
        hd = (pl.program_id(1) * RET_HEADS_PER_STEP + sub).astype(F32)

        def log_decay(shape):
            return jnp.log1p(-jnp.exp2(jnp.full(shape, -5.0, F32) - hd))

        diff = index((c, c), 0) - index((c, c), 1)
        d_intra = jnp.where(diff >= 0, jnp.exp(log_decay((c, c)) * jnp.maximum(diff, 0.0)), 0.0)
        q_decay = jnp.exp(log_decay((c, RET_DV)) * (index((c, RET_DV), 0) + 1.0))
        k_decay = jnp.exp(log_decay((RET_DK, c)) * (c - 1.0 - index((RET_DK, c), 1)))
        chunk_decay = jnp.exp(log_decay((RET_DK, RET_DV)) * c)
        return d_intra, q_decay, k_decay, chunk_decay

    per_head = [decays(sub) for sub in range(RET_HEADS_PER_STEP)]
    state_ref[...] = jnp.zeros_like(state_ref)

    def body(t, carry):
        off = pl.multiple_of(t * c, c)
        for sub, (d_intra, q_decay, k_decay, chunk_decay) in enumerate(per_head):
            qk = slice(sub * RET_DK, (sub + 1) * RET_DK)
            vo = slice(sub * RET_DV, (sub + 1) * RET_DV)
            q = q_ref[pl.ds(off, c), qk]
            kt = kt_ref[t, qk, :]
            v = v_ref[pl.ds(off, c), vo]
            state = state_ref[sub]
            s = _dot(q, kt) * d_intra
            o = _dot(s.astype(BF16), v) + _dot(q, state.astype(BF16)) * q_decay
            ktd = (kt.astype(F32) * k_decay).astype(BF16)
            state_ref[sub] = state * chunk_decay + _dot(ktd, v)
            o_ref[pl.ds(off, c), vo] = o.astype(BF16)
        return carry

    lax.fori_loop(0, seq // c, body, 0, unroll=8)


def _retention(proj, kt, batch, seq):
    m = proj.shape[0]
    g = RET_HEADS_PER_STEP
    qblk = RET_Q0 // (g * RET_DK)
    vblk = RET_V0 // (g * RET_DV)
    return pl.pallas_call(
        _retention_kernel,
        out_shape=jax.ShapeDtypeStruct((m, HEADS * RET_DV), BF16),
        grid=(batch, HEADS // g),
        in_specs=[
            pl.BlockSpec((seq, g * RET_DK), lambda b, h: (b, qblk + h)),
            pl.BlockSpec((seq // RET_CHUNK, g * RET_DK, RET_CHUNK), lambda b, h: (b, h, 0)),
            pl.BlockSpec((seq, g * RET_DV), lambda b, h: (b, vblk + h)),
        ],
        out_specs=pl.BlockSpec((seq, g * RET_DV), lambda b, h: (b, h)),
        scratch_shapes=[pltpu.VMEM((g, RET_DK, RET_DV), F32)],
        compiler_params=_params("parallel", "parallel"),
        name="retention",
    )(proj, kt, proj)


def _rope_tile_for_head(tile, hd):
    lane = lax.broadcasted_iota(jnp.int32, tile.shape, 1)
    keep = (lane < MLA_ROPE) if hd % 2 == 0 else (lane >= MLA_ROPE)
    return jnp.where(keep, tile, jnp.zeros_like(tile))


def _kv_kernel(x_ref, g_ref, wdc_ref, wdr_ref, lat_g_ref, wk_ref, wv_ref, kn_g_ref, kr_g_ref,
               cos_ref, sa_ref, sb_ref, k_out, v_out):
    h = _rms(x_ref[...], g_ref[...], D_MODEL).astype(BF16)
    lat = _rms(_dot(h, wdc_ref[...]), lat_g_ref[...], KV_LORA).astype(BF16)
    pe = _rms(_dot(h, wdr_ref[...]), kr_g_ref[...], LANES)
    pe = _rope64x2(pe, cos_ref[...], sa_ref[...], sb_ref[...]).astype(BF16)
    pe_tiles = (_rope_tile_for_head(pe, 0), _rope_tile_for_head(pe, 1))
    ones = jnp.ones((pe.shape[0], V_PAD - MLA_V), BF16)
    mean_nope = _group_mean_matrix(2 * MLA_NOPE, MLA_NOPE)
    kn_g = kn_g_ref[...]
    for pair in range(HEADS // 2):
        lo = pair * 2 * MLA_NOPE
        kn = _group_rms(_dot(lat, wk_ref[:, lo:lo + 2 * MLA_NOPE]), mean_nope, kn_g)
        kn = kn.astype(BF16)
        vv = _dot(lat, wv_ref[:, pair * 2 * MLA_V:(pair + 1) * 2 * MLA_V]).astype(BF16)
        for sub in range(2):
            hd = 2 * pair + sub
            k_out[:, hd * MLA_QK_PAD:hd * MLA_QK_PAD + MLA_NOPE] = (
                kn[:, sub * MLA_NOPE:(sub + 1) * MLA_NOPE])
            k_out[:, hd * MLA_QK_PAD + MLA_NOPE:(hd + 1) * MLA_QK_PAD] = pe_tiles[sub]
            v_out[:, hd * V_PAD:hd * V_PAD + MLA_V] = vv[:, sub * MLA_V:(sub + 1) * MLA_V]
            v_out[:, hd * V_PAD + MLA_V:(hd + 1) * V_PAD] = ones


def _shared_kv(x, g, wdc, wdr, lat_g, wk, wv, kn_g, kr_g, cos, sa, sb):
    m = x.shape[0]
    return pl.pallas_call(
        _kv_kernel,
        out_shape=(jax.ShapeDtypeStruct((m, HEADS * MLA_QK_PAD), BF16),
                   jax.ShapeDtypeStruct((m, HEADS * V_PAD), BF16)),
        grid=(m // TM_MLA,),
        in_specs=[_rows(D_MODEL, TM_MLA), _resident((1, D_MODEL)), _resident((D_MODEL, KV_LORA)),
                  _resident((D_MODEL, LANES)), _resident((1, KV_LORA)),
                  _resident((KV_LORA, HEADS * MLA_NOPE)), _resident((KV_LORA, HEADS * MLA_V)),
                  _resident((1, 2 * MLA_NOPE)), _resident((1, LANES)),
                  _rows(LANES, TM_MLA), _rows(LANES, TM_MLA), _rows(LANES, TM_MLA)],
        out_specs=(_rows(HEADS * MLA_QK_PAD, TM_MLA), _rows(HEADS * V_PAD, TM_MLA)),
        compiler_params=_params("parallel"),
        name="shared_kv",
    )(x, g, wdc, wdr, lat_g, wk, wv, kn_g, kr_g, cos, sa, sb)


Q_SCALE = (MLA_NOPE + MLA_ROPE) ** -0.5 * math.log2(math.e)
Q_NOPE_COLS = HEADS * MLA_NOPE


def _q_kernel(x_ref, g_ref, wdq_ref, lora_g_ref, wuq_ref, qn_g_ref, qr_g_ref,
              cos_ref, sa_ref, sb_ref, q_out):
    h = _rms(x_ref[...], g_ref[...], D_MODEL).astype(BF16)
    cq = _rms(_dot(h, wdq_ref[...]), lora_g_ref[...], Q_LORA).astype(BF16)
    q = _dot(cq, wuq_ref[...])
    mean_nope = _group_mean_matrix(2 * MLA_NOPE, MLA_NOPE)
    mean_rope = _group_mean_matrix(2 * LANES, MLA_ROPE)
    qn_g = qn_g_ref[...] * Q_SCALE
    qr_g = qr_g_ref[...] * Q_SCALE
    cos = cos_ref[...]
    sa = sa_ref[...]
    sb = sb_ref[...]
    for quad in range(HEADS // 4):
        lo = Q_NOPE_COLS + quad * 2 * LANES
        qp = _group_rms(q[:, lo:lo + 2 * LANES], mean_rope, qr_g)
        for half in range(2):
            pair = 2 * quad + half
            lo = pair * 2 * MLA_NOPE
            qn = _group_rms(q[:, lo:lo + 2 * MLA_NOPE], mean_nope, qn_g).astype(BF16)
            tile = _rope64x2(qp[:, half * LANES:(half + 1) * LANES], cos, sa, sb).astype(BF16)
            for sub in range(2):
                hd = 2 * pair + sub
                q_out[:, hd * MLA_QK_PAD:hd * MLA_QK_PAD + MLA_NOPE] = (
                    qn[:, sub * MLA_NOPE:(sub + 1) * MLA_NOPE])
                q_out[:, hd * MLA_QK_PAD + MLA_NOPE:(hd + 1) * MLA_QK_PAD] = (
                    _rope_tile_for_head(tile, sub))


def _mla_q(x, g, wdq, lora_g, wuq, qn_g, qr_g, cos, sa, sb):
    m = x.shape[0]
    return pl.pallas_call(
        _q_kernel,
        out_shape=jax.ShapeDtypeStruct((m, HEADS * MLA_QK_PAD), BF16),
        grid=(m // TM_MLA,),
        in_specs=[_rows(D_MODEL, TM_MLA), _resident((1, D_MODEL)), _resident((D_MODEL, Q_LORA)),
                  _resident((1, Q_LORA)), _resident(wuq.shape),
                  _resident((1, 2 * MLA_NOPE)), _resident((1, 2 * LANES)),
                  _rows(LANES, TM_MLA), _rows(LANES, TM_MLA), _rows(LANES, TM_MLA)],
        out_specs=_rows(HEADS * MLA_QK_PAD, TM_MLA),
        compiler_params=_params("parallel"),
        name="mla_q",
    )(x, g, wdq, lora_g, wuq, qn_g, qr_g, cos, sa, sb)


def _flash_kernel(q_ref, k_ref, v_ref, o_ref, m_ref, acc_ref):
    tiles = q_ref.shape[0] // TQ
    i = pl.program_id(2)
    pair = (i, tiles - 1 - i)
    m_ref[...] = jnp.full_like(m_ref, NEG_BIG)
    acc_ref[...] = jnp.zeros_like(acc_ref)

    def item(k):
        if k < tiles - 1:
            second = k >= i
            return (jnp.where(second, 1, 0), jnp.where(second, pair[1], pair[0]),
                    jnp.where(second, k - i, k), False)
        which = k - (tiles - 1)
        return which, pair[which], pair[which], True

    def scores(k):
        _, qt, kt, _ = item(k)
        rows = pl.ds(pl.multiple_of(qt * TQ, TQ), TQ)
        keys = pl.ds(pl.multiple_of(kt * TQ, TQ), TQ)
        return [_dot_nt(q_ref[rows, hd * MLA_QK_PAD:(hd + 1) * MLA_QK_PAD],
                        k_ref[keys, hd * MLA_QK_PAD:(hd + 1) * MLA_QK_PAD])
                for hd in range(HEADS_PER_STEP)]

    def consume(k, s_heads):
        which, _, kt, diagonal = item(k)
        keys = pl.ds(pl.multiple_of(kt * TQ, TQ), TQ)
        for hd, s in enumerate(s_heads):
            if diagonal:
                row = lax.broadcasted_iota(jnp.int32, (TQ, TQ), 0)
                col = lax.broadcasted_iota(jnp.int32, (TQ, TQ), 1)
                s = jnp.where(col <= row, s, NEG_BIG)
            m_prev = m_ref[which, hd]
            m_new = jnp.maximum(m_prev, jnp.max(s, axis=1, keepdims=True))
            alpha = jnp.exp2(m_prev - m_new)
            p = jnp.exp2(s - _lane_tile(m_new, TQ // LANES))
            pv = _dot(p.astype(BF16), v_ref[keys, hd * V_PAD:(hd + 1) * V_PAD])
            acc_ref[which, hd] = _lane_tile(alpha, V_PAD // LANES) * acc_ref[which, hd] + pv
            m_ref[which, hd] = m_new

    n_items = tiles + 1
    pending = [scores(k) for k in range(SCORES_AHEAD)]
    for k in range(n_items):
        if k + SCORES_AHEAD < n_items:
            pending.append(scores(k + SCORES_AHEAD))
        consume(k, pending.pop(0))

    for which in range(2):
        rows = pl.ds(pl.multiple_of(pair[which] * TQ, TQ), TQ)
        for hd in range(HEADS_PER_STEP):
            acc = acc_ref[which, hd]
            o_ref[rows, hd * MLA_V:(hd + 1) * MLA_V] = (acc[:, :MLA_V] / acc[:, MLA_V:]).astype(BF16)


def _flash(q, k, v, batch, seq):
    m = q.shape[0]
    g = HEADS_PER_STEP
    tiles = seq // TQ
    assert tiles % 2 == 0, "query tiles are processed in (i, tiles-1-i) pairs"
    return pl.pallas_call(
        _flash_kernel,
        out_shape=jax.ShapeDtypeStruct((m, HEADS * MLA_V), BF16),
        grid=(batch, HEADS // g, tiles // 2),
        in_specs=[
            pl.BlockSpec((seq, g * MLA_QK_PAD), lambda b, h, i: (b, h)),
            pl.BlockSpec((seq, g * MLA_QK_PAD), lambda b, h, i: (b, h)),
            pl.BlockSpec((seq, g * V_PAD), lambda b, h, i: (b, h)),
        ],
        out_specs=pl.BlockSpec((seq, g * MLA_V), lambda b, h, i: (b, h)),
        scratch_shapes=[pltpu.VMEM((2, g, TQ, LANES), F32),
                        pltpu.VMEM((2, g, TQ, V_PAD), F32)],
        compiler_params=_params("parallel", "parallel", "arbitrary"),
        name="flash_attention",
    )(q, k, v)


def _row(g, repeat=1):
    return jnp.tile(g.reshape(1, -1).astype(F32), (1, repeat))


def kernel(x, positions, norm_g, ffn_w_gate, ffn_w_up, ffn_w_down, ret_w_in, ret_gn_g, ret_w_o,
           kv_norm_g, kv_w_down, kv_latent_norm_g, kv_w_up, k_nope_norm_g, k_rope_norm_g,
           mla_w_dq, mla_q_lora_norm_g, mla_w_uq, mla_q_nope_norm_g, mla_q_rope_norm_g, mla_w_o):
    batch, seq, d = x.shape
    depth = norm_g.shape[0]
    n_self = ret_w_in.shape[0]
    m = batch * seq
    x = x.reshape(m, d)
    rcos, rsin, mcos, msa, msb = _rope_tables(positions.reshape(m, 1))
    stacks = (ffn_w_gate, ffn_w_up, ffn_w_down)
    ffn_weights = [tuple(w[0, 0].astype(BF16) for w in stacks)]

    def ffn(x, layer, i, **mix):
        nxt = (layer, i + 1) if i == 0 else (layer + 1, 0)
        cast_next = (*stacks, *nxt) if nxt[0] < depth else None
        wg, wu, wd = ffn_weights.pop()
        x, cast = _ffn(x, _row(norm_g[layer, 2 * i]), wg, wu, wd, cast_next=cast_next, **mix)
        ffn_weights.append(cast)
        return x

    k_shared = v_shared = None
    for layer in range(depth):
        x = ffn(x, layer, 0)
        g_mix = _row(norm_g[layer, 1])
        if layer < n_self:
            proj, kt = _ret_proj(x, g_mix, ret_w_in[layer].astype(BF16), rcos, rsin)
            mix = dict(ret_mix=(_retention(proj, kt, batch, seq), proj, _row(ret_gn_g[layer]),
                                ret_w_o[layer].astype(BF16)))
        else:
            j = layer - n_self
            wuq = mla_w_uq[j].reshape(Q_LORA, HEADS, MLA_NOPE + MLA_ROPE)
            wuq = jnp.concatenate([wuq[:, :, :MLA_NOPE].reshape(Q_LORA, Q_NOPE_COLS),
                                   wuq[:, :, MLA_NOPE:].reshape(Q_LORA, HEADS * MLA_ROPE)], axis=1)
            q = _mla_q(x, g_mix, mla_w_dq[j].astype(BF16), _row(mla_q_lora_norm_g[j]),
                       wuq.astype(BF16), _row(mla_q_nope_norm_g[j], 2), _row(mla_q_rope_norm_g[j], 4),
                       mcos, msa, msb)
            mix = dict(mix=(_flash(q, k_shared, v_shared, batch, seq), mla_w_o[j].astype(BF16)))
        x = ffn(x, layer, 1, **mix)
        if layer == n_self - 1:
            wup = kv_w_up.reshape(KV_LORA, HEADS, MLA_NOPE + MLA_V)
            wk = wup[:, :, :MLA_NOPE].reshape(KV_LORA, HEADS * MLA_NOPE).astype(BF16)
            wv = wup[:, :, MLA_NOPE:].reshape(KV_LORA, HEADS * MLA_V).astype(BF16)
            wdr = jnp.tile(kv_w_down[:, KV_LORA:], (1, 2)).astype(BF16)
            k_shared, v_shared = _shared_kv(
                x, _row(kv_norm_g), kv_w_down[:, :KV_LORA].astype(BF16), wdr,
                _row(kv_latent_norm_g), wk, wv, _row(k_nope_norm_g, 2), _row(k_rope_norm_g, 2),
                mcos, msa, msb)
    return x.reshape(batch, seq, d)
```

```python
import math

import jax
import jax.numpy as jnp
from jax import lax
from jax.experimental import pallas as pl
from jax.experimental.pallas import tpu as pltpu

F32 = jnp.float32
BF16 = jnp.bfloat16

LANES = 128

D_MODEL = 1024
D_FF = 2816
HEADS = 8
RET_DK = 128
RET_DV = 256
RET_QK = HEADS * RET_DK
RET_PROJ = 2 * RET_QK + 2 * HEADS * RET_DV
RET_V0 = HEADS * RET_DV
RET_Q0 = 2 * HEADS * RET_DV
MLA_NOPE = 128
MLA_ROPE = 64
MLA_V = 128
MLA_QK_PAD = 2 * LANES
V_PAD = 2 * LANES
HEADS_PER_STEP = 2
Q_LORA = 384
KV_LORA = 256
ROPE_BASE = 10000.0
EPS = 1e-6

VMEM_LIMIT = 56 * 1024 * 1024

TM = 512
TM_FFN = 1024
TM_MLA = 1024
FF_CHUNK = 256
RET_CHUNK = 256
RET_HEADS_PER_STEP = 2
TQ = 512
SCORES_AHEAD = 2
NEG_BIG = -1e30


def _dot(a, b):
    return jnp.dot(a, b, preferred_element_type=F32)


def _dot_nt(a, b):
    return lax.dot_general(a, b, (((1,), (1,)), ((), ())), preferred_element_type=F32)


def _rms(x, g, n):
    ms = jnp.sum(x * x, axis=-1, keepdims=True) * (1.0 / n)
    return x * lax.rsqrt(ms + EPS) * g


def _silu(x):
    return x / (1.0 + jnp.exp2(x * -math.log2(math.e)))


def _lane_tile(x, n):
    return jnp.concatenate([x] * n, axis=1)


def _group_mean_matrix(width, group):
    shift = group.bit_length() - 1
    r = lax.broadcasted_iota(jnp.int32, (width, width), 0) >> shift
    c = lax.broadcasted_iota(jnp.int32, (width, width), 1) >> shift
    return jnp.where(r == c, 1.0 / group, 0.0).astype(BF16)


def _group_rms(x, mean_matrix, g):
    ms = _dot((x * x).astype(BF16), mean_matrix)
    return x * lax.rsqrt(ms + EPS) * g


def _params(*sem):
    return pltpu.CompilerParams(dimension_semantics=sem, vmem_limit_bytes=VMEM_LIMIT)


def _resident(shape):
    nd = len(shape)
    return pl.BlockSpec(shape, lambda *_: (0,) * nd, pipeline_mode=pl.Buffered(1))


def _rows(width, tm=TM):
    return pl.BlockSpec((tm, width), lambda i: (i, 0))


def _rope_tables_kernel(pos_ref, rcos_ref, rsin_ref, mcos_ref, msa_ref, msb_ref):
    pos = pos_ref[...].astype(F32)
    lane = lax.broadcasted_iota(jnp.int32, (1, LANES), 1)
    is_ret = lane < 64
    freq = jnp.where(is_ret, lane, lane & 31).astype(F32)
    step = jnp.where(is_ret, -2.0 / RET_DK, -2.0 / MLA_ROPE) * math.log(ROPE_BASE)
    ang = pos * jnp.exp(freq * step)
    c = jnp.cos(ang)
    s = jnp.sin(ang)
    rcos_ref[...] = jnp.where(is_ret, c, pltpu.roll(c, 64, 1))
    rsin_ref[...] = jnp.where(is_ret, -s, pltpu.roll(s, 64, 1))

    def spread(t):
        quarter = lane >> 5
        return jnp.where(quarter == 0, pltpu.roll(t, 64, 1),
                         jnp.where(quarter == 1, pltpu.roll(t, 96, 1),
                                   jnp.where(quarter == 2, t, pltpu.roll(t, 32, 1))))

    s = spread(s)
    first_half = (lane & 63) < 32
    mcos_ref[...] = spread(c)
    msa_ref[...] = jnp.where(first_half, -s, 0.0)
    msb_ref[...] = jnp.where(first_half, 0.0, s)


def _rope_tables(pos):
    m = pos.shape[0]
    tm = 1024
    tab = jax.ShapeDtypeStruct((m, LANES), F32)
    spec = pl.BlockSpec((tm, LANES), lambda i: (i, 0))
    return pl.pallas_call(
        _rope_tables_kernel,
        out_shape=(tab,) * 5,
        grid=(m // tm,),
        in_specs=[pl.BlockSpec((tm, 1), lambda i: (i, 0))],
        out_specs=(spec,) * 5,
        compiler_params=_params("parallel"),
        name="rope_tables",
    )(pos)


def _rope128(x, cos, sin):
    return x * cos + pltpu.roll(x, 64, 1) * sin


def _rope64x2(x, cos, sa, sb):
    return x * cos + pltpu.roll(x, 96, 1) * sa + pltpu.roll(x, 32, 1) * sb


def _ffn_body(x, g_ref, wg_ref, wu_ref, wd_ref, o_ref):
    h = _rms(x, g_ref[...], D_MODEL).astype(BF16)
    hidden = []
    for c in range(D_FF // FF_CHUNK):
        sl = slice(c * FF_CHUNK, (c + 1) * FF_CHUNK)
        hidden.append((_silu(_dot(h, wg_ref[:, sl])) * _dot(h, wu_ref[:, sl])).astype(BF16))
    o_ref[...] = x + 0.5 * _dot(jnp.concatenate(hidden, axis=1), wd_ref[...])


def _ffn_kernel(x_ref, g_ref, wg_ref, wu_ref, wd_ref, o_ref):
    _ffn_body(x_ref[...], g_ref, wg_ref, wu_ref, wd_ref, o_ref)


def _mix_ffn_kernel(x_ref, a_ref, wo_ref, g_ref, wg_ref, wu_ref, wd_ref, o_ref):
    _ffn_body(x_ref[...] + _dot(a_ref[...], wo_ref[...]), g_ref, wg_ref, wu_ref, wd_ref, o_ref)


def _ret_mix_ffn_kernel(x_ref, ret_ref, gate_ref, gn_ref, wo_ref, g_ref, wg_ref, wu_ref, wd_ref,
                        o_ref):
    x = x_ref[...]
    for hd in range(HEADS):
        cols = slice(hd * RET_DV, (hd + 1) * RET_DV)
        o = ret_ref[:, cols].astype(F32)
        oc = o - jnp.mean(o, axis=-1, keepdims=True)
        var = jnp.mean(oc * oc, axis=-1, keepdims=True)
        on = oc * lax.rsqrt(var + EPS) * gn_ref[:, cols]
        a = (_silu(gate_ref[:, cols].astype(F32)) * on).astype(BF16)
        x = x + _dot(a, wo_ref[cols, :])
    _ffn_body(x, g_ref, wg_ref, wu_ref, wd_ref, o_ref)


def _casting_next_weights(body, n_in):
    def kernel(*refs):
        for src, dst in zip(refs[n_in:n_in + 3], refs[n_in + 4:]):
            dst[...] = src[...].astype(BF16)
        body(*refs[:n_in], refs[n_in + 3])
    return kernel


def _ffn(x, g, wg, wu, wd, mix=None, ret_mix=None, cast_next=None):
    m = x.shape[0]
    w_specs = [_resident((1, D_MODEL)), _resident((D_MODEL, D_FF)), _resident((D_MODEL, D_FF)),
               _resident((D_FF, D_MODEL))]
    if ret_mix is not None:
        o, proj, gn, wo = ret_mix
        tm = TM
        name, body, args = "ret_mix_ffn", _ret_mix_ffn_kernel, (x, o, proj, gn, wo)
        specs = [_rows(D_MODEL, tm), _rows(HEADS * RET_DV, tm), _rows(HEADS * RET_DV, tm),
                 _resident(gn.shape), _resident(wo.shape)]
    elif mix is not None:
        a, wo = mix
        tm = TM_FFN
        name, body, args = "mix_ffn", _mix_ffn_kernel, (x, a, wo)
        specs = [_rows(D_MODEL, tm), _rows(a.shape[1], tm), _resident(wo.shape)]
    else:
        tm = TM_FFN
        name, body, args, specs = "ffn", _ffn_kernel, (x,), [_rows(D_MODEL, tm)]
    steps = m // tm
    args = (*args, g, wg, wu, wd)
    in_specs = specs + w_specs
    out_shape = jax.ShapeDtypeStruct((m, D_MODEL), F32)
    out_specs = _rows(D_MODEL, tm)
    if cast_next is not None:
        *stacks, layer, half = cast_next
        body = _casting_next_weights(body, len(args))
        args = (*args, *stacks)
        slab = D_MODEL // steps
        last = D_FF // FF_CHUNK - 1
        assert steps > last
        in_specs = in_specs + [
            pl.BlockSpec((None, None, slab, D_FF), lambda i: (layer, half, i, 0)),
            pl.BlockSpec((None, None, slab, D_FF), lambda i: (layer, half, i, 0)),
            pl.BlockSpec((None, None, FF_CHUNK, D_MODEL),
                         lambda i: (layer, half, jnp.minimum(i, last), 0))]
        out_shape = (out_shape, jax.ShapeDtypeStruct((D_MODEL, D_FF), BF16),
                     jax.ShapeDtypeStruct((D_MODEL, D_FF), BF16),
                     jax.ShapeDtypeStruct((D_FF, D_MODEL), BF16))
        out_specs = (out_specs, pl.BlockSpec((slab, D_FF), lambda i: (i, 0)),
                     pl.BlockSpec((slab, D_FF), lambda i: (i, 0)),
                     pl.BlockSpec((FF_CHUNK, D_MODEL), lambda i: (jnp.minimum(i, last), 0)))
    out = pl.pallas_call(
        body,
        out_shape=out_shape,
        grid=(steps,),
        in_specs=in_specs,
        out_specs=out_specs,
        compiler_params=_params("arbitrary"),
        name=name,
    )(*args)
    if cast_next is None:
        return out, None
    return out[0], out[1:]


def _ret_proj_kernel(x_ref, g_ref, w_ref, cos_ref, sin_ref, o_ref, kt_ref):
    h = _rms(x_ref[...], g_ref[...], D_MODEL).astype(BF16)
    cos = cos_ref[...]
    sin = sin_ref[...]
    q = _dot(h, w_ref[:, :RET_QK])
    k = _dot(h, w_ref[:, RET_QK:2 * RET_QK])
    for hd in range(HEADS):
        head = slice(hd * RET_DK, (hd + 1) * RET_DK)
        o_ref[:, RET_Q0 + hd * RET_DK:RET_Q0 + (hd + 1) * RET_DK] = (
            _rope128(q[:, head], cos, sin).astype(BF16))
        kh = _rope128(k[:, head], cos, sin) * RET_DK ** -0.5
        for c in range(TM // RET_CHUNK):
            kt_ref[c, head, :] = kh[c * RET_CHUNK:(c + 1) * RET_CHUNK, :].T.astype(BF16)
    v0 = 2 * RET_QK
    g0 = v0 + HEADS * RET_DV
    o_ref[:, RET_V0:RET_Q0] = _dot(h, w_ref[:, v0:g0]).astype(BF16)
    o_ref[:, :RET_V0] = _dot(h, w_ref[:, g0:]).astype(BF16)


def _ret_proj(x, g, w, cos, sin):
    m = x.shape[0]
    chunks = TM // RET_CHUNK
    return pl.pallas_call(
        _ret_proj_kernel,
        out_shape=(jax.ShapeDtypeStruct((m, RET_PROJ - RET_QK), BF16),
                   jax.ShapeDtypeStruct((m // RET_CHUNK, RET_QK, RET_CHUNK), BF16)),
        grid=(m // TM,),
        in_specs=[_rows(D_MODEL), _resident((1, D_MODEL)), _resident((D_MODEL, RET_PROJ)),
                  _rows(LANES), _rows(LANES)],
        out_specs=(_rows(RET_PROJ - RET_QK),
                   pl.BlockSpec((chunks, RET_QK, RET_CHUNK), lambda i: (i, 0, 0))),
        compiler_params=_params("parallel"),
        name="ret_proj",
    )(x, g, w, cos, sin)


def _retention_kernel(q_ref, kt_ref, v_ref, o_ref, state_ref):
    c = RET_CHUNK
    seq = q_ref.shape[0]

    def index(shape, axis):
        return lax.broadcasted_iota(jnp.int32, shape, axis).astype(F32)

    def decays(sub):
        hd = (pl.program_id(1) * RET_HEADS_PER_STEP + sub).astype(F32)

        def log_decay(shape):
            return jnp.log1p(-jnp.exp2(jnp.full(shape, -5.0, F32) - hd))

        diff = index((c, c), 0) - index((c, c), 1)
        d_intra = jnp.where(diff >= 0, jnp.exp(log_decay((c, c)) * jnp.maximum(diff, 0.0)), 0.0)
        q_decay = jnp.exp(log_decay((c, RET_DV)) * (index((c, RET_DV), 0) + 1.0))
        k_decay = jnp.exp(log_decay((RET_DK, c)) * (c - 1.0 - index((RET_DK, c), 1)))
        chunk_decay = jnp.exp(log_decay((RET_DK, RET_DV)) * c)
        return d_intra, q_decay, k_decay, chunk_decay

    per_head = [decays(sub) for sub in range(RET_HEADS_PER_STEP)]
    state_ref[...] = jnp.zeros_like(state_ref)

    def body(t, carry):
        off = pl.multiple_of(t * c, c)
        for sub, (d_intra, q_decay, k_decay, chunk_decay) in enumerate(per_head):
            qk = slice(sub * RET_DK, (sub + 1) * RET_DK)
            vo = slice(sub * RET_DV, (sub + 1) * RET_DV)
            q = q_ref[pl.ds(off, c), qk]
            kt = kt_ref[t, qk, :]
            v = v_ref[pl.ds(off, c), vo]
            state = state_ref[sub]
            s = _dot(q, kt) * d_intra
            o = _dot(s.astype(BF16), v) + _dot(q, state.astype(BF16)) * q_decay
            ktd = (kt.astype(F32) * k_decay).astype(BF16)
            state_ref[sub] = state * chunk_decay + _dot(ktd, v)
            o_ref[pl.ds(off, c), vo] = o.astype(BF16)
        return carry

    lax.fori_loop(0, seq // c, body, 0, unroll=8)


def _retention(proj, kt, batch, seq):
    m = proj.shape[0]
    g = RET_HEADS_PER_STEP
    qblk = RET_Q0 // (g * RET_DK)
    vblk = RET_V0 // (g * RET_DV)
    return pl.pallas_call(
        _retention_kernel,
        out_shape=jax.ShapeDtypeStruct((m, HEADS * RET_DV), BF16),
        grid=(batch, HEADS // g),
        in_specs=[
            pl.BlockSpec((seq, g * RET_DK), lambda b, h: (b, qblk + h)),
            pl.BlockSpec((seq // RET_CHUNK, g * RET_DK, RET_CHUNK), lambda b, h: (b, h, 0)),
            pl.BlockSpec((seq, g * RET_DV), lambda b, h: (b, vblk + h)),
        ],
        out_specs=pl.BlockSpec((seq, g * RET_DV), lambda b, h: (b, h)),
        scratch_shapes=[pltpu.VMEM((g, RET_DK, RET_DV), F32)],
        compiler_params=_params("parallel", "parallel"),
        name="retention",
    )(proj, kt, proj)


def _rope_tile_for_head(tile, hd):
    lane = lax.broadcasted_iota(jnp.int32, tile.shape, 1)
    keep = (lane < MLA_ROPE) if hd % 2 == 0 else (lane >= MLA_ROPE)
    return jnp.where(keep, tile, jnp.zeros_like(tile))


def _kv_kernel(x_ref, g_ref, wdc_ref, wdr_ref, lat_g_ref, wk_ref, wv_ref, kn_g_ref, kr_g_ref,
               cos_ref, sa_ref, sb_ref, k_out, v_out):
    h = _rms(x_ref[...], g_ref[...], D_MODEL).astype(BF16)
    lat = _rms(_dot(h, wdc_ref[...]), lat_g_ref[...], KV_LORA).astype(BF16)
    pe = _rms(_dot(h, wdr_ref[...]), kr_g_ref[...], LANES)
    pe = _rope64x2(pe, cos_ref[...], sa_ref[...], sb_ref[...]).astype(BF16)
    pe_tiles = (_rope_tile_for_head(pe, 0), _rope_tile_for_head(pe, 1))
    ones = jnp.ones((pe.shape[0], V_PAD - MLA_V), BF16)
    mean_nope = _group_mean_matrix(2 * MLA_NOPE, MLA_NOPE)
    kn_g = kn_g_ref[...]
    for pair in range(HEADS // 2):
        lo = pair * 2 * MLA_NOPE
        kn = _group_rms(_dot(lat, wk_ref[:, lo:lo + 2 * MLA_NOPE]), mean_nope, kn_g)
        kn = kn.astype(BF16)
        vv = _dot(lat, wv_ref[:, pair * 2 * MLA_V:(pair + 1) * 2 * MLA_V]).astype(BF16)
        for sub in range(2):
            hd = 2 * pair + sub
            k_out[:, hd * MLA_QK_PAD:hd * MLA_QK_PAD + MLA_NOPE] = (
                kn[:, sub * MLA_NOPE:(sub + 1) * MLA_NOPE])
            k_out[:, hd * MLA_QK_PAD + MLA_NOPE:(hd + 1) * MLA_QK_PAD] = pe_tiles[sub]
            v_out[:, hd * V_PAD:hd * V_PAD + MLA_V] = vv[:, sub * MLA_V:(sub + 1) * MLA_V]
            v_out[:, hd * V_PAD + MLA_V:(hd + 1) * V_PAD] = ones


def _shared_kv(x, g, wdc, wdr, lat_g, wk, wv, kn_g, kr_g, cos, sa, sb):
    m = x.shape[0]
    return pl.pallas_call(
        _kv_kernel,
        out_shape=(jax.ShapeDtypeStruct((m, HEADS * MLA_QK_PAD), BF16),
                   jax.ShapeDtypeStruct((m, HEADS * V_PAD), BF16)),
        grid=(m // TM_MLA,),
        in_specs=[_rows(D_MODEL, TM_MLA), _resident((1, D_MODEL)), _resident((D_MODEL, KV_LORA)),
                  _resident((D_MODEL, LANES)), _resident((1, KV_LORA)),
                  _resident((KV_LORA, HEADS * MLA_NOPE)), _resident((KV_LORA, HEADS * MLA_V)),
                  _resident((1, 2 * MLA_NOPE)), _resident((1, LANES)),
                  _rows(LANES, TM_MLA), _rows(LANES, TM_MLA), _rows(LANES, TM_MLA)],
        out_specs=(_rows(HEADS * MLA_QK_PAD, TM_MLA), _rows(HEADS * V_PAD, TM_MLA)),
        compiler_params=_params("parallel"),
        name="shared_kv",
    )(x, g, wdc, wdr, lat_g, wk, wv, kn_g, kr_g, cos, sa, sb)


Q_SCALE = (MLA_NOPE + MLA_ROPE) ** -0.5 * math.log2(math.e)
Q_NOPE_COLS = HEADS * MLA_NOPE


def _q_kernel(x_ref, g_ref, wdq_ref, lora_g_ref, wuq_ref, qn_g_ref, qr_g_ref,
              cos_ref, sa_ref, sb_ref, q_out):
    h = _rms(x_ref[...], g_ref[...], D_MODEL).astype(BF16)
    cq = _rms(_dot(h, wdq_ref[...]), lora_g_ref[...], Q_LORA).astype(BF16)
    q = _dot(cq, wuq_ref[...])
    mean_nope = _group_mean_matrix(2 * MLA_NOPE, MLA_NOPE)
    mean_rope = _group_mean_matrix(2 * LANES, MLA_ROPE)
    qn_g = qn_g_ref[...] * Q_SCALE
    qr_g = qr_g_ref[...] * Q_SCALE
    cos = cos_ref[...]
    sa = sa_ref[...]
    sb = sb_ref[...]
    for quad in range(HEADS // 4):
        lo = Q_NOPE_COLS + quad * 2 * LANES
        qp = _group_rms(q[:, lo:lo + 2 * LANES], mean_rope, qr_g)
        for half in range(2):
            pair = 2 * quad + half
            lo = pair * 2 * MLA_NOPE
            qn = _group_rms(q[:, lo:lo + 2 * MLA_NOPE], mean_nope, qn_g).astype(BF16)
            tile = _rope64x2(qp[:, half * LANES:(half + 1) * LANES], cos, sa, sb).astype(BF16)
            for sub in range(2):
                hd = 2 * pair + sub
                q_out[:, hd * MLA_QK_PAD:hd * MLA_QK_PAD + MLA_NOPE] = (
                    qn[:, sub * MLA_NOPE:(sub + 1) * MLA_NOPE])
                q_out[:, hd * MLA_QK_PAD + MLA_NOPE:(hd + 1) * MLA_QK_PAD] = (
                    _rope_tile_for_head(tile, sub))


def _mla_q(x, g, wdq, lora_g, wuq, qn_g, qr_g, cos, sa, sb):
    m = x.shape[0]
    return pl.pallas_call(
        _q_kernel,
        out_shape=jax.ShapeDtypeStruct((m, HEADS * MLA_QK_PAD), BF16),
        grid=(m // TM_MLA,),
        in_specs=[_rows(D_MODEL, TM_MLA), _resident((1, D_MODEL)), _resident((D_MODEL, Q_LORA)),
                  _resident((1, Q_LORA)), _resident(wuq.shape),
                  _resident((1, 2 * MLA_NOPE)), _resident((1, 2 * LANES)),
                  _rows(LANES, TM_MLA), _rows(LANES, TM_MLA), _rows(LANES, TM_MLA)],
        out_specs=_rows(HEADS * MLA_QK_PAD, TM_MLA),
        compiler_params=_params("parallel"),
        name="mla_q",
    )(x, g, wdq, lora_g, wuq, qn_g, qr_g, cos, sa, sb)


def _flash_kernel(q_ref, k_ref, v_ref, o_ref, m_ref, acc_ref):
    tiles = q_ref.shape[0] // TQ
    i = pl.program_id(2)
    pair = (i, tiles - 1 - i)
    m_ref[...] = jnp.full_like(m_ref, NEG_BIG)
    acc_ref[...] = jnp.zeros_like(acc_ref)

    def item(k):
        if k < tiles - 1:
            second = k >= i
            return (jnp.where(second, 1, 0), jnp.where(second, pair[1], pair[0]),
                    jnp.where(second, k - i, k), False)
        which = k - (tiles - 1)
        return which, pair[which], pair[which], True

    def scores(k):
        _, qt, kt, _ = item(k)
        rows = pl.ds(pl.multiple_of(qt * TQ, TQ), TQ)
        keys = pl.ds(pl.multiple_of(kt * TQ, TQ), TQ)
        return [_dot_nt(q_ref[rows, hd * MLA_QK_PAD:(hd + 1) * MLA_QK_PAD],
                        k_ref[keys, hd * MLA_QK_PAD:(hd + 1) * MLA_QK_PAD])
                for hd in range(HEADS_PER_STEP)]

    def consume(k, s_heads):
        which, _, kt, diagonal = item(k)
        keys = pl.ds(pl.multiple_of(kt * TQ, TQ), TQ)
        for hd, s in enumerate(s_heads):
            if diagonal:
                row = lax.broadcasted_iota(jnp.int32, (TQ, TQ), 0)
                col = lax.broadcasted_iota(jnp.int32, (TQ, TQ), 1)
                s = jnp.where(col <= row, s, NEG_BIG)
            m_prev = m_ref[which, hd]
            m_new = jnp.maximum(m_prev, jnp.max(s, axis=1, keepdims=True))
            alpha = jnp.exp2(m_prev - m_new)
            p = jnp.exp2(s - _lane_tile(m_new, TQ // LANES))
            pv = _dot(p.astype(BF16), v_ref[keys, hd * V_PAD:(hd + 1) * V_PAD])
            acc_ref[which, hd] = _lane_tile(alpha, V_PAD // LANES) * acc_ref[which, hd] + pv
            m_ref[which, hd] = m_new

    n_items = tiles + 1
    pending = [scores(k) for k in range(SCORES_AHEAD)]
    for k in range(n_items):
        if k + SCORES_AHEAD < n_items:
            pending.append(scores(k + SCORES_AHEAD))
        consume(k, pending.pop(0))

    for which in range(2):
        rows = pl.ds(pl.multiple_of(pair[which] * TQ, TQ), TQ)
        for hd in range(HEADS_PER_STEP):
            acc = acc_ref[which, hd]
            o_ref[rows, hd * MLA_V:(hd + 1) * MLA_V] = (acc[:, :MLA_V] / acc[:, MLA_V:]).astype(BF16)


def _flash(q, k, v, batch, seq):
    m = q.shape[0]
    g = HEADS_PER_STEP
    tiles = seq // TQ
    assert tiles % 2 == 0, "query tiles are processed in (i, tiles-1-i) pairs"
    return pl.pallas_call(
        _flash_kernel,
        out_shape=jax.ShapeDtypeStruct((m, HEADS * MLA_V), BF16),
        grid=(batch, HEADS // g, tiles // 2),
        in_specs=[
            pl.BlockSpec((seq, g * MLA_QK_PAD), lambda b, h, i: (b, h)),
            pl.BlockSpec((seq, g * MLA_QK_PAD), lambda b, h, i: (b, h)),
            pl.BlockSpec((seq, g * V_PAD), lambda b, h, i: (b, h)),
        ],
        out_specs=pl.BlockSpec((seq, g * MLA_V), lambda b, h, i: (b, h)),
        scratch_shapes=[pltpu.VMEM((2, g, TQ, LANES), F32),
                        pltpu.VMEM((2, g, TQ, V_PAD), F32)],
        compiler_params=_params("parallel", "parallel", "arbitrary"),
        name="flash_attention",
    )(q, k, v)


def _row(g, repeat=1):
    return jnp.tile(g.reshape(1, -1).astype(F32), (1, repeat))


def kernel(x, positions, norm_g, ffn_w_gate, ffn_w_up, ffn_w_down, ret_w_in, ret_gn_g, ret_w_o,
           kv_norm_g, kv_w_down, kv_latent_norm_g, kv_w_up, k_nope_norm_g, k_rope_norm_g,
           mla_w_dq, mla_q_lora_norm_g, mla_w_uq, mla_q_nope_norm_g, mla_q_rope_norm_g, mla_w_o):
    batch, seq, d = x.shape
    depth = norm_g.shape[0]
    n_self = ret_w_in.shape[0]
    m = batch * seq
    x = x.reshape(m, d)
    rcos, rsin, mcos, msa, msb = _rope_tables(positions.reshape(m, 1))
    stacks = (ffn_w_gate, ffn_w_up, ffn_w_down)
    ffn_weights = [tuple(w[0, 0].astype(BF16) for w in stacks)]

    def ffn(x, layer, i, **mix):
        nxt = (layer, i + 1) if i == 0 else (layer + 1, 0)
        cast_next = (*stacks, *nxt) if nxt[0] < depth else None
        wg, wu, wd = ffn_weights.pop()
        x, cast = _ffn(x, _row(norm_g[layer, 2 * i]), wg, wu, wd, cast_next=cast_next, **mix)
        ffn_weights.append(cast)
        return x

    k_shared = v_shared = None
    for layer in range(depth):
        x = ffn(x, layer, 0)
        g_mix = _row(norm_g[layer, 1])
        if layer < n_self:
            proj, kt = _ret_proj(x, g_mix, ret_w_in[layer].astype(BF16), rcos, rsin)
            mix = dict(ret_mix=(_retention(proj, kt, batch, seq), proj, _row(ret_gn_g[layer]),
                                ret_w_o[layer].astype(BF16)))
        else:
            j = layer - n_self
            wuq = mla_w_uq[j].reshape(Q_LORA, HEADS, MLA_NOPE + MLA_ROPE)
            wuq = jnp.concatenate([wuq[:, :, :MLA_NOPE].reshape(Q_LORA, Q_NOPE_COLS),
                                   wuq[:, :, MLA_NOPE:].reshape(Q_LORA, HEADS * MLA_ROPE)], axis=1)
            q = _mla_q(x, g_mix, mla_w_dq[j].astype(BF16), _row(mla_q_lora_norm_g[j]),
                       wuq.astype(BF16), _row(mla_q_nope_norm_g[j], 2), _row(mla_q_rope_norm_g[j], 4),
                       mcos, msa, msb)
            mix = dict(mix=(_flash(q, k_shared, v_shared, batch, seq), mla_w_o[j].astype(BF16)))
        x = ffn(x, layer, 1, **mix)
        if layer == n_self - 1:
            wup = kv_w_up.reshape(KV_LORA, HEADS, MLA_NOPE + MLA_V)
            wk = wup[:, :, :MLA_NOPE].reshape(KV_LORA, HEADS * MLA_NOPE).astype(BF16)
            wv = wup[:, :, MLA_NOPE:].reshape(KV_LORA, HEADS * MLA_V).astype(BF16)
            wdr = jnp.tile(kv_w_down[:, KV_LORA:], (1, 2)).astype(BF16)
            k_shared, v_shared = _shared_kv(
                x, _row(kv_norm_g), kv_w_down[:, :KV_LORA].astype(BF16), wdr,
                _row(kv_latent_norm_g), wk, wv, _row(k_nope_norm_g, 2), _row(k_rope_norm_g, 2),
                mcos, msa, msb)
    return x.reshape(batch, seq, d)
```

```python
import math

import jax
import jax.numpy as jnp
from jax import lax
from jax.experimental import pallas as pl
from jax.experimental.pallas import tpu as pltpu

F32 = jnp.float32
BF16 = jnp.bfloat16

LANES = 128

D_MODEL = 1024
D_FF = 2816
HEADS = 8
RET_DK = 128
RET_DV = 256
RET_QK = HEADS * RET_DK
RET_PROJ = 2 * RET_QK + 2 * HEADS * RET_DV
RET_V0 = HEADS * RET_DV
RET_Q0 = 2 * HEADS * RET_DV
MLA_NOPE = 128
MLA_ROPE = 64
MLA_V = 128
MLA_QK_PAD = 2 * LANES
V_PAD = 2 * LANES
HEADS_PER_STEP = 2
Q_LORA = 384
KV_LORA = 256
ROPE_BASE = 10000.0
EPS = 1e-6

VMEM_LIMIT = 56 * 1024 * 1024

TM = 512
TM_FFN = 1024
TM_MLA = 1024
FF_CHUNK = 256
RET_CHUNK = 256
RET_HEADS_PER_STEP = 2
TQ = 512
SCORES_AHEAD = 1
NEG_BIG = -1e30


def _dot(a, b):
    return jnp.dot(a, b, preferred_element_type=F32)


def _dot_nt(a, b):
    return lax.dot_general(a, b, (((1,), (1,)), ((), ())), preferred_element_type=F32)


def _rms(x, g, n):
    ms = jnp.sum(x * x, axis=-1, keepdims=True) * (1.0 / n)
    return x * lax.rsqrt(ms + EPS) * g


def _silu(x):
    return x / (1.0 + jnp.exp2(x * -math.log2(math.e)))


def _lane_tile(x, n):
    return jnp.concatenate([x] * n, axis=1)


def _group_mean_matrix(width, group):
    shift = group.bit_length() - 1
    r = lax.broadcasted_iota(jnp.int32, (width, width), 0) >> shift
    c = lax.broadcasted_iota(jnp.int32, (width, width), 1) >> shift
    return jnp.where(r == c, 1.0 / group, 0.0).astype(BF16)


def _group_rms(x, mean_matrix, g):
    ms = _dot((x * x).astype(BF16), mean_matrix)
    return x * lax.rsqrt(ms + EPS) * g


def _params(*sem):
    return pltpu.CompilerParams(dimension_semantics=sem, vmem_limit_bytes=VMEM_LIMIT)


def _resident(shape):
    nd = len(shape)
    return pl.BlockSpec(shape, lambda *_: (0,) * nd, pipeline_mode=pl.Buffered(1))


def _rows(width, tm=TM):
    return pl.BlockSpec((tm, width), lambda i: (i, 0))


def _rope_tables_kernel(pos_ref, rcos_ref, rsin_ref, mcos_ref, msa_ref, msb_ref):
    pos = pos_ref[...].astype(F32)
    lane = lax.broadcasted_iota(jnp.int32, (1, LANES), 1)
    is_ret = lane < 64
    freq = jnp.where(is_ret, lane, lane & 31).astype(F32)
    step = jnp.where(is_ret, -2.0 / RET_DK, -2.0 / MLA_ROPE) * math.log(ROPE_BASE)
    ang = pos * jnp.exp(freq * step)
    c = jnp.cos(ang)
    s = jnp.sin(ang)
    rcos_ref[...] = jnp.where(is_ret, c, pltpu.roll(c, 64, 1))
    rsin_ref[...] = jnp.where(is_ret, -s, pltpu.roll(s, 64, 1))

    def spread(t):
        quarter = lane >> 5
        return jnp.where(quarter == 0, pltpu.roll(t, 64, 1),
                         jnp.where(quarter == 1, pltpu.roll(t, 96, 1),
                                   jnp.where(quarter == 2, t, pltpu.roll(t, 32, 1))))

    s = spread(s)
    first_half = (lane & 63) < 32
    mcos_ref[...] = spread(c)
    msa_ref[...] = jnp.where(first_half, -s, 0.0)
    msb_ref[...] = jnp.where(first_half, 0.0, s)


def _rope_tables(pos):
    m = pos.shape[0]
    tm = 1024
    tab = jax.ShapeDtypeStruct((m, LANES), F32)
    spec = pl.BlockSpec((tm, LANES), lambda i: (i, 0))
    return pl.pallas_call(
        _rope_tables_kernel,
        out_shape=(tab,) * 5,
        grid=(m // tm,),
        in_specs=[pl.BlockSpec((tm, 1), lambda i: (i, 0))],
        out_specs=(spec,) * 5,
        compiler_params=_params("parallel"),
        name="rope_tables",
    )(pos)


def _rope128(x, cos, sin):
    return x * cos + pltpu.roll(x, 64, 1) * sin


def _rope64x2(x, cos, sa, sb):
    return x * cos + pltpu.roll(x, 96, 1) * sa + pltpu.roll(x, 32, 1) * sb


def _ffn_body(x, g_ref, wg_ref, wu_ref, wd_ref, o_ref):
    h = _rms(x, g_ref[...], D_MODEL).astype(BF16)
    hidden = []
    for c in range(D_FF // FF_CHUNK):
        sl = slice(c * FF_CHUNK, (c + 1) * FF_CHUNK)
        hidden.append((_silu(_dot(h, wg_ref[:, sl])) * _dot(h, wu_ref[:, sl])).astype(BF16))
    o_ref[...] = x + 0.5 * _dot(jnp.concatenate(hidden, axis=1), wd_ref[...])


def _ffn_kernel(x_ref, g_ref, wg_ref, wu_ref, wd_ref, o_ref):
    _ffn_body(x_ref[...], g_ref, wg_ref, wu_ref, wd_ref, o_ref)


def _mix_ffn_kernel(x_ref, a_ref, wo_ref, g_ref, wg_ref, wu_ref, wd_ref, o_ref):
    _ffn_body(x_ref[...] + _dot(a_ref[...], wo_ref[...]), g_ref, wg_ref, wu_ref, wd_ref, o_ref)


def _ret_mix_ffn_kernel(x_ref, ret_ref, gate_ref, gn_ref, wo_ref, g_ref, wg_ref, wu_ref, wd_ref,
                        o_ref):
    x = x_ref[...]
    for hd in range(HEADS):
        cols = slice(hd * RET_DV, (hd + 1) * RET_DV)
        o = ret_ref[:, cols].astype(F32)
        oc = o - jnp.mean(o, axis=-1, keepdims=True)
        var = jnp.mean(oc * oc, axis=-1, keepdims=True)
        on = oc * lax.rsqrt(var + EPS) * gn_ref[:, cols]
        a = (_silu(gate_ref[:, cols].astype(F32)) * on).astype(BF16)
        x = x + _dot(a, wo_ref[cols, :])
    _ffn_body(x, g_ref, wg_ref, wu_ref, wd_ref, o_ref)


def _casting_next_weights(body, n_in):
    def kernel(*refs):
        for src, dst in zip(refs[n_in:n_in + 3], refs[n_in + 4:]):
            dst[...] = src[...].astype(BF16)
        body(*refs[:n_in], refs[n_in + 3])
    return kernel


def _ffn(x, g, wg, wu, wd, mix=None, ret_mix=None, cast_next=None):
    m = x.shape[0]
    w_specs = [_resident((1, D_MODEL)), _resident((D_MODEL, D_FF)), _resident((D_MODEL, D_FF)),
               _resident((D_FF, D_MODEL))]
    if ret_mix is not None:
        o, proj, gn, wo = ret_mix
        tm = TM
        name, body, args = "ret_mix_ffn", _ret_mix_ffn_kernel, (x, o, proj, gn, wo)
        specs = [_rows(D_MODEL, tm), _rows(HEADS * RET_DV, tm), _rows(HEADS * RET_DV, tm),
                 _resident(gn.shape), _resident(wo.shape)]
    elif mix is not None:
        a, wo = mix
        tm = TM_FFN
        name, body, args = "mix_ffn", _mix_ffn_kernel, (x, a, wo)
        specs = [_rows(D_MODEL, tm), _rows(a.shape[1], tm), _resident(wo.shape)]
    else:
        tm = TM_FFN
        name, body, args, specs = "ffn", _ffn_kernel, (x,), [_rows(D_MODEL, tm)]
    steps = m // tm
    args = (*args, g, wg, wu, wd)
    in_specs = specs + w_specs
    out_shape = jax.ShapeDtypeStruct((m, D_MODEL), F32)
    out_specs = _rows(D_MODEL, tm)
    if cast_next is not None:
        *stacks, layer, half = cast_next
        body = _casting_next_weights(body, len(args))
        args = (*args, *stacks)
        slab = D_MODEL // steps
        last = D_FF // FF_CHUNK - 1
        assert steps > last
        in_specs = in_specs + [
            pl.BlockSpec((None, None, slab, D_FF), lambda i: (layer, half, i, 0)),
            pl.BlockSpec((None, None, slab, D_FF), lambda i: (layer, half, i, 0)),
            pl.BlockSpec((None, None, FF_CHUNK, D_MODEL),
                         lambda i: (layer, half, jnp.minimum(i, last), 0))]
        out_shape = (out_shape, jax.ShapeDtypeStruct((D_MODEL, D_FF), BF16),
                     jax.ShapeDtypeStruct((D_MODEL, D_FF), BF16),
                     jax.ShapeDtypeStruct((D_FF, D_MODEL), BF16))
        out_specs = (out_specs, pl.BlockSpec((slab, D_FF), lambda i: (i, 0)),
                     pl.BlockSpec((slab, D_FF), lambda i: (i, 0)),
                     pl.BlockSpec((FF_CHUNK, D_MODEL), lambda i: (jnp.minimum(i, last), 0)))
    out = pl.pallas_call(
        body,
        out_shape=out_shape,
        grid=(steps,),
        in_specs=in_specs,
        out_specs=out_specs,
        compiler_params=_params("arbitrary"),
        name=name,
    )(*args)
    if cast_next is None:
        return out, None
    return out[0], out[1:]


def _ret_proj_kernel(x_ref, g_ref, w_ref, cos_ref, sin_ref, o_ref, kt_ref):
    h = _rms(x_ref[...], g_ref[...], D_MODEL).astype(BF16)
    cos = cos_ref[...]
    sin = sin_ref[...]
    q = _dot(h, w_ref[:, :RET_QK])
    k = _dot(h, w_ref[:, RET_QK:2 * RET_QK])
    for hd in range(HEADS):
        head = slice(hd * RET_DK, (hd + 1) * RET_DK)
        o_ref[:, RET_Q0 + hd * RET_DK:RET_Q0 + (hd + 1) * RET_DK] = (
            _rope128(q[:, head], cos, sin).astype(BF16))
        kh = _rope128(k[:, head], cos, sin) * RET_DK ** -0.5
        for c in range(TM // RET_CHUNK):
            kt_ref[c, head, :] = kh[c * RET_CHUNK:(c + 1) * RET_CHUNK, :].T.astype(BF16)
    v0 = 2 * RET_QK
    g0 = v0 + HEADS * RET_DV
    o_ref[:, RET_V0:RET_Q0] = _dot(h, w_ref[:, v0:g0]).astype(BF16)
    o_ref[:, :RET_V0] = _dot(h, w_ref[:, g0:]).astype(BF16)


def _ret_proj(x, g, w, cos, sin):
    m = x.shape[0]
    chunks = TM // RET_CHUNK
    return pl.pallas_call(
        _ret_proj_kernel,
        out_shape=(jax.ShapeDtypeStruct((m, RET_PROJ - RET_QK), BF16),
                   jax.ShapeDtypeStruct((m // RET_CHUNK, RET_QK, RET_CHUNK), BF16)),
        grid=(m // TM,),
        in_specs=[_rows(D_MODEL), _resident((1, D_MODEL)), _resident((D_MODEL, RET_PROJ)),
                  _rows(LANES), _rows(LANES)],
        out_specs=(_rows(RET_PROJ - RET_QK),
                   pl.BlockSpec((chunks, RET_QK, RET_CHUNK), lambda i: (i, 0, 0))),
        compiler_params=_params("parallel"),
        name="ret_proj",
    )(x, g, w, cos, sin)


def _retention_kernel(q_ref, kt_ref, v_ref, o_ref, state_ref):
    c = RET_CHUNK
    seq = q_ref.shape[0]

    def index(shape, axis):
        return lax.broadcasted_iota(jnp.int32, shape, axis).astype(F32)

    def decays(sub):
        hd = (pl.program_id(1) * RET_HEADS_PER_STEP + sub).astype(F32)

        def log_decay(shape):
            return jnp.log1p(-jnp.exp2(jnp.full(shape, -5.0, F32) - hd))

        diff = index((c, c), 0) - index((c, c), 1)
        d_intra = jnp.where(diff >= 0, jnp.exp(log_decay((c, c)) * jnp.maximum(diff, 0.0)), 0.0)
        q_decay = jnp.exp(log_decay((c, RET_DV)) * (index((c, RET_DV), 0) + 1.0))
        k_decay = jnp.exp(log_decay((RET_DK, c)) * (c - 1.0 - index((RET_DK, c), 1)))
        chunk_decay = jnp.exp(log_decay((RET_DK, RET_DV)) * c)
        return d_intra, q_decay, k_decay, chunk_decay

    per_head = [decays(sub) for sub in range(RET_HEADS_PER_STEP)]
    state_ref[...] = jnp.zeros_like(state_ref)

    def body(t, carry):
        off = pl.multiple_of(t * c, c)
        for sub, (d_intra, q_decay, k_decay, chunk_decay) in enumerate(per_head):
            qk = slice(sub * RET_DK, (sub + 1) * RET_DK)
            vo = slice(sub * RET_DV, (sub + 1) * RET_DV)
            q = q_ref[pl.ds(off, c), qk]
            kt = kt_ref[t, qk, :]
            v = v_ref[pl.ds(off, c), vo]
            state = state_ref[sub]
            s = _dot(q, kt) * d_intra
            o = _dot(s.astype(BF16), v) + _dot(q, state.astype(BF16)) * q_decay
            ktd = (kt.astype(F32) * k_decay).astype(BF16)
            state_ref[sub] = state * chunk_decay + _dot(ktd, v)
            o_ref[pl.ds(off, c), vo] = o.astype(BF16)
        return carry

    lax.fori_loop(0, seq // c, body, 0, unroll=16)


def _retention(proj, kt, batch, seq):
    m = proj.shape[0]
    g = RET_HEADS_PER_STEP
    qblk = RET_Q0 // (g * RET_DK)
    vblk = RET_V0 // (g * RET_DV)
    return pl.pallas_call(
        _retention_kernel,
        out_shape=jax.ShapeDtypeStruct((m, HEADS * RET_DV), BF16),
        grid=(batch, HEADS // g),
        in_specs=[
            pl.BlockSpec((seq, g * RET_DK), lambda b, h: (b, qblk + h)),
            pl.BlockSpec((seq // RET_CHUNK, g * RET_DK, RET_CHUNK), lambda b, h: (b, h, 0)),
            pl.BlockSpec((seq, g * RET_DV), lambda b, h: (b, vblk + h)),
        ],
        out_specs=pl.BlockSpec((seq, g * RET_DV), lambda b, h: (b, h)),
        scratch_shapes=[pltpu.VMEM((g, RET_DK, RET_DV), F32)],
        compiler_params=_params("parallel", "parallel"),
        name="retention",
    )(proj, kt, proj)


def _rope_tile_for_head(tile, hd):
    lane = lax.broadcasted_iota(jnp.int32, tile.shape, 1)
    keep = (lane < MLA_ROPE) if hd % 2 == 0 else (lane >= MLA_ROPE)
    return jnp.where(keep, tile, jnp.zeros_like(tile))


def _kv_kernel(x_ref, g_ref, wdc_ref, wdr_ref, lat_g_ref, wk_ref, wv_ref, kn_g_ref, kr_g_ref,
               cos_ref, sa_ref, sb_ref, k_out, v_out):
    h = _rms(x_ref[...], g_ref[...], D_MODEL).astype(BF16)
    lat = _rms(_dot(h, wdc_ref[...]), lat_g_ref[...], KV_LORA).astype(BF16)
    pe = _rms(_dot(h, wdr_ref[...]), kr_g_ref[...], LANES)
    pe = _rope64x2(pe, cos_ref[...], sa_ref[...], sb_ref[...]).astype(BF16)
    pe_tiles = (_rope_tile_for_head(pe, 0), _rope_tile_for_head(pe, 1))
    ones = jnp.ones((pe.shape[0], V_PAD - MLA_V), BF16)
    mean_nope = _group_mean_matrix(2 * MLA_NOPE, MLA_NOPE)
    kn_g = kn_g_ref[...]
    for pair in range(HEADS // 2):
        lo = pair * 2 * MLA_NOPE
        kn = _group_rms(_dot(lat, wk_ref[:, lo:lo + 2 * MLA_NOPE]), mean_nope, kn_g)
        kn = kn.astype(BF16)
        vv = _dot(lat, wv_ref[:, pair * 2 * MLA_V:(pair + 1) * 2 * MLA_V]).astype(BF16)
        for sub in range(2):
            hd = 2 * pair + sub
            k_out[:, hd * MLA_QK_PAD:hd * MLA_QK_PAD + MLA_NOPE] = (
                kn[:, sub * MLA_NOPE:(sub + 1) * MLA_NOPE])
            k_out[:, hd * MLA_QK_PAD + MLA_NOPE:(hd + 1) * MLA_QK_PAD] = pe_tiles[sub]
            v_out[:, hd * V_PAD:hd * V_PAD + MLA_V] = vv[:, sub * MLA_V:(sub + 1) * MLA_V]
            v_out[:, hd * V_PAD + MLA_V:(hd + 1) * V_PAD] = ones


def _shared_kv(x, g, wdc, wdr, lat_g, wk, wv, kn_g, kr_g, cos, sa, sb):
    m = x.shape[0]
    return pl.pallas_call(
        _kv_kernel,
        out_shape=(jax.ShapeDtypeStruct((m, HEADS * MLA_QK_PAD), BF16),
                   jax.ShapeDtypeStruct((m, HEADS * V_PAD), BF16)),
        grid=(m // TM_MLA,),
        in_specs=[_rows(D_MODEL, TM_MLA), _resident((1, D_MODEL)), _resident((D_MODEL, KV_LORA)),
                  _resident((D_MODEL, LANES)), _resident((1, KV_LORA)),
                  _resident((KV_LORA, HEADS * MLA_NOPE)), _resident((KV_LORA, HEADS * MLA_V)),
                  _resident((1, 2 * MLA_NOPE)), _resident((1, LANES)),
                  _rows(LANES, TM_MLA), _rows(LANES, TM_MLA), _rows(LANES, TM_MLA)],
        out_specs=(_rows(HEADS * MLA_QK_PAD, TM_MLA), _rows(HEADS * V_PAD, TM_MLA)),
        compiler_params=_params("parallel"),
        name="shared_kv",
    )(x, g, wdc, wdr, lat_g, wk, wv, kn_g, kr_g, cos, sa, sb)


Q_SCALE = (MLA_NOPE + MLA_ROPE) ** -0.5 * math.log2(math.e)
Q_NOPE_COLS = HEADS * MLA_NOPE


def _q_kernel(x_ref, g_ref, wdq_ref, lora_g_ref, wuq_ref, qn_g_ref, qr_g_ref,
              cos_ref, sa_ref, sb_ref, q_out):
    h = _rms(x_ref[...], g_ref[...], D_MODEL).astype(BF16)
    cq = _rms(_dot(h, wdq_ref[...]), lora_g_ref[...], Q_LORA).astype(BF16)
    q = _dot(cq, wuq_ref[...])
    mean_nope = _group_mean_matrix(2 * MLA_NOPE, MLA_NOPE)
    mean_rope = _group_mean_matrix(2 * LANES, MLA_ROPE)
    qn_g = qn_g_ref[...] * Q_SCALE
    qr_g = qr_g_ref[...] * Q_SCALE
    cos = cos_ref[...]
    sa = sa_ref[...]
    sb = sb_ref[...]
    for quad in range(HEADS // 4):
        lo = Q_NOPE_COLS + quad * 2 * LANES
        qp = _group_rms(q[:, lo:lo + 2 * LANES], mean_rope, qr_g)
        for half in range(2):
            pair = 2 * quad + half
            lo = pair * 2 * MLA_NOPE
            qn = _group_rms(q[:, lo:lo + 2 * MLA_NOPE], mean_nope, qn_g).astype(BF16)
            tile = _rope64x2(qp[:, half * LANES:(half + 1) * LANES], cos, sa, sb).astype(BF16)
            for sub in range(2):
                hd = 2 * pair + sub
                q_out[:, hd * MLA_QK_PAD:hd * MLA_QK_PAD + MLA_NOPE] = (
                    qn[:, sub * MLA_NOPE:(sub + 1) * MLA_NOPE])
                q_out[:, hd * MLA_QK_PAD + MLA_NOPE:(hd + 1) * MLA_QK_PAD] = (
                    _rope_tile_for_head(tile, sub))


def _mla_q(x, g, wdq, lora_g, wuq, qn_g, qr_g, cos, sa, sb):
    m = x.shape[0]
    return pl.pallas_call(
        _q_kernel,
        out_shape=jax.ShapeDtypeStruct((m, HEADS * MLA_QK_PAD), BF16),
        grid=(m // TM_MLA,),
        in_specs=[_rows(D_MODEL, TM_MLA), _resident((1, D_MODEL)), _resident((D_MODEL, Q_LORA)),
                  _resident((1, Q_LORA)), _resident(wuq.shape),
                  _resident((1, 2 * MLA_NOPE)), _resident((1, 2 * LANES)),
                  _rows(LANES, TM_MLA), _rows(LANES, TM_MLA), _rows(LANES, TM_MLA)],
        out_specs=_rows(HEADS * MLA_QK_PAD, TM_MLA),
        compiler_params=_params("parallel"),
        name="mla_q",
    )(x, g, wdq, lora_g, wuq, qn_g, qr_g, cos, sa, sb)


def _flash_kernel(q_ref, k_ref, v_ref, o_ref, m_ref, acc_ref):
    tiles = q_ref.shape[0] // TQ
    i = pl.program_id(2)
    pair = (i, tiles - 1 - i)
    m_ref[...] = jnp.full_like(m_ref, NEG_BIG)
    acc_ref[...] = jnp.zeros_like(acc_ref)

    def item(k):
        if k < tiles - 1:
            second = k >= i
            return (jnp.where(second, 1, 0), jnp.where(second, pair[1], pair[0]),
                    jnp.where(second, k - i, k), False)
        which = k - (tiles - 1)
        return which, pair[which], pair[which], True

    def scores(k):
        _, qt, kt, _ = item(k)
        rows = pl.ds(pl.multiple_of(qt * TQ, TQ), TQ)
        keys = pl.ds(pl.multiple_of(kt * TQ, TQ), TQ)
        return [_dot_nt(q_ref[rows, hd * MLA_QK_PAD:(hd + 1) * MLA_QK_PAD],
                        k_ref[keys, hd * MLA_QK_PAD:(hd + 1) * MLA_QK_PAD])
                for hd in range(HEADS_PER_STEP)]

    def consume(k, s_heads):
        which, _, kt, diagonal = item(k)
        keys = pl.ds(pl.multiple_of(kt * TQ, TQ), TQ)
        for hd, s in enumerate(s_heads):
            if diagonal:
                row = lax.broadcasted_iota(jnp.int32, (TQ, TQ), 0)
                col = lax.broadcasted_iota(jnp.int32, (TQ, TQ), 1)
                s = jnp.where(col <= row, s, NEG_BIG)
            m_prev = m_ref[which, hd]
            m_new = jnp.maximum(m_prev, jnp.max(s, axis=1, keepdims=True))
            alpha = jnp.exp2(m_prev - m_new)
            p = jnp.exp2(s - _lane_tile(m_new, TQ // LANES))
            pv = _dot(p.astype(BF16), v_ref[keys, hd * V_PAD:(hd + 1) * V_PAD])
            acc_ref[which, hd] = _lane_tile(alpha, V_PAD // LANES) * acc_ref[which, hd] + pv
            m_ref[which, hd] = m_new

    n_items = tiles + 1
    pending = [scores(k) for k in range(SCORES_AHEAD)]
    for k in range(n_items):
        if k + SCORES_AHEAD < n_items:
            pending.append(scores(k + SCORES_AHEAD))
        consume(k, pending.pop(0))

    for which in range(2):
        rows = pl.ds(pl.multiple_of(pair[which] * TQ, TQ), TQ)
        for hd in range(HEADS_PER_STEP):
            acc = acc_ref[which, hd]
            o_ref[rows, hd * MLA_V:(hd + 1) * MLA_V] = (acc[:, :MLA_V] / acc[:, MLA_V:]).astype(BF16)


def _flash(q, k, v, batch, seq):
    m = q.shape[0]
    g = HEADS_PER_STEP
    tiles = seq // TQ
    assert tiles % 2 == 0, "query tiles are processed in (i, tiles-1-i) pairs"
    return pl.pallas_call(
        _flash_kernel,
        out_shape=jax.ShapeDtypeStruct((m, HEADS * MLA_V), BF16),
        grid=(batch, HEADS // g, tiles // 2),
        in_specs=[
            pl.BlockSpec((seq, g * MLA_QK_PAD), lambda b, h, i: (b, h)),
            pl.BlockSpec((seq, g * MLA_QK_PAD), lambda b, h, i: (b, h)),
            pl.BlockSpec((seq, g * V_PAD), lambda b, h, i: (b, h)),
        ],
        out_specs=pl.BlockSpec((seq, g * MLA_V), lambda b, h, i: (b, h)),
        scratch_shapes=[pltpu.VMEM((2, g, TQ, LANES), F32),
                        pltpu.VMEM((2, g, TQ, V_PAD), F32)],
        compiler_params=_params("parallel", "parallel", "arbitrary"),
        name="flash_attention",
    )(q, k, v)


def _row(g, repeat=1):
    return jnp.tile(g.reshape(1, -1).astype(F32), (1, repeat))


def kernel(x, positions, norm_g, ffn_w_gate, ffn_w_up, ffn_w_down, ret_w_in, ret_gn_g, ret_w_o,
           kv_norm_g, kv_w_down, kv_latent_norm_g, kv_w_up, k_nope_norm_g, k_rope_norm_g,
           mla_w_dq, mla_q_lora_norm_g, mla_w_uq, mla_q_nope_norm_g, mla_q_rope_norm_g, mla_w_o):
    batch, seq, d = x.shape
    depth = norm_g.shape[0]
    n_self = ret_w_in.shape[0]
    m = batch * seq
    x = x.reshape(m, d)
    rcos, rsin, mcos, msa, msb = _rope_tables(positions.reshape(m, 1))
    stacks = (ffn_w_gate, ffn_w_up, ffn_w_down)
    ffn_weights = [tuple(w[0, 0].astype(BF16) for w in stacks)]

    def ffn(x, layer, i, **mix):
        nxt = (layer, i + 1) if i == 0 else (layer + 1, 0)
        cast_next = (*stacks, *nxt) if nxt[0] < depth else None
        wg, wu, wd = ffn_weights.pop()
        x, cast = _ffn(x, _row(norm_g[layer, 2 * i]), wg, wu, wd, cast_next=cast_next, **mix)
        ffn_weights.append(cast)
        return x

    k_shared = v_shared = None
    for layer in range(depth):
        x = ffn(x, layer, 0)
        g_mix = _row(norm_g[layer, 1])
        if layer < n_self:
            proj, kt = _ret_proj(x, g_mix, ret_w_in[layer].astype(BF16), rcos, rsin)
            mix = dict(ret_mix=(_retention(proj, kt, batch, seq), proj, _row(ret_gn_g[layer]),
                                ret_w_o[layer].astype(BF16)))
        else:
            j = layer - n_self
            wuq = mla_w_uq[j].reshape(Q_LORA, HEADS, MLA_NOPE + MLA_ROPE)
            wuq = jnp.concatenate([wuq[:, :, :MLA_NOPE].reshape(Q_LORA, Q_NOPE_COLS),
                                   wuq[:, :, MLA_NOPE:].reshape(Q_LORA, HEADS * MLA_ROPE)], axis=1)
            q = _mla_q(x, g_mix, mla_w_dq[j].astype(BF16), _row(mla_q_lora_norm_g[j]),
                       wuq.astype(BF16), _row(mla_q_nope_norm_g[j], 2), _row(mla_q_rope_norm_g[j], 4),
                       mcos, msa, msb)
            mix = dict(mix=(_flash(q, k_shared, v_shared, batch, seq), mla_w_o[j].astype(BF16)))
        x = ffn(x, layer, 1, **mix)
        if layer == n_self - 1:
            wup = kv_w_up.reshape(KV_LORA, HEADS, MLA_NOPE + MLA_V)
            wk = wup[:, :, :MLA_NOPE].reshape(KV_LORA, HEADS * MLA_NOPE).astype(BF16)
            wv = wup[:, :, MLA_NOPE:].reshape(KV_LORA, HEADS * MLA_V).astype(BF16)
            wdr = jnp.tile(kv_w_down[:, KV_LORA:], (1, 2)).astype(BF16)
            k_shared, v_shared = _shared_kv(
                x, _row(kv_norm_g), kv_w_down[:, :KV_LORA].astype(BF16), wdr,
                _row(kv_latent_norm_g), wk, wv, _row(k_nope_norm_g, 2), _row(k_rope_norm_g, 2),
                mcos, msa, msb)
    return x.reshape(batch, seq, d)
```

```python
import math

import jax
import jax.numpy as jnp
from jax import lax
from jax.experimental import pallas as pl
from jax.experimental.pallas import tpu as pltpu

F32 = jnp.float32
BF16 = jnp.bfloat16

LANES = 128

D_MODEL = 1024
D_FF = 2816
HEADS = 8
RET_DK = 128
RET_DV = 256
RET_QK = HEADS * RET_DK
RET_PROJ = 2 * RET_QK + 2 * HEADS * RET_DV
RET_V0 = HEADS * RET_DV
RET_Q0 = 2 * HEADS * RET_DV
MLA_NOPE = 128
MLA_ROPE = 64
MLA_V = 128
MLA_QK_PAD = 2 * LANES
V_PAD = 2 * LANES
HEADS_PER_STEP = 2
Q_LORA = 384
KV_LORA = 256
ROPE_BASE = 10000.0
EPS = 1e-6

VMEM_LIMIT = 56 * 1024 * 1024

TM = 512
TM_FFN = 1024
TM_MLA = 1024
FF_CHUNK = 256
RET_CHUNK = 256
RET_HEADS_PER_STEP = 2
TQ = 512
SCORES_AHEAD = 2
NEG_BIG = -1e30


def _dot(a, b):
    return jnp.dot(a, b, preferred_element_type=F32)


def _dot_nt(a, b):
    return lax.dot_general(a, b, (((1,), (1,)), ((), ())), preferred_element_type=F32)


def _rms(x, g, n):
    ms = jnp.sum(x * x, axis=-1, keepdims=True) * (1.0 / n)
    return x * lax.rsqrt(ms + EPS) * g


def _silu(x):
    return x / (1.0 + jnp.exp2(x * -math.log2(math.e)))


def _lane_tile(x, n):
    return jnp.concatenate([x] * n, axis=1)


def _group_mean_matrix(width, group):
    shift = group.bit_length() - 1
    r = lax.broadcasted_iota(jnp.int32, (width, width), 0) >> shift
    c = lax.broadcasted_iota(jnp.int32, (width, width), 1) >> shift
    return jnp.where(r == c, 1.0 / group, 0.0).astype(BF16)


def _group_rms(x, mean_matrix, g):
    ms = _dot((x * x).astype(BF16), mean_matrix)
    return x * lax.rsqrt(ms + EPS) * g


def _params(*sem):
    return pltpu.CompilerParams(dimension_semantics=sem, vmem_limit_bytes=VMEM_LIMIT)


def _resident(shape):
    nd = len(shape)
    return pl.BlockSpec(shape, lambda *_: (0,) * nd, pipeline_mode=pl.Buffered(1))


def _rows(width, tm=TM):
    return pl.BlockSpec((tm, width), lambda i: (i, 0))


def _rope_tables_kernel(pos_ref, rcos_ref, rsin_ref, mcos_ref, msa_ref, msb_ref):
    pos = pos_ref[...].astype(F32)
    lane = lax.broadcasted_iota(jnp.int32, (1, LANES), 1)
    is_ret = lane < 64
    freq = jnp.where(is_ret, lane, lane & 31).astype(F32)
    step = jnp.where(is_ret, -2.0 / RET_DK, -2.0 / MLA_ROPE) * math.log(ROPE_BASE)
    ang = pos * jnp.exp(freq * step)
    c = jnp.cos(ang)
    s = jnp.sin(ang)
    rcos_ref[...] = jnp.where(is_ret, c, pltpu.roll(c, 64, 1))
    rsin_ref[...] = jnp.where(is_ret, -s, pltpu.roll(s, 64, 1))

    def spread(t):
        quarter = lane >> 5
        return jnp.where(quarter == 0, pltpu.roll(t, 64, 1),
                         jnp.where(quarter == 1, pltpu.roll(t, 96, 1),
                                   jnp.where(quarter == 2, t, pltpu.roll(t, 32, 1))))

    s = spread(s)
    first_half = (lane & 63) < 32
    mcos_ref[...] = spread(c)
    msa_ref[...] = jnp.where(first_half, -s, 0.0)
    msb_ref[...] = jnp.where(first_half, 0.0, s)


def _rope_tables(pos):
    m = pos.shape[0]
    tm = 1024
    tab = jax.ShapeDtypeStruct((m, LANES), F32)
    spec = pl.BlockSpec((tm, LANES), lambda i: (i, 0))
    return pl.pallas_call(
        _rope_tables_kernel,
        out_shape=(tab,) * 5,
        grid=(m // tm,),
        in_specs=[pl.BlockSpec((tm, 1), lambda i: (i, 0))],
        out_specs=(spec,) * 5,
        compiler_params=_params("parallel"),
        name="rope_tables",
    )(pos)


def _rope128(x, cos, sin):
    return x * cos + pltpu.roll(x, 64, 1) * sin


def _rope64x2(x, cos, sa, sb):
    return x * cos + pltpu.roll(x, 96, 1) * sa + pltpu.roll(x, 32, 1) * sb


def _ffn_body(x, g_ref, wg_ref, wu_ref, wd_ref, o_ref):
    h = _rms(x, g_ref[...], D_MODEL).astype(BF16)
    hidden = []
    for c in range(D_FF // FF_CHUNK):
        sl = slice(c * FF_CHUNK, (c + 1) * FF_CHUNK)
        hidden.append((_silu(_dot(h, wg_ref[:, sl])) * _dot(h, wu_ref[:, sl])).astype(BF16))
    o_ref[...] = x + 0.5 * _dot(jnp.concatenate(hidden, axis=1), wd_ref[...])


def _ffn_kernel(x_ref, g_ref, wg_ref, wu_ref, wd_ref, o_ref):
    _ffn_body(x_ref[...], g_ref, wg_ref, wu_ref, wd_ref, o_ref)


def _mix_ffn_kernel(x_ref, a_ref, wo_ref, g_ref, wg_ref, wu_ref, wd_ref, o_ref):
    _ffn_body(x_ref[...] + _dot(a_ref[...], wo_ref[...]), g_ref, wg_ref, wu_ref, wd_ref, o_ref)


def _ret_mix_ffn_kernel(x_ref, ret_ref, gate_ref, gn_ref, wo_ref, g_ref, wg_ref, wu_ref, wd_ref,
                        o_ref):
    x = x_ref[...]
    for hd in range(HEADS):
        cols = slice(hd * RET_DV, (hd + 1) * RET_DV)
        o = ret_ref[:, cols].astype(F32)
        oc = o - jnp.mean(o, axis=-1, keepdims=True)
        var = jnp.mean(oc * oc, axis=-1, keepdims=True)
        on = oc * lax.rsqrt(var + EPS) * gn_ref[:, cols]
        a = (_silu(gate_ref[:, cols].astype(F32)) * on).astype(BF16)
        x = x + _dot(a, wo_ref[cols, :])
    _ffn_body(x, g_ref, wg_ref, wu_ref, wd_ref, o_ref)


def _casting_next_weights(body, n_in):
    def kernel(*refs):
        for src, dst in zip(refs[n_in:n_in + 3], refs[n_in + 4:]):
            dst[...] = src[...].astype(BF16)
        body(*refs[:n_in], refs[n_in + 3])
    return kernel


def _ffn(x, g, wg, wu, wd, mix=None, ret_mix=None, cast_next=None):
    m = x.shape[0]
    w_specs = [_resident((1, D_MODEL)), _resident((D_MODEL, D_FF)), _resident((D_MODEL, D_FF)),
               _resident((D_FF, D_MODEL))]
    if ret_mix is not None:
        o, proj, gn, wo = ret_mix
        tm = TM
        name, body, args = "ret_mix_ffn", _ret_mix_ffn_kernel, (x, o, proj, gn, wo)
        specs = [_rows(D_MODEL, tm), _rows(HEADS * RET_DV, tm), _rows(HEADS * RET_DV, tm),
                 _resident(gn.shape), _resident(wo.shape)]
    elif mix is not None:
        a, wo = mix
        tm = TM_FFN
        name, body, args = "mix_ffn", _mix_ffn_kernel, (x, a, wo)
        specs = [_rows(D_MODEL, tm), _rows(a.shape[1], tm), _resident(wo.shape)]
    else:
        tm = TM_FFN
        name, body, args, specs = "ffn", _ffn_kernel, (x,), [_rows(D_MODEL, tm)]
    steps = m // tm
    args = (*args, g, wg, wu, wd)
    in_specs = specs + w_specs
    out_shape = jax.ShapeDtypeStruct((m, D_MODEL), F32)
    out_specs = _rows(D_MODEL, tm)
    if cast_next is not None:
        *stacks, layer, half = cast_next
        body = _casting_next_weights(body, len(args))
        args = (*args, *stacks)
        slab = D_MODEL // steps
        last = D_FF // FF_CHUNK - 1
        assert steps > last
        in_specs = in_specs + [
            pl.BlockSpec((None, None, slab, D_FF), lambda i: (layer, half, i, 0)),
            pl.BlockSpec((None, None, slab, D_FF), lambda i: (layer, half, i, 0)),
            pl.BlockSpec((None, None, FF_CHUNK, D_MODEL),
                         lambda i: (layer, half, jnp.minimum(i, last), 0))]
        out_shape = (out_shape, jax.ShapeDtypeStruct((D_MODEL, D_FF), BF16),
                     jax.ShapeDtypeStruct((D_MODEL, D_FF), BF16),
                     jax.ShapeDtypeStruct((D_FF, D_MODEL), BF16))
        out_specs = (out_specs, pl.BlockSpec((slab, D_FF), lambda i: (i, 0)),
                     pl.BlockSpec((slab, D_FF), lambda i: (i, 0)),
                     pl.BlockSpec((FF_CHUNK, D_MODEL), lambda i: (jnp.minimum(i, last), 0)))
    out = pl.pallas_call(
        body,
        out_shape=out_shape,
        grid=(steps,),
        in_specs=in_specs,
        out_specs=out_specs,
        compiler_params=_params("arbitrary"),
        name=name,
    )(*args)
    if cast_next is None:
        return out, None
    return out[0], out[1:]


def _ret_proj_kernel(x_ref, g_ref, w_ref, cos_ref, sin_ref, o_ref, kt_ref):
    h = _rms(x_ref[...], g_ref[...], D_MODEL).astype(BF16)
    cos = cos_ref[...]
    sin = sin_ref[...]
    q = _dot(h, w_ref[:, :RET_QK])
    k = _dot(h, w_ref[:, RET_QK:2 * RET_QK])
    for hd in range(HEADS):
        head = slice(hd * RET_DK, (hd + 1) * RET_DK)
        o_ref[:, RET_Q0 + hd * RET_DK:RET_Q0 + (hd + 1) * RET_DK] = (
            _rope128(q[:, head], cos, sin).astype(BF16))
        kh = _rope128(k[:, head], cos, sin) * RET_DK ** -0.5
        for c in range(TM // RET_CHUNK):
            kt_ref[c, head, :] = kh[c * RET_CHUNK:(c + 1) * RET_CHUNK, :].T.astype(BF16)
    v0 = 2 * RET_QK
    g0 = v0 + HEADS * RET_DV
    o_ref[:, RET_V0:RET_Q0] = _dot(h, w_ref[:, v0:g0]).astype(BF16)
    o_ref[:, :RET_V0] = _dot(h, w_ref[:, g0:]).astype(BF16)


def _ret_proj(x, g, w, cos, sin):
    m = x.shape[0]
    chunks = TM // RET_CHUNK
    return pl.pallas_call(
        _ret_proj_kernel,
        out_shape=(jax.ShapeDtypeStruct((m, RET_PROJ - RET_QK), BF16),
                   jax.ShapeDtypeStruct((m // RET_CHUNK, RET_QK, RET_CHUNK), BF16)),
        grid=(m // TM,),
        in_specs=[_rows(D_MODEL), _resident((1, D_MODEL)), _resident((D_MODEL, RET_PROJ)),
                  _rows(LANES), _rows(LANES)],
        out_specs=(_rows(RET_PROJ - RET_QK),
                   pl.BlockSpec((chunks, RET_QK, RET_CHUNK), lambda i: (i, 0, 0))),
        compiler_params=_params("parallel"),
        name="ret_proj",
    )(x, g, w, cos, sin)


def _retention_kernel(q_ref, kt_ref, v_ref, o_ref, state_ref):
    c = RET_CHUNK
    seq = q_ref.shape[0]

    def index(shape, axis):
        return lax.broadcasted_iota(jnp.int32, shape, axis).astype(F32)

    def decays(sub):
        hd = (pl.program_id(1) * RET_HEADS_PER_STEP + sub).astype(F32)

        def log_decay(shape):
            return jnp.log1p(-jnp.exp2(jnp.full(shape, -5.0, F32) - hd))

        diff = index((c, c), 0) - index((c, c), 1)
        d_intra = jnp.where(diff >= 0, jnp.exp(log_decay((c, c)) * jnp.maximum(diff, 0.0)), 0.0)
        q_decay = jnp.exp(log_decay((c, RET_DV)) * (index((c, RET_DV), 0) + 1.0))
        k_decay = jnp.exp(log_decay((RET_DK, c)) * (c - 1.0 - index((RET_DK, c), 1)))
        chunk_decay = jnp.exp(log_decay((RET_DK, RET_DV)) * c)
        return d_intra, q_decay, k_decay, chunk_decay

    per_head = [decays(sub) for sub in range(RET_HEADS_PER_STEP)]
    state_ref[...] = jnp.zeros_like(state_ref)

    def body(t, carry):
        off = pl.multiple_of(t * c, c)
        for sub, (d_intra, q_decay, k_decay, chunk_decay) in enumerate(per_head):
            qk = slice(sub * RET_DK, (sub + 1) * RET_DK)
            vo = slice(sub * RET_DV, (sub + 1) * RET_DV)
            q = q_ref[pl.ds(off, c), qk]
            kt = kt_ref[t, qk, :]
            v = v_ref[pl.ds(off, c), vo]
            state = state_ref[sub]
            s = _dot(q, kt) * d_intra
            o = _dot(s.astype(BF16), v) + _dot(q, state.astype(BF16)) * q_decay
            ktd = (kt.astype(F32) * k_decay).astype(BF16)
            state_ref[sub] = state * chunk_decay + _dot(ktd, v)
            o_ref[pl.ds(off, c), vo] = o.astype(BF16)
        return carry

    lax.fori_loop(0, seq // c, body, 0, unroll=True)


def _retention(proj, kt, batch, seq):
    m = proj.shape[0]
    g = RET_HEADS_PER_STEP
    qblk = RET_Q0 // (g * RET_DK)
    vblk = RET_V0 // (g * RET_DV)
    return pl.pallas_call(
        _retention_kernel,
        out_shape=jax.ShapeDtypeStruct((m, HEADS * RET_DV), BF16),
        grid=(batch, HEADS // g),
        in_specs=[
            pl.BlockSpec((seq, g * RET_DK), lambda b, h: (b, qblk + h)),
            pl.BlockSpec((seq // RET_CHUNK, g * RET_DK, RET_CHUNK), lambda b, h: (b, h, 0)),
            pl.BlockSpec((seq, g * RET_DV), lambda b, h: (b, vblk + h)),
        ],
        out_specs=pl.BlockSpec((seq, g * RET_DV), lambda b, h: (b, h)),
        scratch_shapes=[pltpu.VMEM((g, RET_DK, RET_DV), F32)],
        compiler_params=_params("parallel", "parallel"),
        name="retention",
    )(proj, kt, proj)


def _rope_tile_for_head(tile, hd):
    lane = lax.broadcasted_iota(jnp.int32, tile.shape, 1)
    keep = (lane < MLA_ROPE) if hd % 2 == 0 else (lane >= MLA_ROPE)
    return jnp.where(keep, tile, jnp.zeros_like(tile))


def _kv_kernel(x_ref, g_ref, wdc_ref, wdr_ref, lat_g_ref, wk_ref, wv_ref, kn_g_ref, kr_g_ref,
               cos_ref, sa_ref, sb_ref, k_out, v_out):
    h = _rms(x_ref[...], g_ref[...], D_MODEL).astype(BF16)
    lat = _rms(_dot(h, wdc_ref[...]), lat_g_ref[...], KV_LORA).astype(BF16)
    pe = _rms(_dot(h, wdr_ref[...]), kr_g_ref[...], LANES)
    pe = _rope64x2(pe, cos_ref[...], sa_ref[...], sb_ref[...]).astype(BF16)
    pe_tiles = (_rope_tile_for_head(pe, 0), _rope_tile_for_head(pe, 1))
    ones = jnp.ones((pe.shape[0], V_PAD - MLA_V), BF16)
    mean_nope = _group_mean_matrix(2 * MLA_NOPE, MLA_NOPE)
    kn_g = kn_g_ref[...]
    for pair in range(HEADS // 2):
        lo = pair * 2 * MLA_NOPE
        kn = _group_rms(_dot(lat, wk_ref[:, lo:lo + 2 * MLA_NOPE]), mean_nope, kn_g)
        kn = kn.astype(BF16)
        vv = _dot(lat, wv_ref[:, pair * 2 * MLA_V:(pair + 1) * 2 * MLA_V]).astype(BF16)
        for sub in range(2):
            hd = 2 * pair + sub
            k_out[:, hd * MLA_QK_PAD:hd * MLA_QK_PAD + MLA_NOPE] = (
                kn[:, sub * MLA_NOPE:(sub + 1) * MLA_NOPE])
            k_out[:, hd * MLA_QK_PAD + MLA_NOPE:(hd + 1) * MLA_QK_PAD] = pe_tiles[sub]
            v_out[:, hd * V_PAD:hd * V_PAD + MLA_V] = vv[:, sub * MLA_V:(sub + 1) * MLA_V]
            v_out[:, hd * V_PAD + MLA_V:(hd + 1) * V_PAD] = ones


def _shared_kv(x, g, wdc, wdr, lat_g, wk, wv, kn_g, kr_g, cos, sa, sb):
    m = x.shape[0]
    return pl.pallas_call(
        _kv_kernel,
        out_shape=(jax.ShapeDtypeStruct((m, HEADS * MLA_QK_PAD), BF16),
                   jax.ShapeDtypeStruct((m, HEADS * V_PAD), BF16)),
        grid=(m // TM_MLA,),
        in_specs=[_rows(D_MODEL, TM_MLA), _resident((1, D_MODEL)), _resident((D_MODEL, KV_LORA)),
                  _resident((D_MODEL, LANES)), _resident((1, KV_LORA)),
                  _resident((KV_LORA, HEADS * MLA_NOPE)), _resident((KV_LORA, HEADS * MLA_V)),
                  _resident((1, 2 * MLA_NOPE)), _resident((1, LANES)),
                  _rows(LANES, TM_MLA), _rows(LANES, TM_MLA), _rows(LANES, TM_MLA)],
        out_specs=(_rows(HEADS * MLA_QK_PAD, TM_MLA), _rows(HEADS * V_PAD, TM_MLA)),
        compiler_params=_params("parallel"),
        name="shared_kv",
    )(x, g, wdc, wdr, lat_g, wk, wv, kn_g, kr_g, cos, sa, sb)


Q_SCALE = (MLA_NOPE + MLA_ROPE) ** -0.5 * math.log2(math.e)
Q_NOPE_COLS = HEADS * MLA_NOPE


def _q_kernel(x_ref, g_ref, wdq_ref, lora_g_ref, wuq_ref, qn_g_ref, qr_g_ref,
              cos_ref, sa_ref, sb_ref, q_out):
    h = _rms(x_ref[...], g_ref[...], D_MODEL).astype(BF16)
    cq = _rms(_dot(h, wdq_ref[...]), lora_g_ref[...], Q_LORA).astype(BF16)
    q = _dot(cq, wuq_ref[...])
    mean_nope = _group_mean_matrix(2 * MLA_NOPE, MLA_NOPE)
    mean_rope = _group_mean_matrix(2 * LANES, MLA_ROPE)
    qn_g = qn_g_ref[...] * Q_SCALE
    qr_g = qr_g_ref[...] * Q_SCALE
    cos = cos_ref[...]
    sa = sa_ref[...]
    sb = sb_ref[...]
    for quad in range(HEADS // 4):
        lo = Q_NOPE_COLS + quad * 2 * LANES
        qp = _group_rms(q[:, lo:lo + 2 * LANES], mean_rope, qr_g)
        for half in range(2):
            pair = 2 * quad + half
            lo = pair * 2 * MLA_NOPE
            qn = _group_rms(q[:, lo:lo + 2 * MLA_NOPE], mean_nope, qn_g).astype(BF16)
            tile = _rope64x2(qp[:, half * LANES:(half + 1) * LANES], cos, sa, sb).astype(BF16)
            for sub in range(2):
                hd = 2 * pair + sub
                q_out[:, hd * MLA_QK_PAD:hd * MLA_QK_PAD + MLA_NOPE] = (
                    qn[:, sub * MLA_NOPE:(sub + 1) * MLA_NOPE])
                q_out[:, hd * MLA_QK_PAD + MLA_NOPE:(hd + 1) * MLA_QK_PAD] = (
                    _rope_tile_for_head(tile, sub))


def _mla_q(x, g, wdq, lora_g, wuq, qn_g, qr_g, cos, sa, sb):
    m = x.shape[0]
    return pl.pallas_call(
        _q_kernel,
        out_shape=jax.ShapeDtypeStruct((m, HEADS * MLA_QK_PAD), BF16),
        grid=(m // TM_MLA,),
        in_specs=[_rows(D_MODEL, TM_MLA), _resident((1, D_MODEL)), _resident((D_MODEL, Q_LORA)),
                  _resident((1, Q_LORA)), _resident(wuq.shape),
                  _resident((1, 2 * MLA_NOPE)), _resident((1, 2 * LANES)),
                  _rows(LANES, TM_MLA), _rows(LANES, TM_MLA), _rows(LANES, TM_MLA)],
        out_specs=_rows(HEADS * MLA_QK_PAD, TM_MLA),
        compiler_params=_params("parallel"),
        name="mla_q",
    )(x, g, wdq, lora_g, wuq, qn_g, qr_g, cos, sa, sb)


def _flash_kernel(q_ref, k_ref, v_ref, o_ref, m_ref, acc_ref):
    tiles = q_ref.shape[0] // TQ
    i = pl.program_id(2)
    pair = (i, tiles - 1 - i)
    m_ref[...] = jnp.full_like(m_ref, NEG_BIG)
    acc_ref[...] = jnp.zeros_like(acc_ref)

    def item(k):
        if k < tiles - 1:
            second = k >= i
            return (jnp.where(second, 1, 0), jnp.where(second, pair[1], pair[0]),
                    jnp.where(second, k - i, k), False)
        which = k - (tiles - 1)
        return which, pair[which], pair[which], True

    def scores(k):
        _, qt, kt, _ = item(k)
        rows = pl.ds(pl.multiple_of(qt * TQ, TQ), TQ)
        keys = pl.ds(pl.multiple_of(kt * TQ, TQ), TQ)
        return [_dot_nt(q_ref[rows, hd * MLA_QK_PAD:(hd + 1) * MLA_QK_PAD],
                        k_ref[keys, hd * MLA_QK_PAD:(hd + 1) * MLA_QK_PAD])
                for hd in range(HEADS_PER_STEP)]

    def consume(k, s_heads):
        which, _, kt, diagonal = item(k)
        keys = pl.ds(pl.multiple_of(kt * TQ, TQ), TQ)
        for hd, s in enumerate(s_heads):
            if diagonal:
                row = lax.broadcasted_iota(jnp.int32, (TQ, TQ), 0)
                col = lax.broadcasted_iota(jnp.int32, (TQ, TQ), 1)
                s = jnp.where(col <= row, s, NEG_BIG)
            m_prev = m_ref[which, hd]
            m_new = jnp.maximum(m_prev, jnp.max(s, axis=1, keepdims=True))
            alpha = jnp.exp2(m_prev - m_new)
            p = jnp.exp2(s - _lane_tile(m_new, TQ // LANES))
            pv = _dot(p.astype(BF16), v_ref[keys, hd * V_PAD:(hd + 1) * V_PAD])
            acc_ref[which, hd] = _lane_tile(alpha, V_PAD // LANES) * acc_ref[which, hd] + pv
            m_ref[which, hd] = m_new

    n_items = tiles + 1
    pending = [scores(k) for k in range(SCORES_AHEAD)]
    for k in range(n_items):
        if k + SCORES_AHEAD < n_items:
            pending.append(scores(k + SCORES_AHEAD))
        consume(k, pending.pop(0))

    for which in range(2):
        rows = pl.ds(pl.multiple_of(pair[which] * TQ, TQ), TQ)
        for hd in range(HEADS_PER_STEP):
            acc = acc_ref[which, hd]
            o_ref[rows, hd * MLA_V:(hd + 1) * MLA_V] = (acc[:, :MLA_V] / acc[:, MLA_V:]).astype(BF16)


def _flash(q, k, v, batch, seq):
    m = q.shape[0]
    g = HEADS_PER_STEP
    tiles = seq // TQ
    assert tiles % 2 == 0, "query tiles are processed in (i, tiles-1-i) pairs"
    return pl.pallas_call(
        _flash_kernel,
        out_shape=jax.ShapeDtypeStruct((m, HEADS * MLA_V), BF16),
        grid=(batch, HEADS // g, tiles // 2),
        in_specs=[
            pl.BlockSpec((seq, g * MLA_QK_PAD), lambda b, h, i: (b, h)),
            pl.BlockSpec((seq, g * MLA_QK_PAD), lambda b, h, i: (b, h)),
            pl.BlockSpec((seq, g * V_PAD), lambda b, h, i: (b, h)),
        ],
        out_specs=pl.BlockSpec((seq, g * MLA_V), lambda b, h, i: (b, h)),
        scratch_shapes=[pltpu.VMEM((2, g, TQ, LANES), F32),
                        pltpu.VMEM((2, g, TQ, V_PAD), F32)],
        compiler_params=_params("parallel", "parallel", "arbitrary"),
        name="flash_attention",
    )(q, k, v)


def _row(g, repeat=1):
    return jnp.tile(g.reshape(1, -1).astype(F32), (1, repeat))


def kernel(x, positions, norm_g, ffn_w_gate, ffn_w_up, ffn_w_down, ret_w_in, ret_gn_g, ret_w_o,
           kv_norm_g, kv_w_down, kv_latent_norm_g, kv_w_up, k_nope_norm_g, k_rope_norm_g,
           mla_w_dq, mla_q_lora_norm_g, mla_w_uq, mla_q_nope_norm_g, mla_q_rope_norm_g, mla_w_o):
    batch, seq, d = x.shape
    depth = norm_g.shape[0]
    n_self = ret_w_in.shape[0]
    m = batch * seq
    x = x.reshape(m, d)
    rcos, rsin, mcos, msa, msb = _rope_tables(positions.reshape(m, 1))
    stacks = (ffn_w_gate, ffn_w_up, ffn_w_down)
    ffn_weights = [tuple(w[0, 0].astype(BF16) for w in stacks)]

    def ffn(x, layer, i, **mix):
        nxt = (layer, i + 1) if i == 0 else (layer + 1, 0)
        cast_next = (*stacks, *nxt) if nxt[0] < depth else None
        wg, wu, wd = ffn_weights.pop()
        x, cast = _ffn(x, _row(norm_g[layer, 2 * i]), wg, wu, wd, cast_next=cast_next, **mix)
        ffn_weights.append(cast)
        return x

    k_shared = v_shared = None
    for layer in range(depth):
        x = ffn(x, layer, 0)
        g_mix = _row(norm_g[layer, 1])
        if layer < n_self:
            proj, kt = _ret_proj(x, g_mix, ret_w_in[layer].astype(BF16), rcos, rsin)
            mix = dict(ret_mix=(_retention(proj, kt, batch, seq), proj, _row(ret_gn_g[layer]),
                                ret_w_o[layer].astype(BF16)))
        else:
            j = layer - n_self
            wuq = mla_w_uq[j].reshape(Q_LORA, HEADS, MLA_NOPE + MLA_ROPE)
            wuq = jnp.concatenate([wuq[:, :, :MLA_NOPE].reshape(Q_LORA, Q_NOPE_COLS),
                                   wuq[:, :, MLA_NOPE:].reshape(Q_LORA, HEADS * MLA_ROPE)], axis=1)
            q = _mla_q(x, g_mix, mla_w_dq[j].astype(BF16), _row(mla_q_lora_norm_g[j]),
                       wuq.astype(BF16), _row(mla_q_nope_norm_g[j], 2), _row(mla_q_rope_norm_g[j], 4),
                       mcos, msa, msb)
            mix = dict(mix=(_flash(q, k_shared, v_shared, batch, seq), mla_w_o[j].astype(BF16)))
        x = ffn(x, layer, 1, **mix)
        if layer == n_self - 1:
            wup = kv_w_up.reshape(KV_LORA, HEADS, MLA_NOPE + MLA_V)
            wk = wup[:, :, :MLA_NOPE].reshape(KV_LORA, HEADS * MLA_NOPE).astype(BF16)
            wv = wup[:, :, MLA_NOPE:].reshape(KV_LORA, HEADS * MLA_V).astype(BF16)
            wdr = jnp.tile(kv_w_down[:, KV_LORA:], (1, 2)).astype(BF16)
            k_shared, v_shared = _shared_kv(
                x, _row(kv_norm_g), kv_w_down[:, :KV_LORA].astype(BF16), wdr,
                _row(kv_latent_norm_g), wk, wv, _row(k_nope_norm_g, 2), _row(k_rope_norm_g, 2),
                mcos, msa, msb)
    return x.reshape(batch, seq, d)
```

```python
import math

import jax
import jax.numpy as jnp
from jax import lax
from jax.experimental import pallas as pl
from jax.experimental.pallas import tpu as pltpu

F32 = jnp.float32
BF16 = jnp.bfloat16

LANES = 128
BF16_ROWS = 16

D_MODEL = 1024
D_FF = 2816
HEADS = 8
RET_DK = 128
RET_DV = 256
RET_QK = HEADS * RET_DK
RET_PROJ = 2 * RET_QK + 2 * HEADS * RET_DV
RET_V0 = HEADS * RET_DV
RET_Q0 = 2 * HEADS * RET_DV
MLA_NOPE = 128
MLA_ROPE = 64
MLA_V = 128
MLA_QK_PAD = 2 * LANES
V_PAD = 2 * LANES
HEADS_PER_STEP = 2
Q_LORA = 384
KV_LORA = 256
ROPE_BASE = 10000.0
EPS = 1e-6

VMEM_LIMIT = 56 * 1024 * 1024

TM = 512
TM_FFN = 1024
TM_MLA = 1024
FF_CHUNK = 256
RET_CHUNK = 256
RET_HEADS_PER_STEP = 2
TQ = 512
SCORES_AHEAD = 2
NEG_BIG = -1e30


def _dot(a, b):
    return jnp.dot(a, b, preferred_element_type=F32)


def _dot_nt(a, b):
    return lax.dot_general(a, b, (((1,), (1,)), ((), ())), preferred_element_type=F32)


def _rms(x, g, n):
    ms = jnp.sum(x * x, axis=-1, keepdims=True) * (1.0 / n)
    return x * lax.rsqrt(ms + EPS) * g


def _silu(x):
    return x / (1.0 + jnp.exp2(x * -math.log2(math.e)))


def _lane_tile(x, n):
    return jnp.concatenate([x] * n, axis=1)


def _group_mean_matrix(width, group):
    shift = group.bit_length() - 1
    r = lax.broadcasted_iota(jnp.int32, (width, width), 0) >> shift
    c = lax.broadcasted_iota(jnp.int32, (width, width), 1) >> shift
    return jnp.where(r == c, 1.0 / group, 0.0).astype(BF16)


def _group_rms(x, mean_matrix, g):
    ms = _dot((x * x).astype(BF16), mean_matrix)
    return x * lax.rsqrt(ms + EPS) * g


def _params(*sem):
    return pltpu.CompilerParams(dimension_semantics=sem, vmem_limit_bytes=VMEM_LIMIT)


def _resident(shape):
    nd = len(shape)
    return pl.BlockSpec(shape, lambda *_: (0,) * nd, pipeline_mode=pl.Buffered(1))


def _rows(width, tm=TM):
    return pl.BlockSpec((tm, width), lambda i: (i, 0))


class _RideAlongCasts:
    def __init__(self, casts, steps):
        self.sources = tuple(src for src, _ in casts)
        self.in_specs, self.out_shapes, self.out_specs = [], [], []
        for src, prefix in casts:
            rows, cols = src.shape[-2:]
            slab = next(s for s in range(BF16_ROWS, rows + 1, BF16_ROWS)
                        if rows % s == 0 and rows // s <= steps)
            last = rows // slab - 1
            lead = (None,) * len(prefix)
            self.in_specs.append(pl.BlockSpec(
                (*lead, slab, cols),
                lambda i, prefix=prefix, last=last: (*prefix, jnp.minimum(i, last), 0)))
            self.out_shapes.append(jax.ShapeDtypeStruct((rows, cols), BF16))
            self.out_specs.append(pl.BlockSpec(
                (slab, cols), lambda i, last=last: (jnp.minimum(i, last), 0)))

    def wrap(self, body, n_in, n_out):
        k = len(self.sources)

        def kernel(*refs):
            ins, refs = refs[:n_in], refs[n_in:]
            srcs, refs = refs[:k], refs[k:]
            outs, refs = refs[:n_out], refs[n_out:]
            for src, dst in zip(srcs, refs[:k]):
                dst[...] = src[...].astype(BF16)
            body(*ins, *outs, *refs[k:])
        return kernel


def _rope_tables_kernel(pos_ref, rcos_ref, rsin_ref, mcos_ref, msa_ref, msb_ref):
    pos = pos_ref[...].astype(F32)
    lane = lax.broadcasted_iota(jnp.int32, (1, LANES), 1)
    is_ret = lane < 64
    freq = jnp.where(is_ret, lane, lane & 31).astype(F32)
    step = jnp.where(is_ret, -2.0 / RET_DK, -2.0 / MLA_ROPE) * math.log(ROPE_BASE)
    ang = pos * jnp.exp(freq * step)
    c = jnp.cos(ang)
    s = jnp.sin(ang)
    rcos_ref[...] = jnp.where(is_ret, c, pltpu.roll(c, 64, 1))
    rsin_ref[...] = jnp.where(is_ret, -s, pltpu.roll(s, 64, 1))

    def spread(t):
        quarter = lane >> 5
        return jnp.where(quarter == 0, pltpu.roll(t, 64, 1),
                         jnp.where(quarter == 1, pltpu.roll(t, 96, 1),
                                   jnp.where(quarter == 2, t, pltpu.roll(t, 32, 1))))

    s = spread(s)
    first_half = (lane & 63) < 32
    mcos_ref[...] = spread(c)
    msa_ref[...] = jnp.where(first_half, -s, 0.0)
    msb_ref[...] = jnp.where(first_half, 0.0, s)


def _rope_tables(pos, casts):
    m = pos.shape[0]
    tm = 1024
    tab = jax.ShapeDtypeStruct((m, LANES), F32)
    spec = pl.BlockSpec((tm, LANES), lambda i: (i, 0))
    riders = _RideAlongCasts(casts, m // tm)
    out = pl.pallas_call(
        riders.wrap(_rope_tables_kernel, n_in=1, n_out=5),
        out_shape=[tab] * 5 + riders.out_shapes,
        grid=(m // tm,),
        in_specs=[pl.BlockSpec((tm, 1), lambda i: (i, 0)), *riders.in_specs],
        out_specs=[spec] * 5 + riders.out_specs,
        compiler_params=_params("arbitrary"),
        name="rope_tables",
    )(pos, *riders.sources)
    return out[:5], out[5:]


def _rope128(x, cos, sin):
    return x * cos + pltpu.roll(x, 64, 1) * sin


def _rope64x2(x, cos, sa, sb):
    return x * cos + pltpu.roll(x, 96, 1) * sa + pltpu.roll(x, 32, 1) * sb


def _ffn_body(x, g_ref, wg_ref, wu_ref, wd_ref, o_ref):
    h = _rms(x, g_ref[...], D_MODEL).astype(BF16)
    hidden = []
    for c in range(D_FF // FF_CHUNK):
        sl = slice(c * FF_CHUNK, (c + 1) * FF_CHUNK)
        hidden.append((_silu(_dot(h, wg_ref[:, sl])) * _dot(h, wu_ref[:, sl])).astype(BF16))
    o_ref[...] = x + 0.5 * _dot(jnp.concatenate(hidden, axis=1), wd_ref[...])


def _ffn_kernel(x_ref, g_ref, wg_ref, wu_ref, wd_ref, o_ref):
    _ffn_body(x_ref[...], g_ref, wg_ref, wu_ref, wd_ref, o_ref)


def _mix_ffn_kernel(x_ref, a_ref, wo_ref, g_ref, wg_ref, wu_ref, wd_ref, o_ref):
    _ffn_body(x_ref[...] + _dot(a_ref[...], wo_ref[...]), g_ref, wg_ref, wu_ref, wd_ref, o_ref)


def _ret_mix_ffn_kernel(x_ref, ret_ref, gate_ref, gn_ref, wo_ref, g_ref, wg_ref, wu_ref, wd_ref,
                        o_ref):
    x = x_ref[...]
    for hd in range(HEADS):
        cols = slice(hd * RET_DV, (hd + 1) * RET_DV)
        o = ret_ref[:, cols].astype(F32)
        oc = o - jnp.mean(o, axis=-1, keepdims=True)
        var = jnp.mean(oc * oc, axis=-1, keepdims=True)
        on = oc * lax.rsqrt(var + EPS) * gn_ref[:, cols]
        a = (_silu(gate_ref[:, cols].astype(F32)) * on).astype(BF16)
        x = x + _dot(a, wo_ref[cols, :])
    _ffn_body(x, g_ref, wg_ref, wu_ref, wd_ref, o_ref)


def _ffn(x, g, wg, wu, wd, mix=None, ret_mix=None, casts=()):
    m = x.shape[0]
    w_specs = [_resident((1, D_MODEL)), _resident((D_MODEL, D_FF)), _resident((D_MODEL, D_FF)),
               _resident((D_FF, D_MODEL))]
    if ret_mix is not None:
        o, proj, gn, wo = ret_mix
        tm = TM
        name, body, args = "ret_mix_ffn", _ret_mix_ffn_kernel, (x, o, proj, gn, wo)
        specs = [_rows(D_MODEL, tm), _rows(HEADS * RET_DV, tm), _rows(HEADS * RET_DV, tm),
                 _resident(gn.shape), _resident(wo.shape)]
    elif mix is not None:
        a, wo = mix
        tm = TM_FFN
        name, body, args = "mix_ffn", _mix_ffn_kernel, (x, a, wo)
        specs = [_rows(D_MODEL, tm), _rows(a.shape[1], tm), _resident(wo.shape)]
    else:
        tm = TM_FFN
        name, body, args, specs = "ffn", _ffn_kernel, (x,), [_rows(D_MODEL, tm)]
    steps = m // tm
    args = (*args, g, wg, wu, wd)
    riders = _RideAlongCasts(casts, steps)
    out = pl.pallas_call(
        riders.wrap(body, n_in=len(args), n_out=1),
        out_shape=[jax.ShapeDtypeStruct((m, D_MODEL), F32)] + riders.out_shapes,
        grid=(steps,),
        in_specs=[*specs, *w_specs, *riders.in_specs],
        out_specs=[_rows(D_MODEL, tm)] + riders.out_specs,
        compiler_params=_params("arbitrary"),
        name=name,
    )(*args, *riders.sources)
    return out[0], out[1:]


def _ret_proj_kernel(x_ref, g_ref, w_ref, cos_ref, sin_ref, o_ref, kt_ref):
    h = _rms(x_ref[...], g_ref[...], D_MODEL).astype(BF16)
    cos = cos_ref[...]
    sin = sin_ref[...]
    q = _dot(h, w_ref[:, :RET_QK])
    k = _dot(h, w_ref[:, RET_QK:2 * RET_QK])
    for hd in range(HEADS):
        head = slice(hd * RET_DK, (hd + 1) * RET_DK)
        o_ref[:, RET_Q0 + hd * RET_DK:RET_Q0 + (hd + 1) * RET_DK] = (
            _rope128(q[:, head], cos, sin).astype(BF16))
        kh = _rope128(k[:, head], cos, sin) * RET_DK ** -0.5
        for c in range(TM // RET_CHUNK):
            kt_ref[c, head, :] = kh[c * RET_CHUNK:(c + 1) * RET_CHUNK, :].T.astype(BF16)
    v0 = 2 * RET_QK
    g0 = v0 + HEADS * RET_DV
    o_ref[:, RET_V0:RET_Q0] = _dot(h, w_ref[:, v0:g0]).astype(BF16)
    o_ref[:, :RET_V0] = _dot(h, w_ref[:, g0:]).astype(BF16)


def _ret_proj(x, g, w, cos, sin):
    m = x.shape[0]
    chunks = TM // RET_CHUNK
    return pl.pallas_call(
        _ret_proj_kernel,
        out_shape=(jax.ShapeDtypeStruct((m, RET_PROJ - RET_QK), BF16),
                   jax.ShapeDtypeStruct((m // RET_CHUNK, RET_QK, RET_CHUNK), BF16)),
        grid=(m // TM,),
        in_specs=[_rows(D_MODEL), _resident((1, D_MODEL)), _resident((D_MODEL, RET_PROJ)),
                  _rows(LANES), _rows(LANES)],
        out_specs=(_rows(RET_PROJ - RET_QK),
                   pl.BlockSpec((chunks, RET_QK, RET_CHUNK), lambda i: (i, 0, 0))),
        compiler_params=_params("parallel"),
        name="ret_proj",
    )(x, g, w, cos, sin)


def _retention_kernel(q_ref, kt_ref, v_ref, o_ref, state_ref):
    c = RET_CHUNK
    seq = q_ref.shape[0]

    def index(shape, axis):
        return lax.broadcasted_iota(jnp.int32, shape, axis).astype(F32)

    def decays(sub):
        hd = (pl.program_id(1) * RET_HEADS_PER_STEP + sub).astype(F32)

        def log_decay(shape):
            return jnp.log1p(-jnp.exp2(jnp.full(shape, -5.0, F32) - hd))

        diff = index((c, c), 0) - index((c, c), 1)
        d_intra = jnp.where(diff >= 0, jnp.exp(log_decay((c, c)) * jnp.maximum(diff, 0.0)), 0.0)
        q_decay = jnp.exp(log_decay((c, RET_DV)) * (index((c, RET_DV), 0) + 1.0))
        k_decay = jnp.exp(log_decay((RET_DK, c)) * (c - 1.0 - index((RET_DK, c), 1)))
        chunk_decay = jnp.exp(log_decay((RET_DK, RET_DV)) * c)
        return d_intra, q_decay, k_decay, chunk_decay

    per_head = [decays(sub) for sub in range(RET_HEADS_PER_STEP)]
    state_ref[...] = jnp.zeros_like(state_ref)

    def body(t, carry):
        off = pl.multiple_of(t * c, c)
        for sub, (d_intra, q_decay, k_decay, chunk_decay) in enumerate(per_head):
            qk = slice(sub * RET_DK, (sub + 1) * RET_DK)
            vo = slice(sub * RET_DV, (sub + 1) * RET_DV)
            q = q_ref[pl.ds(off, c), qk]
            kt = kt_ref[t, qk, :]
            v = v_ref[pl.ds(off, c), vo]
            state = state_ref[sub]
            s = _dot(q, kt) * d_intra
            o = _dot(s.astype(BF16), v) + _dot(q, state.astype(BF16)) * q_decay
            ktd = (kt.astype(F32) * k_decay).astype(BF16)
            state_ref[sub] = state * chunk_decay + _dot(ktd, v)
            o_ref[pl.ds(off, c), vo] = o.astype(BF16)
        return carry

    lax.fori_loop(0, seq // c, body, 0, unroll=8)


def _retention(proj, kt, batch, seq):
    m = proj.shape[0]
    g = RET_HEADS_PER_STEP
    qblk = RET_Q0 // (g * RET_DK)
    vblk = RET_V0 // (g * RET_DV)
    return pl.pallas_call(
        _retention_kernel,
        out_shape=jax.ShapeDtypeStruct((m, HEADS * RET_DV), BF16),
        grid=(batch, HEADS // g),
        in_specs=[
            pl.BlockSpec((seq, g * RET_DK), lambda b, h: (b, qblk + h)),
            pl.BlockSpec((seq // RET_CHUNK, g * RET_DK, RET_CHUNK), lambda b, h: (b, h, 0)),
            pl.BlockSpec((seq, g * RET_DV), lambda b, h: (b, vblk + h)),
        ],
        out_specs=pl.BlockSpec((seq, g * RET_DV), lambda b, h: (b, h)),
        scratch_shapes=[pltpu.VMEM((g, RET_DK, RET_DV), F32)],
        compiler_params=_params("parallel", "parallel"),
        name="retention",
    )(proj, kt, proj)


def _rope_tile_for_head(tile, hd):
    lane = lax.broadcasted_iota(jnp.int32, tile.shape, 1)
    keep = (lane < MLA_ROPE) if hd % 2 == 0 else (lane >= MLA_ROPE)
    return jnp.where(keep, tile, jnp.zeros_like(tile))


def _kv_kernel(x_ref, g_ref, wdc_ref, wdr_ref, lat_g_ref, wk_ref, wv_ref, kn_g_ref, kr_g_ref,
               cos_ref, sa_ref, sb_ref, k_out, v_out):
    h = _rms(x_ref[...], g_ref[...], D_MODEL).astype(BF16)
    lat = _rms(_dot(h, wdc_ref[...]), lat_g_ref[...], KV_LORA).astype(BF16)
    pe = _rms(_dot(h, wdr_ref[...]), kr_g_ref[...], LANES)
    pe = _rope64x2(pe, cos_ref[...], sa_ref[...], sb_ref[...]).astype(BF16)
    pe_tiles = (_rope_tile_for_head(pe, 0), _rope_tile_for_head(pe, 1))
    ones = jnp.ones((pe.shape[0], V_PAD - MLA_V), BF16)
    mean_nope = _group_mean_matrix(2 * MLA_NOPE, MLA_NOPE)
    kn_g = kn_g_ref[...]
    for pair in range(HEADS // 2):
        lo = pair * 2 * MLA_NOPE
        kn = _group_rms(_dot(lat, wk_ref[:, lo:lo + 2 * MLA_NOPE]), mean_nope, kn_g)
        kn = kn.astype(BF16)
        vv = _dot(lat, wv_ref[:, pair * 2 * MLA_V:(pair + 1) * 2 * MLA_V]).astype(BF16)
        for sub in range(2):
            hd = 2 * pair + sub
            k_out[:, hd * MLA_QK_PAD:hd * MLA_QK_PAD + MLA_NOPE] = (
                kn[:, sub * MLA_NOPE:(sub + 1) * MLA_NOPE])
            k_out[:, hd * MLA_QK_PAD + MLA_NOPE:(hd + 1) * MLA_QK_PAD] = pe_tiles[sub]
            v_out[:, hd * V_PAD:hd * V_PAD + MLA_V] = vv[:, sub * MLA_V:(sub + 1) * MLA_V]
            v_out[:, hd * V_PAD + MLA_V:(hd + 1) * V_PAD] = ones


def _shared_kv(x, g, wdc, wdr, lat_g, wk, wv, kn_g, kr_g, cos, sa, sb):
    m = x.shape[0]
    return pl.pallas_call(
        _kv_kernel,
        out_shape=(jax.ShapeDtypeStruct((m, HEADS * MLA_QK_PAD), BF16),
                   jax.ShapeDtypeStruct((m, HEADS * V_PAD), BF16)),
        grid=(m // TM_MLA,),
        in_specs=[_rows(D_MODEL, TM_MLA), _resident((1, D_MODEL)), _resident((D_MODEL, KV_LORA)),
                  _resident((D_MODEL, LANES)), _resident((1, KV_LORA)),
                  _resident((KV_LORA, HEADS * MLA_NOPE)), _resident((KV_LORA, HEADS * MLA_V)),
                  _resident((1, 2 * MLA_NOPE)), _resident((1, LANES)),
                  _rows(LANES, TM_MLA), _rows(LANES, TM_MLA), _rows(LANES, TM_MLA)],
        out_specs=(_rows(HEADS * MLA_QK_PAD, TM_MLA), _rows(HEADS * V_PAD, TM_MLA)),
        compiler_params=_params("parallel"),
        name="shared_kv",
    )(x, g, wdc, wdr, lat_g, wk, wv, kn_g, kr_g, cos, sa, sb)


Q_SCALE = (MLA_NOPE + MLA_ROPE) ** -0.5 * math.log2(math.e)
Q_NOPE_COLS = HEADS * MLA_NOPE


def _q_kernel(x_ref, g_ref, wdq_ref, lora_g_ref, wuq_ref, qn_g_ref, qr_g_ref,
              cos_ref, sa_ref, sb_ref, q_out):
    h = _rms(x_ref[...], g_ref[...], D_MODEL).astype(BF16)
    cq = _rms(_dot(h, wdq_ref[...]), lora_g_ref[...], Q_LORA).astype(BF16)
    q = _dot(cq, wuq_ref[...])
    mean_nope = _group_mean_matrix(2 * MLA_NOPE, MLA_NOPE)
    mean_rope = _group_mean_matrix(2 * LANES, MLA_ROPE)
    qn_g = qn_g_ref[...] * Q_SCALE
    qr_g = qr_g_ref[...] * Q_SCALE
    cos = cos_ref[...]
    sa = sa_ref[...]
    sb = sb_ref[...]
    for quad in range(HEADS // 4):
        lo = Q_NOPE_COLS + quad * 2 * LANES
        qp = _group_rms(q[:, lo:lo + 2 * LANES], mean_rope, qr_g)
        for half in range(2):
            pair = 2 * quad + half
            lo = pair * 2 * MLA_NOPE
            qn = _group_rms(q[:, lo:lo + 2 * MLA_NOPE], mean_nope, qn_g).astype(BF16)
            tile = _rope64x2(qp[:, half * LANES:(half + 1) * LANES], cos, sa, sb).astype(BF16)
            for sub in range(2):
                hd = 2 * pair + sub
                q_out[:, hd * MLA_QK_PAD:hd * MLA_QK_PAD + MLA_NOPE] = (
                    qn[:, sub * MLA_NOPE:(sub + 1) * MLA_NOPE])
                q_out[:, hd * MLA_QK_PAD + MLA_NOPE:(hd + 1) * MLA_QK_PAD] = (
                    _rope_tile_for_head(tile, sub))


def _mla_q(x, g, wdq, lora_g, wuq, qn_g, qr_g, cos, sa, sb):
    m = x.shape[0]
    return pl.pallas_call(
        _q_kernel,
        out_shape=jax.ShapeDtypeStruct((m, HEADS * MLA_QK_PAD), BF16),
        grid=(m // TM_MLA,),
        in_specs=[_rows(D_MODEL, TM_MLA), _resident((1, D_MODEL)), _resident((D_MODEL, Q_LORA)),
                  _resident((1, Q_LORA)), _resident(wuq.shape),
                  _resident((1, 2 * MLA_NOPE)), _resident((1, 2 * LANES)),
                  _rows(LANES, TM_MLA), _rows(LANES, TM_MLA), _rows(LANES, TM_MLA)],
        out_specs=_rows(HEADS * MLA_QK_PAD, TM_MLA),
        compiler_params=_params("parallel"),
        name="mla_q",
    )(x, g, wdq, lora_g, wuq, qn_g, qr_g, cos, sa, sb)


def _flash_kernel(q_ref, k_ref, v_ref, o_ref, m_ref, acc_ref):
    tiles = q_ref.shape[0] // TQ
    i = pl.program_id(2)
    pair = (i, tiles - 1 - i)
    m_ref[...] = jnp.full_like(m_ref, NEG_BIG)
    acc_ref[...] = jnp.zeros_like(acc_ref)

    def item(k):
        if k < tiles - 1:
            second = k >= i
            return (jnp.where(second, 1, 0), jnp.where(second, pair[1], pair[0]),
                    jnp.where(second, k - i, k), False)
        which = k - (tiles - 1)
        return which, pair[which], pair[which], True

    def scores(k):
        _, qt, kt, _ = item(k)
        rows = pl.ds(pl.multiple_of(qt * TQ, TQ), TQ)
        keys = pl.ds(pl.multiple_of(kt * TQ, TQ), TQ)
        return [_dot_nt(q_ref[rows, hd * MLA_QK_PAD:(hd + 1) * MLA_QK_PAD],
                        k_ref[keys, hd * MLA_QK_PAD:(hd + 1) * MLA_QK_PAD])
                for hd in range(HEADS_PER_STEP)]

    def consume(k, s_heads):
        which, _, kt, diagonal = item(k)
        keys = pl.ds(pl.multiple_of(kt * TQ, TQ), TQ)
        for hd, s in enumerate(s_heads):
            if diagonal:
                row = lax.broadcasted_iota(jnp.int32, (TQ, TQ), 0)
                col = lax.broadcasted_iota(jnp.int32, (TQ, TQ), 1)
                s = jnp.where(col <= row, s, NEG_BIG)
            m_prev = m_ref[which, hd]
            m_new = jnp.maximum(m_prev, jnp.max(s, axis=1, keepdims=True))
            alpha = jnp.exp2(m_prev - m_new)
            p = jnp.exp2(s - _lane_tile(m_new, TQ // LANES))
            pv = _dot(p.astype(BF16), v_ref[keys, hd * V_PAD:(hd + 1) * V_PAD])
            acc_ref[which, hd] = _lane_tile(alpha, V_PAD // LANES) * acc_ref[which, hd] + pv
            m_ref[which, hd] = m_new

    n_items = tiles + 1
    pending = [scores(k) for k in range(SCORES_AHEAD)]
    for k in range(n_items):
        if k + SCORES_AHEAD < n_items:
            pending.append(scores(k + SCORES_AHEAD))
        consume(k, pending.pop(0))

    for which in range(2):
        rows = pl.ds(pl.multiple_of(pair[which] * TQ, TQ), TQ)
        for hd in range(HEADS_PER_STEP):
            acc = acc_ref[which, hd]
            o_ref[rows, hd * MLA_V:(hd + 1) * MLA_V] = (acc[:, :MLA_V] / acc[:, MLA_V:]).astype(BF16)


def _flash(q, k, v, batch, seq):
    m = q.shape[0]
    g = HEADS_PER_STEP
    tiles = seq // TQ
    assert tiles % 2 == 0, "query tiles are processed in (i, tiles-1-i) pairs"
    return pl.pallas_call(
        _flash_kernel,
        out_shape=jax.ShapeDtypeStruct((m, HEADS * MLA_V), BF16),
        grid=(batch, HEADS // g, tiles // 2),
        in_specs=[
            pl.BlockSpec((seq, g * MLA_QK_PAD), lambda b, h, i: (b, h)),
            pl.BlockSpec((seq, g * MLA_QK_PAD), lambda b, h, i: (b, h)),
            pl.BlockSpec((seq, g * V_PAD), lambda b, h, i: (b, h)),
        ],
        out_specs=pl.BlockSpec((seq, g * MLA_V), lambda b, h, i: (b, h)),
        scratch_shapes=[pltpu.VMEM((2, g, TQ, LANES), F32),
                        pltpu.VMEM((2, g, TQ, V_PAD), F32)],
        compiler_params=_params("parallel", "parallel", "arbitrary"),
        name="flash_attention",
    )(q, k, v)


def _row(g, repeat=1):
    return jnp.tile(g.reshape(1, -1).astype(F32), (1, repeat))


def kernel(x, positions, norm_g, ffn_w_gate, ffn_w_up, ffn_w_down, ret_w_in, ret_gn_g, ret_w_o,
           kv_norm_g, kv_w_down, kv_latent_norm_g, kv_w_up, k_nope_norm_g, k_rope_norm_g,
           mla_w_dq, mla_q_lora_norm_g, mla_w_uq, mla_q_nope_norm_g, mla_q_rope_norm_g, mla_w_o):
    batch, seq, d = x.shape
    depth = norm_g.shape[0]
    n_self = ret_w_in.shape[0]
    m = batch * seq
    x = x.reshape(m, d)
    def ffn_casts(layer, half):
        return [(w, (layer, half)) for w in (ffn_w_gate, ffn_w_up, ffn_w_down)]

    def mixer_casts(layer):
        if layer < n_self:
            return [(ret_w_in, (layer,)), (ret_w_o, (layer,))]
        return [(mla_w_dq, (layer - n_self,)), (mla_w_o, (layer - n_self,))]

    tables, cast = _rope_tables(positions.reshape(m, 1), ffn_casts(0, 0) + mixer_casts(0))
    rcos, rsin, mcos, msa, msb = tables
    ffn_w, (w_in, w_o) = cast[:3], cast[3:]

    k_shared = v_shared = None
    for layer in range(depth):
        x, ffn_w = _ffn(x, _row(norm_g[layer, 0]), *ffn_w, casts=ffn_casts(layer, 1))
        g_mix = _row(norm_g[layer, 1])
        if layer < n_self:
            proj, kt = _ret_proj(x, g_mix, w_in, rcos, rsin)
            mix = dict(ret_mix=(_retention(proj, kt, batch, seq), proj, _row(ret_gn_g[layer]), w_o))
        else:
            j = layer - n_self
            wuq = mla_w_uq[j].reshape(Q_LORA, HEADS, MLA_NOPE + MLA_ROPE)
            wuq = jnp.concatenate([wuq[:, :, :MLA_NOPE].reshape(Q_LORA, Q_NOPE_COLS),
                                   wuq[:, :, MLA_NOPE:].reshape(Q_LORA, HEADS * MLA_ROPE)], axis=1)
            q = _mla_q(x, g_mix, w_in, _row(mla_q_lora_norm_g[j]),
                       wuq.astype(BF16), _row(mla_q_nope_norm_g[j], 2), _row(mla_q_rope_norm_g[j], 4),
                       mcos, msa, msb)
            mix = dict(mix=(_flash(q, k_shared, v_shared, batch, seq), w_o))
        following = ffn_casts(layer + 1, 0) + mixer_casts(layer + 1) if layer + 1 < depth else []
        x, cast = _ffn(x, _row(norm_g[layer, 2]), *ffn_w, casts=following, **mix)
        if following:
            ffn_w, (w_in, w_o) = cast[:3], cast[3:]
        if layer == n_self - 1:
            wup = kv_w_up.reshape(KV_LORA, HEADS, MLA_NOPE + MLA_V)
            wk = wup[:, :, :MLA_NOPE].reshape(KV_LORA, HEADS * MLA_NOPE).astype(BF16)
            wv = wup[:, :, MLA_NOPE:].reshape(KV_LORA, HEADS * MLA_V).astype(BF16)
            wdr = jnp.tile(kv_w_down[:, KV_LORA:], (1, 2)).astype(BF16)
            k_shared, v_shared = _shared_kv(
                x, _row(kv_norm_g), kv_w_down[:, :KV_LORA].astype(BF16), wdr,
                _row(kv_latent_norm_g), wk, wv, _row(k_nope_norm_g, 2), _row(k_rope_norm_g, 2),
                mcos, msa, msb)
    return x.reshape(batch, seq, d)
```

```python
import math

import jax
import jax.numpy as jnp
from jax import lax
from jax.experimental import pallas as pl
from jax.experimental.pallas import tpu as pltpu

F32 = jnp.float32
BF16 = jnp.bfloat16

LANES = 128
BF16_ROWS = 16

D_MODEL = 1024
D_FF = 2816
HEADS = 8
RET_DK = 128
RET_DV = 256
RET_QK = HEADS * RET_DK
RET_PROJ = 2 * RET_QK + 2 * HEADS * RET_DV
RET_V0 = HEADS * RET_DV
RET_Q0 = 2 * HEADS * RET_DV
MLA_NOPE = 128
MLA_ROPE = 64
MLA_V = 128
MLA_QK_PAD = 2 * LANES
V_PAD = 2 * LANES
HEADS_PER_STEP = 2
Q_LORA = 384
KV_LORA = 256
ROPE_BASE = 10000.0
EPS = 1e-6

VMEM_LIMIT = 56 * 1024 * 1024

TM = 512
TM_FFN = 1024
TM_MLA = 1024
FF_CHUNK = 256
RET_CHUNK = 256
RET_HEADS_PER_STEP = 2
TQ = 512
SCORES_AHEAD = 2
NEG_BIG = -1e30


def _dot(a, b):
    return jnp.dot(a, b, preferred_element_type=F32)


def _dot_nt(a, b):
    return lax.dot_general(a, b, (((1,), (1,)), ((), ())), preferred_element_type=F32)


def _rms(x, g, n):
    ms = jnp.sum(x * x, axis=-1, keepdims=True) * (1.0 / n)
    return x * lax.rsqrt(ms + EPS) * g


def _silu(x):
    return x / (1.0 + jnp.exp2(x * -math.log2(math.e)))


def _lane_tile(x, n):
    return jnp.concatenate([x] * n, axis=1)


def _group_mean_matrix(width, group):
    shift = group.bit_length() - 1
    r = lax.broadcasted_iota(jnp.int32, (width, width), 0) >> shift
    c = lax.broadcasted_iota(jnp.int32, (width, width), 1) >> shift
    return jnp.where(r == c, 1.0 / group, 0.0).astype(BF16)


def _group_rms(x, mean_matrix, g):
    ms = _dot((x * x).astype(BF16), mean_matrix)
    return x * lax.rsqrt(ms + EPS) * g


def _params(*sem):
    return pltpu.CompilerParams(dimension_semantics=sem, vmem_limit_bytes=VMEM_LIMIT)


def _resident(shape):
    nd = len(shape)
    return pl.BlockSpec(shape, lambda *_: (0,) * nd, pipeline_mode=pl.Buffered(1))


def _rows(width, tm=TM):
    return pl.BlockSpec((tm, width), lambda i: (i, 0))


class _RideAlongCasts:
    def __init__(self, casts, steps):
        self.sources = tuple(src for src, _ in casts)
        self.in_specs, self.out_shapes, self.out_specs = [], [], []
        for src, prefix in casts:
            rows, cols = src.shape[-2:]
            slab = next(s for s in range(BF16_ROWS, rows + 1, BF16_ROWS)
                        if rows % s == 0 and rows // s <= steps)
            last = rows // slab - 1
            lead = (None,) * len(prefix)
            self.in_specs.append(pl.BlockSpec(
                (*lead, slab, cols),
                lambda i, prefix=prefix, last=last: (*prefix, jnp.minimum(i, last), 0)))
            self.out_shapes.append(jax.ShapeDtypeStruct((rows, cols), BF16))
            self.out_specs.append(pl.BlockSpec(
                (slab, cols), lambda i, last=last: (jnp.minimum(i, last), 0)))

    def wrap(self, body, n_in, n_out):
        k = len(self.sources)

        def kernel(*refs):
            ins, refs = refs[:n_in], refs[n_in:]
            srcs, refs = refs[:k], refs[k:]
            outs, refs = refs[:n_out], refs[n_out:]
            for src, dst in zip(srcs, refs[:k]):
                dst[...] = src[...].astype(BF16)
            body(*ins, *outs, *refs[k:])
        return kernel


def _rope_tables_kernel(pos_ref, rcos_ref, rsin_ref, mcos_ref, msa_ref, msb_ref):
    pos = pos_ref[...].astype(F32)
    lane = lax.broadcasted_iota(jnp.int32, (1, LANES), 1)
    is_ret = lane < 64
    freq = jnp.where(is_ret, lane, lane & 31).astype(F32)
    step = jnp.where(is_ret, -2.0 / RET_DK, -2.0 / MLA_ROPE) * math.log(ROPE_BASE)
    ang = pos * jnp.exp(freq * step)
    c = jnp.cos(ang)
    s = jnp.sin(ang)
    rcos_ref[...] = jnp.where(is_ret, c, pltpu.roll(c, 64, 1))
    rsin_ref[...] = jnp.where(is_ret, -s, pltpu.roll(s, 64, 1))

    def spread(t):
        quarter = lane >> 5
        return jnp.where(quarter == 0, pltpu.roll(t, 64, 1),
                         jnp.where(quarter == 1, pltpu.roll(t, 96, 1),
                                   jnp.where(quarter == 2, t, pltpu.roll(t, 32, 1))))

    s = spread(s)
    first_half = (lane & 63) < 32
    mcos_ref[...] = spread(c)
    msa_ref[...] = jnp.where(first_half, -s, 0.0)
    msb_ref[...] = jnp.where(first_half, 0.0, s)


def _rope_tables(pos, casts):
    m = pos.shape[0]
    tm = 1024
    tab = jax.ShapeDtypeStruct((m, LANES), F32)
    spec = pl.BlockSpec((tm, LANES), lambda i: (i, 0))
    riders = _RideAlongCasts(casts, m // tm)
    out = pl.pallas_call(
        riders.wrap(_rope_tables_kernel, n_in=1, n_out=5),
        out_shape=[tab] * 5 + riders.out_shapes,
        grid=(m // tm,),
        in_specs=[pl.BlockSpec((tm, 1), lambda i: (i, 0)), *riders.in_specs],
        out_specs=[spec] * 5 + riders.out_specs,
        compiler_params=_params("arbitrary"),
        name="rope_tables",
    )(pos, *riders.sources)
    return out[:5], out[5:]


def _rope128(x, cos, sin):
    return x * cos + pltpu.roll(x, 64, 1) * sin


def _rope64x2(x, cos, sa, sb):
    return x * cos + pltpu.roll(x, 96, 1) * sa + pltpu.roll(x, 32, 1) * sb


def _ffn_body(x, g_ref, wg_ref, wu_ref, wd_ref, o_ref):
    h = _rms(x, g_ref[...], D_MODEL).astype(BF16)
    hidden = []
    for c in range(D_FF // FF_CHUNK):
        sl = slice(c * FF_CHUNK, (c + 1) * FF_CHUNK)
        hidden.append((_silu(_dot(h, wg_ref[:, sl])) * _dot(h, wu_ref[:, sl])).astype(BF16))
    o_ref[...] = x + 0.5 * _dot(jnp.concatenate(hidden, axis=1), wd_ref[...])


def _ffn_kernel(x_ref, g_ref, wg_ref, wu_ref, wd_ref, o_ref):
    _ffn_body(x_ref[...], g_ref, wg_ref, wu_ref, wd_ref, o_ref)


def _mix_ffn_kernel(x_ref, a_ref, wo_ref, g_ref, wg_ref, wu_ref, wd_ref, o_ref):
    _ffn_body(x_ref[...] + _dot(a_ref[...], wo_ref[...]), g_ref, wg_ref, wu_ref, wd_ref, o_ref)


def _ret_mix_ffn_kernel(x_ref, ret_ref, gate_ref, gn_ref, wo_ref, g_ref, wg_ref, wu_ref, wd_ref,
                        o_ref):
    x = x_ref[...]
    for hd in range(HEADS):
        cols = slice(hd * RET_DV, (hd + 1) * RET_DV)
        o = ret_ref[:, cols].astype(F32)
        oc = o - jnp.mean(o, axis=-1, keepdims=True)
        var = jnp.mean(oc * oc, axis=-1, keepdims=True)
        on = oc * lax.rsqrt(var + EPS) * gn_ref[:, cols]
        a = (_silu(gate_ref[:, cols].astype(F32)) * on).astype(BF16)
        x = x + _dot(a, wo_ref[cols, :])
    _ffn_body(x, g_ref, wg_ref, wu_ref, wd_ref, o_ref)


def _ffn(x, g, wg, wu, wd, mix=None, ret_mix=None, casts=()):
    m = x.shape[0]
    w_specs = [_resident((1, D_MODEL)), _resident((D_MODEL, D_FF)), _resident((D_MODEL, D_FF)),
               _resident((D_FF, D_MODEL))]
    if ret_mix is not None:
        o, proj, gn, wo = ret_mix
        tm = TM
        name, body, args = "ret_mix_ffn", _ret_mix_ffn_kernel, (x, o, proj, gn, wo)
        specs = [_rows(D_MODEL, tm), _rows(HEADS * RET_DV, tm), _rows(HEADS * RET_DV, tm),
                 _resident(gn.shape), _resident(wo.shape)]
    elif mix is not None:
        a, wo = mix
        tm = TM_FFN
        name, body, args = "mix_ffn", _mix_ffn_kernel, (x, a, wo)
        specs = [_rows(D_MODEL, tm), _rows(a.shape[1], tm), _resident(wo.shape)]
    else:
        tm = TM_FFN
        name, body, args, specs = "ffn", _ffn_kernel, (x,), [_rows(D_MODEL, tm)]
    steps = m // tm
    args = (*args, g, wg, wu, wd)
    riders = _RideAlongCasts(casts, steps)
    out = pl.pallas_call(
        riders.wrap(body, n_in=len(args), n_out=1),
        out_shape=[jax.ShapeDtypeStruct((m, D_MODEL), F32)] + riders.out_shapes,
        grid=(steps,),
        in_specs=[*specs, *w_specs, *riders.in_specs],
        out_specs=[_rows(D_MODEL, tm)] + riders.out_specs,
        compiler_params=_params("arbitrary"),
        name=name,
    )(*args, *riders.sources)
    return out[0], out[1:]


def _ret_proj_kernel(x_ref, g_ref, w_ref, cos_ref, sin_ref, o_ref, kt_ref):
    h = _rms(x_ref[...], g_ref[...], D_MODEL).astype(BF16)
    cos = cos_ref[...]
    sin = sin_ref[...]
    q = _dot(h, w_ref[:, :RET_QK])
    k = _dot(h, w_ref[:, RET_QK:2 * RET_QK])
    for hd in range(HEADS):
        head = slice(hd * RET_DK, (hd + 1) * RET_DK)
        o_ref[:, RET_Q0 + hd * RET_DK:RET_Q0 + (hd + 1) * RET_DK] = (
            _rope128(q[:, head], cos, sin).astype(BF16))
        kh = _rope128(k[:, head], cos, sin) * RET_DK ** -0.5
        for c in range(TM // RET_CHUNK):
            kt_ref[c, head, :] = kh[c * RET_CHUNK:(c + 1) * RET_CHUNK, :].T.astype(BF16)
    v0 = 2 * RET_QK
    g0 = v0 + HEADS * RET_DV
    o_ref[:, RET_V0:RET_Q0] = _dot(h, w_ref[:, v0:g0]).astype(BF16)
    o_ref[:, :RET_V0] = _dot(h, w_ref[:, g0:]).astype(BF16)


def _ret_proj(x, g, w, cos, sin):
    m = x.shape[0]
    chunks = TM // RET_CHUNK
    return pl.pallas_call(
        _ret_proj_kernel,
        out_shape=(jax.ShapeDtypeStruct((m, RET_PROJ - RET_QK), BF16),
                   jax.ShapeDtypeStruct((m // RET_CHUNK, RET_QK, RET_CHUNK), BF16)),
        grid=(m // TM,),
        in_specs=[_rows(D_MODEL), _resident((1, D_MODEL)), _resident((D_MODEL, RET_PROJ)),
                  _rows(LANES), _rows(LANES)],
        out_specs=(_rows(RET_PROJ - RET_QK),
                   pl.BlockSpec((chunks, RET_QK, RET_CHUNK), lambda i: (i, 0, 0))),
        compiler_params=_params("parallel"),
        name="ret_proj",
    )(x, g, w, cos, sin)


def _retention_kernel(q_ref, kt_ref, v_ref, o_ref, state_ref):
    c = RET_CHUNK
    seq = q_ref.shape[0]

    def index(shape, axis):
        return lax.broadcasted_iota(jnp.int32, shape, axis).astype(F32)

    def decays(sub):
        hd = (pl.program_id(1) * RET_HEADS_PER_STEP + sub).astype(F32)

        def log_decay(shape):
            return jnp.log1p(-jnp.exp2(jnp.full(shape, -5.0, F32) - hd))

        diff = index((c, c), 0) - index((c, c), 1)
        d_intra = jnp.where(diff >= 0, jnp.exp(log_decay((c, c)) * jnp.maximum(diff, 0.0)), 0.0)
        q_decay = jnp.exp(log_decay((c, RET_DV)) * (index((c, RET_DV), 0) + 1.0))
        k_decay = jnp.exp(log_decay((RET_DK, c)) * (c - 1.0 - index((RET_DK, c), 1)))
        chunk_decay = jnp.exp(log_decay((RET_DK, RET_DV)) * c)
        return d_intra, q_decay, k_decay, chunk_decay

    per_head = [decays(sub) for sub in range(RET_HEADS_PER_STEP)]
    state_ref[...] = jnp.zeros_like(state_ref)

    def body(t, carry):
        off = pl.multiple_of(t * c, c)
        for sub, (d_intra, q_decay, k_decay, chunk_decay) in enumerate(per_head):
            qk = slice(sub * RET_DK, (sub + 1) * RET_DK)
            vo = slice(sub * RET_DV, (sub + 1) * RET_DV)
            q = q_ref[pl.ds(off, c), qk]
            kt = kt_ref[t, qk, :]
            v = v_ref[pl.ds(off, c), vo]
            state = state_ref[sub]
            s = _dot(q, kt) * d_intra
            o = _dot(s.astype(BF16), v) + _dot(q, state.astype(BF16)) * q_decay
            ktd = (kt.astype(F32) * k_decay).astype(BF16)
            state_ref[sub] = state * chunk_decay + _dot(ktd, v)
            o_ref[pl.ds(off, c), vo] = o.astype(BF16)
        return carry

    lax.fori_loop(0, seq // c, body, 0, unroll=True)


def _retention(proj, kt, batch, seq):
    m = proj.shape[0]
    g = RET_HEADS_PER_STEP
    qblk = RET_Q0 // (g * RET_DK)
    vblk = RET_V0 // (g * RET_DV)
    return pl.pallas_call(
        _retention_kernel,
        out_shape=jax.ShapeDtypeStruct((m, HEADS * RET_DV), BF16),
        grid=(batch, HEADS // g),
        in_specs=[
            pl.BlockSpec((seq, g * RET_DK), lambda b, h: (b, qblk + h)),
            pl.BlockSpec((seq // RET_CHUNK, g * RET_DK, RET_CHUNK), lambda b, h: (b, h, 0)),
            pl.BlockSpec((seq, g * RET_DV), lambda b, h: (b, vblk + h)),
        ],
        out_specs=pl.BlockSpec((seq, g * RET_DV), lambda b, h: (b, h)),
        scratch_shapes=[pltpu.VMEM((g, RET_DK, RET_DV), F32)],
        compiler_params=_params("parallel", "parallel"),
        name="retention",
    )(proj, kt, proj)


def _rope_tile_for_head(tile, hd):
    lane = lax.broadcasted_iota(jnp.int32, tile.shape, 1)
    keep = (lane < MLA_ROPE) if hd % 2 == 0 else (lane >= MLA_ROPE)
    return jnp.where(keep, tile, jnp.zeros_like(tile))


def _kv_kernel(x_ref, g_ref, wdc_ref, wdr_ref, lat_g_ref, wk_ref, wv_ref, kn_g_ref, kr_g_ref,
               cos_ref, sa_ref, sb_ref, k_out, v_out):
    h = _rms(x_ref[...], g_ref[...], D_MODEL).astype(BF16)
    lat = _rms(_dot(h, wdc_ref[...]), lat_g_ref[...], KV_LORA).astype(BF16)
    pe = _rms(_dot(h, wdr_ref[...]), kr_g_ref[...], LANES)
    pe = _rope64x2(pe, cos_ref[...], sa_ref[...], sb_ref[...]).astype(BF16)
    pe_tiles = (_rope_tile_for_head(pe, 0), _rope_tile_for_head(pe, 1))
    ones = jnp.ones((pe.shape[0], V_PAD - MLA_V), BF16)
    mean_nope = _group_mean_matrix(2 * MLA_NOPE, MLA_NOPE)
    kn_g = kn_g_ref[...]
    for pair in range(HEADS // 2):
        lo = pair * 2 * MLA_NOPE
        kn = _group_rms(_dot(lat, wk_ref[:, lo:lo + 2 * MLA_NOPE]), mean_nope, kn_g)
        kn = kn.astype(BF16)
        vv = _dot(lat, wv_ref[:, pair * 2 * MLA_V:(pair + 1) * 2 * MLA_V]).astype(BF16)
        for sub in range(2):
            hd = 2 * pair + sub
            k_out[:, hd * MLA_QK_PAD:hd * MLA_QK_PAD + MLA_NOPE] = (
                kn[:, sub * MLA_NOPE:(sub + 1) * MLA_NOPE])
            k_out[:, hd * MLA_QK_PAD + MLA_NOPE:(hd + 1) * MLA_QK_PAD] = pe_tiles[sub]
            v_out[:, hd * V_PAD:hd * V_PAD + MLA_V] = vv[:, sub * MLA_V:(sub + 1) * MLA_V]
            v_out[:, hd * V_PAD + MLA_V:(hd + 1) * V_PAD] = ones


def _shared_kv(x, g, wdc, wdr, lat_g, wk, wv, kn_g, kr_g, cos, sa, sb):
    m = x.shape[0]
    return pl.pallas_call(
        _kv_kernel,
        out_shape=(jax.ShapeDtypeStruct((m, HEADS * MLA_QK_PAD), BF16),
                   jax.ShapeDtypeStruct((m, HEADS * V_PAD), BF16)),
        grid=(m // TM_MLA,),
        in_specs=[_rows(D_MODEL, TM_MLA), _resident((1, D_MODEL)), _resident((D_MODEL, KV_LORA)),
                  _resident((D_MODEL, LANES)), _resident((1, KV_LORA)),
                  _resident((KV_LORA, HEADS * MLA_NOPE)), _resident((KV_LORA, HEADS * MLA_V)),
                  _resident((1, 2 * MLA_NOPE)), _resident((1, LANES)),
                  _rows(LANES, TM_MLA), _rows(LANES, TM_MLA), _rows(LANES, TM_MLA)],
        out_specs=(_rows(HEADS * MLA_QK_PAD, TM_MLA), _rows(HEADS * V_PAD, TM_MLA)),
        compiler_params=_params("parallel"),
        name="shared_kv",
    )(x, g, wdc, wdr, lat_g, wk, wv, kn_g, kr_g, cos, sa, sb)


Q_SCALE = (MLA_NOPE + MLA_ROPE) ** -0.5 * math.log2(math.e)
Q_NOPE_COLS = HEADS * MLA_NOPE


def _q_kernel(x_ref, g_ref, wdq_ref, lora_g_ref, wuq_ref, qn_g_ref, qr_g_ref,
              cos_ref, sa_ref, sb_ref, q_out):
    h = _rms(x_ref[...], g_ref[...], D_MODEL).astype(BF16)
    cq = _rms(_dot(h, wdq_ref[...]), lora_g_ref[...], Q_LORA).astype(BF16)
    q = _dot(cq, wuq_ref[...])
    mean_nope = _group_mean_matrix(2 * MLA_NOPE, MLA_NOPE)
    mean_rope = _group_mean_matrix(2 * LANES, MLA_ROPE)
    qn_g = qn_g_ref[...] * Q_SCALE
    qr_g = qr_g_ref[...] * Q_SCALE
    cos = cos_ref[...]
    sa = sa_ref[...]
    sb = sb_ref[...]
    for quad in range(HEADS // 4):
        lo = Q_NOPE_COLS + quad * 2 * LANES
        qp = _group_rms(q[:, lo:lo + 2 * LANES], mean_rope, qr_g)
        for half in range(2):
            pair = 2 * quad + half
            lo = pair * 2 * MLA_NOPE
            qn = _group_rms(q[:, lo:lo + 2 * MLA_NOPE], mean_nope, qn_g).astype(BF16)
            tile = _rope64x2(qp[:, half * LANES:(half + 1) * LANES], cos, sa, sb).astype(BF16)
            for sub in range(2):
                hd = 2 * pair + sub
                q_out[:, hd * MLA_QK_PAD:hd * MLA_QK_PAD + MLA_NOPE] = (
                    qn[:, sub * MLA_NOPE:(sub + 1) * MLA_NOPE])
                q_out[:, hd * MLA_QK_PAD + MLA_NOPE:(hd + 1) * MLA_QK_PAD] = (
                    _rope_tile_for_head(tile, sub))


def _mla_q(x, g, wdq, lora_g, wuq, qn_g, qr_g, cos, sa, sb):
    m = x.shape[0]
    return pl.pallas_call(
        _q_kernel,
        out_shape=jax.ShapeDtypeStruct((m, HEADS * MLA_QK_PAD), BF16),
        grid=(m // TM_MLA,),
        in_specs=[_rows(D_MODEL, TM_MLA), _resident((1, D_MODEL)), _resident((D_MODEL, Q_LORA)),
                  _resident((1, Q_LORA)), _resident(wuq.shape),
                  _resident((1, 2 * MLA_NOPE)), _resident((1, 2 * LANES)),
                  _rows(LANES, TM_MLA), _rows(LANES, TM_MLA), _rows(LANES, TM_MLA)],
        out_specs=_rows(HEADS * MLA_QK_PAD, TM_MLA),
        compiler_params=_params("parallel"),
        name="mla_q",
    )(x, g, wdq, lora_g, wuq, qn_g, qr_g, cos, sa, sb)


def _flash_kernel(q_ref, k_ref, v_ref, o_ref, m_ref, acc_ref):
    tiles = q_ref.shape[0] // TQ
    i = pl.program_id(2)
    pair = (i, tiles - 1 - i)
    m_ref[...] = jnp.full_like(m_ref, NEG_BIG)
    acc_ref[...] = jnp.zeros_like(acc_ref)

    def item(k):
        if k < tiles - 1:
            second = k >= i
            return (jnp.where(second, 1, 0), jnp.where(second, pair[1], pair[0]),
                    jnp.where(second, k - i, k), False)
        which = k - (tiles - 1)
        return which, pair[which], pair[which], True

    def scores(k):
        _, qt, kt, _ = item(k)
        rows = pl.ds(pl.multiple_of(qt * TQ, TQ), TQ)
        keys = pl.ds(pl.multiple_of(kt * TQ, TQ), TQ)
        return [_dot_nt(q_ref[rows, hd * MLA_QK_PAD:(hd + 1) * MLA_QK_PAD],
                        k_ref[keys, hd * MLA_QK_PAD:(hd + 1) * MLA_QK_PAD])
                for hd in range(HEADS_PER_STEP)]

    def consume(k, s_heads):
        which, _, kt, diagonal = item(k)
        keys = pl.ds(pl.multiple_of(kt * TQ, TQ), TQ)
        for hd, s in enumerate(s_heads):
            if diagonal:
                row = lax.broadcasted_iota(jnp.int32, (TQ, TQ), 0)
                col = lax.broadcasted_iota(jnp.int32, (TQ, TQ), 1)
                s = jnp.where(col <= row, s, NEG_BIG)
            m_prev = m_ref[which, hd]
            m_new = jnp.maximum(m_prev, jnp.max(s, axis=1, keepdims=True))
            alpha = jnp.exp2(m_prev - m_new)
            p = jnp.exp2(s - _lane_tile(m_new, TQ // LANES))
            pv = _dot(p.astype(BF16), v_ref[keys, hd * V_PAD:(hd + 1) * V_PAD])
            acc_ref[which, hd] = _lane_tile(alpha, V_PAD // LANES) * acc_ref[which, hd] + pv
            m_ref[which, hd] = m_new

    n_items = tiles + 1
    pending = [scores(k) for k in range(SCORES_AHEAD)]
    for k in range(n_items):
        if k + SCORES_AHEAD < n_items:
            pending.append(scores(k + SCORES_AHEAD))
        consume(k, pending.pop(0))

    for which in range(2):
        rows = pl.ds(pl.multiple_of(pair[which] * TQ, TQ), TQ)
        for hd in range(HEADS_PER_STEP):
            acc = acc_ref[which, hd]
            o_ref[rows, hd * MLA_V:(hd + 1) * MLA_V] = (acc[:, :MLA_V] / acc[:, MLA_V:]).astype(BF16)


def _flash(q, k, v, batch, seq):
    m = q.shape[0]
    g = HEADS_PER_STEP
    tiles = seq // TQ
    assert tiles % 2 == 0, "query tiles are processed in (i, tiles-1-i) pairs"
    return pl.pallas_call(
        _flash_kernel,
        out_shape=jax.ShapeDtypeStruct((m, HEADS * MLA_V), BF16),
        grid=(batch, HEADS // g, tiles // 2),
        in_specs=[
            pl.BlockSpec((seq, g * MLA_QK_PAD), lambda b, h, i: (b, h)),
            pl.BlockSpec((seq, g * MLA_QK_PAD), lambda b, h, i: (b, h)),
            pl.BlockSpec((seq, g * V_PAD), lambda b, h, i: (b, h)),
        ],
        out_specs=pl.BlockSpec((seq, g * MLA_V), lambda b, h, i: (b, h)),
        scratch_shapes=[pltpu.VMEM((2, g, TQ, LANES), F32),
                        pltpu.VMEM((2, g, TQ, V_PAD), F32)],
        compiler_params=_params("parallel", "parallel", "arbitrary"),
        name="flash_attention",
    )(q, k, v)


def _row(g, repeat=1):
    return jnp.tile(g.reshape(1, -1).astype(F32), (1, repeat))


def kernel(x, positions, norm_g, ffn_w_gate, ffn_w_up, ffn_w_down, ret_w_in, ret_gn_g, ret_w_o,
           kv_norm_g, kv_w_down, kv_latent_norm_g, kv_w_up, k_nope_norm_g, k_rope_norm_g,
           mla_w_dq, mla_q_lora_norm_g, mla_w_uq, mla_q_nope_norm_g, mla_q_rope_norm_g, mla_w_o):
    batch, seq, d = x.shape
    depth = norm_g.shape[0]
    n_self = ret_w_in.shape[0]
    m = batch * seq
    x = x.reshape(m, d)
    def ffn_casts(layer, half):
        return [(w, (layer, half)) for w in (ffn_w_gate, ffn_w_up, ffn_w_down)]

    def mixer_casts(layer):
        if layer < n_self:
            return [(ret_w_in, (layer,)), (ret_w_o, (layer,))]
        return [(mla_w_dq, (layer - n_self,)), (mla_w_o, (layer - n_self,))]

    tables, cast = _rope_tables(positions.reshape(m, 1), ffn_casts(0, 0) + mixer_casts(0))
    rcos, rsin, mcos, msa, msb = tables
    ffn_w, (w_in, w_o) = cast[:3], cast[3:]

    k_shared = v_shared = None
    for layer in range(depth):
        x, ffn_w = _ffn(x, _row(norm_g[layer, 0]), *ffn_w, casts=ffn_casts(layer, 1))
        g_mix = _row(norm_g[layer, 1])
        if layer < n_self:
            proj, kt = _ret_proj(x, g_mix, w_in, rcos, rsin)
            mix = dict(ret_mix=(_retention(proj, kt, batch, seq), proj, _row(ret_gn_g[layer]), w_o))
        else:
            j = layer - n_self
            wuq = mla_w_uq[j].reshape(Q_LORA, HEADS, MLA_NOPE + MLA_ROPE)
            wuq = jnp.concatenate([wuq[:, :, :MLA_NOPE].reshape(Q_LORA, Q_NOPE_COLS),
                                   wuq[:, :, MLA_NOPE:].reshape(Q_LORA, HEADS * MLA_ROPE)], axis=1)
            q = _mla_q(x, g_mix, w_in, _row(mla_q_lora_norm_g[j]),
                       wuq.astype(BF16), _row(mla_q_nope_norm_g[j], 2), _row(mla_q_rope_norm_g[j], 4),
                       mcos, msa, msb)
            mix = dict(mix=(_flash(q, k_shared, v_shared, batch, seq), w_o))
        following = ffn_casts(layer + 1, 0) + mixer_casts(layer + 1) if layer + 1 < depth else []
        x, cast = _ffn(x, _row(norm_g[layer, 2]), *ffn_w, casts=following, **mix)
        if following:
            ffn_w, (w_in, w_o) = cast[:3], cast[3:]
        if layer == n_self - 1:
            wup = kv_w_up.reshape(KV_LORA, HEADS, MLA_NOPE + MLA_V)
            wk = wup[:, :, :MLA_NOPE].reshape(KV_LORA, HEADS * MLA_NOPE).astype(BF16)
            wv = wup[:, :, MLA_NOPE:].reshape(KV_LORA, HEADS * MLA_V).astype(BF16)
            wdr = jnp.tile(kv_w_down[:, KV_LORA:], (1, 2)).astype(BF16)
            k_shared, v_shared = _shared_kv(
                x, _row(kv_norm_g), kv_w_down[:, :KV_LORA].astype(BF16), wdr,
                _row(kv_latent_norm_g), wk, wv, _row(k_nope_norm_g, 2), _row(k_rope_norm_g, 2),
                mcos, msa, msb)
    return x.reshape(batch, seq, d)
```

```python
import math

import jax
import jax.numpy as jnp
from jax import lax
from jax.experimental import pallas as pl
from jax.experimental.pallas import tpu as pltpu

F32 = jnp.float32
BF16 = jnp.bfloat16

LANES = 128
BF16_ROWS = 16

D_MODEL = 1024
D_FF = 2816
HEADS = 8
RET_DK = 128
RET_DV = 256
RET_QK = HEADS * RET_DK
RET_PROJ = 2 * RET_QK + 2 * HEADS * RET_DV
RET_V0 = HEADS * RET_DV
RET_Q0 = 2 * HEADS * RET_DV
MLA_NOPE = 128
MLA_ROPE = 64
MLA_V = 128
MLA_QK_PAD = 2 * LANES
V_PAD = 2 * LANES
HEADS_PER_STEP = 2
Q_LORA = 384
KV_LORA = 256
ROPE_BASE = 10000.0
EPS = 1e-6

VMEM_LIMIT = 56 * 1024 * 1024

TM = 512
TM_FFN = 1024
TM_MLA = 1024
FF_CHUNK = 256
RET_CHUNK = 256
RET_HEADS_PER_STEP = 2
TQ = 512
SCORES_AHEAD = 2
NEG_BIG = -1e30


def _dot(a, b):
    return jnp.dot(a, b, preferred_element_type=F32)


def _dot_nt(a, b):
    return lax.dot_general(a, b, (((1,), (1,)), ((), ())), preferred_element_type=F32)


def _rms(x, g, n):
    ms = jnp.sum(x * x, axis=-1, keepdims=True) * (1.0 / n)
    return x * lax.rsqrt(ms + EPS) * g


def _silu(x):
    return x / (1.0 + jnp.exp2(x * -math.log2(math.e)))


def _lane_tile(x, n):
    return jnp.concatenate([x] * n, axis=1)


def _group_mean_matrix(width, group):
    shift = group.bit_length() - 1
    r = lax.broadcasted_iota(jnp.int32, (width, width), 0) >> shift
    c = lax.broadcasted_iota(jnp.int32, (width, width), 1) >> shift
    return jnp.where(r == c, 1.0 / group, 0.0).astype(BF16)


def _group_rms(x, mean_matrix, g):
    ms = _dot((x * x).astype(BF16), mean_matrix)
    return x * lax.rsqrt(ms + EPS) * g


def _params(*sem):
    return pltpu.CompilerParams(dimension_semantics=sem, vmem_limit_bytes=VMEM_LIMIT)


def _resident(shape):
    nd = len(shape)
    return pl.BlockSpec(shape, lambda *_: (0,) * nd, pipeline_mode=pl.Buffered(1))


def _rows(width, tm=TM):
    return pl.BlockSpec((tm, width), lambda i: (i, 0))


class _RideAlongCasts:
    def __init__(self, casts, steps):
        self.sources = tuple(src for src, _ in casts)
        self.in_specs, self.out_shapes, self.out_specs = [], [], []
        for src, prefix in casts:
            rows, cols = src.shape[-2:]
            slab = next(s for s in range(BF16_ROWS, rows + 1, BF16_ROWS)
                        if rows % s == 0 and rows // s <= steps)
            last = rows // slab - 1
            lead = (None,) * len(prefix)
            self.in_specs.append(pl.BlockSpec(
                (*lead, slab, cols),
                lambda i, prefix=prefix, last=last: (*prefix, jnp.minimum(i, last), 0)))
            self.out_shapes.append(jax.ShapeDtypeStruct((rows, cols), BF16))
            self.out_specs.append(pl.BlockSpec(
                (slab, cols), lambda i, last=last: (jnp.minimum(i, last), 0)))

    def wrap(self, body, n_in, n_out):
        k = len(self.sources)

        def kernel(*refs):
            ins, refs = refs[:n_in], refs[n_in:]
            srcs, refs = refs[:k], refs[k:]
            outs, refs = refs[:n_out], refs[n_out:]
            for src, dst in zip(srcs, refs[:k]):
                dst[...] = src[...].astype(BF16)
            body(*ins, *outs, *refs[k:])
        return kernel


def _rope_tables_kernel(pos_ref, rcos_ref, rsin_ref, mcos_ref, msa_ref, msb_ref):
    pos = pos_ref[...].astype(F32)
    lane = lax.broadcasted_iota(jnp.int32, (1, LANES), 1)
    is_ret = lane < 64
    freq = jnp.where(is_ret, lane, lane & 31).astype(F32)
    step = jnp.where(is_ret, -2.0 / RET_DK, -2.0 / MLA_ROPE) * math.log(ROPE_BASE)
    ang = pos * jnp.exp(freq * step)
    c = jnp.cos(ang)
    s = jnp.sin(ang)
    rcos_ref[...] = jnp.where(is_ret, c, pltpu.roll(c, 64, 1))
    rsin_ref[...] = jnp.where(is_ret, -s, pltpu.roll(s, 64, 1))

    def spread(t):
        quarter = lane >> 5
        return jnp.where(quarter == 0, pltpu.roll(t, 64, 1),
                         jnp.where(quarter == 1, pltpu.roll(t, 96, 1),
                                   jnp.where(quarter == 2, t, pltpu.roll(t, 32, 1))))

    s = spread(s)
    first_half = (lane & 63) < 32
    mcos_ref[...] = spread(c)
    msa_ref[...] = jnp.where(first_half, -s, 0.0)
    msb_ref[...] = jnp.where(first_half, 0.0, s)


def _rope_tables(pos, casts):
    m = pos.shape[0]
    tm = 1024
    tab = jax.ShapeDtypeStruct((m, LANES), F32)
    spec = pl.BlockSpec((tm, LANES), lambda i: (i, 0))
    riders = _RideAlongCasts(casts, m // tm)
    out = pl.pallas_call(
        riders.wrap(_rope_tables_kernel, n_in=1, n_out=5),
        out_shape=[tab] * 5 + riders.out_shapes,
        grid=(m // tm,),
        in_specs=[pl.BlockSpec((tm, 1), lambda i: (i, 0)), *riders.in_specs],
        out_specs=[spec] * 5 + riders.out_specs,
        compiler_params=_params("arbitrary"),
        name="rope_tables",
    )(pos, *riders.sources)
    return out[:5], out[5:]


def _rope128(x, cos, sin):
    return x * cos + pltpu.roll(x, 64, 1) * sin


def _rope64x2(x, cos, sa, sb):
    return x * cos + pltpu.roll(x, 96, 1) * sa + pltpu.roll(x, 32, 1) * sb


def _ffn_body(x, g_ref, wg_ref, wu_ref, wd_ref, o_ref):
    h = _rms(x, g_ref[...], D_MODEL).astype(BF16)
    hidden = []
    for c in range(D_FF // FF_CHUNK):
        sl = slice(c * FF_CHUNK, (c + 1) * FF_CHUNK)
        hidden.append((_silu(_dot(h, wg_ref[:, sl])) * _dot(h, wu_ref[:, sl])).astype(BF16))
    o_ref[...] = x + 0.5 * _dot(jnp.concatenate(hidden, axis=1), wd_ref[...])


def _ffn_kernel(x_ref, g_ref, wg_ref, wu_ref, wd_ref, o_ref):
    _ffn_body(x_ref[...], g_ref, wg_ref, wu_ref, wd_ref, o_ref)


def _mix_ffn_kernel(x_ref, a_ref, wo_ref, g_ref, wg_ref, wu_ref, wd_ref, o_ref):
    _ffn_body(x_ref[...] + _dot(a_ref[...], wo_ref[...]), g_ref, wg_ref, wu_ref, wd_ref, o_ref)


def _ret_mix_ffn_kernel(x_ref, ret_ref, gate_ref, gn_ref, wo_ref, g_ref, wg_ref, wu_ref, wd_ref,
                        o_ref):
    x = x_ref[...]
    for hd in range(HEADS):
        cols = slice(hd * RET_DV, (hd + 1) * RET_DV)
        o = ret_ref[:, cols].astype(F32)
        oc = o - jnp.mean(o, axis=-1, keepdims=True)
        var = jnp.mean(oc * oc, axis=-1, keepdims=True)
        on = oc * lax.rsqrt(var + EPS) * gn_ref[:, cols]
        a = (_silu(gate_ref[:, cols].astype(F32)) * on).astype(BF16)
        x = x + _dot(a, wo_ref[cols, :])
    _ffn_body(x, g_ref, wg_ref, wu_ref, wd_ref, o_ref)


def _ffn(x, g, wg, wu, wd, mix=None, ret_mix=None, casts=()):
    m = x.shape[0]
    w_specs = [_resident((1, D_MODEL)), _resident((D_MODEL, D_FF)), _resident((D_MODEL, D_FF)),
               _resident((D_FF, D_MODEL))]
    if ret_mix is not None:
        o, proj, gn, wo = ret_mix
        tm = TM
        name, body, args = "ret_mix_ffn", _ret_mix_ffn_kernel, (x, o, proj, gn, wo)
        specs = [_rows(D_MODEL, tm), _rows(HEADS * RET_DV, tm), _rows(HEADS * RET_DV, tm),
                 _resident(gn.shape), _resident(wo.shape)]
    elif mix is not None:
        a, wo = mix
        tm = TM_FFN
        name, body, args = "mix_ffn", _mix_ffn_kernel, (x, a, wo)
        specs = [_rows(D_MODEL, tm), _rows(a.shape[1], tm), _resident(wo.shape)]
    else:
        tm = TM_FFN
        name, body, args, specs = "ffn", _ffn_kernel, (x,), [_rows(D_MODEL, tm)]
    steps = m // tm
    args = (*args, g, wg, wu, wd)
    riders = _RideAlongCasts(casts, steps)
    out = pl.pallas_call(
        riders.wrap(body, n_in=len(args), n_out=1),
        out_shape=[jax.ShapeDtypeStruct((m, D_MODEL), F32)] + riders.out_shapes,
        grid=(steps,),
        in_specs=[*specs, *w_specs, *riders.in_specs],
        out_specs=[_rows(D_MODEL, tm)] + riders.out_specs,
        compiler_params=_params("arbitrary"),
        name=name,
    )(*args, *riders.sources)
    return out[0], out[1:]


def _ret_proj_kernel(x_ref, g_ref, w_ref, cos_ref, sin_ref, o_ref, kt_ref):
    h = _rms(x_ref[...], g_ref[...], D_MODEL).astype(BF16)
    cos = cos_ref[...]
    sin = sin_ref[...]
    q = _dot(h, w_ref[:, :RET_QK])
    k = _dot(h, w_ref[:, RET_QK:2 * RET_QK])
    for hd in range(HEADS):
        head = slice(hd * RET_DK, (hd + 1) * RET_DK)
        o_ref[:, RET_Q0 + hd * RET_DK:RET_Q0 + (hd + 1) * RET_DK] = (
            _rope128(q[:, head], cos, sin).astype(BF16))
        kh = _rope128(k[:, head], cos, sin) * RET_DK ** -0.5
        for c in range(TM // RET_CHUNK):
            kt_ref[c, head, :] = kh[c * RET_CHUNK:(c + 1) * RET_CHUNK, :].T.astype(BF16)
    v0 = 2 * RET_QK
    g0 = v0 + HEADS * RET_DV
    o_ref[:, RET_V0:RET_Q0] = _dot(h, w_ref[:, v0:g0]).astype(BF16)
    o_ref[:, :RET_V0] = _dot(h, w_ref[:, g0:]).astype(BF16)


def _ret_proj(x, g, w, cos, sin):
    m = x.shape[0]
    chunks = TM // RET_CHUNK
    return pl.pallas_call(
        _ret_proj_kernel,
        out_shape=(jax.ShapeDtypeStruct((m, RET_PROJ - RET_QK), BF16),
                   jax.ShapeDtypeStruct((m // RET_CHUNK, RET_QK, RET_CHUNK), BF16)),
        grid=(m // TM,),
        in_specs=[_rows(D_MODEL), _resident((1, D_MODEL)), _resident((D_MODEL, RET_PROJ)),
                  _rows(LANES), _rows(LANES)],
        out_specs=(_rows(RET_PROJ - RET_QK),
                   pl.BlockSpec((chunks, RET_QK, RET_CHUNK), lambda i: (i, 0, 0))),
        compiler_params=_params("parallel"),
        name="ret_proj",
    )(x, g, w, cos, sin)


def _retention_kernel(q_ref, kt_ref, v_ref, o_ref, state_ref):
    c = RET_CHUNK
    seq = q_ref.shape[0]

    def index(shape, axis):
        return lax.broadcasted_iota(jnp.int32, shape, axis).astype(F32)

    def decays(sub):
        hd = (pl.program_id(1) * RET_HEADS_PER_STEP + sub).astype(F32)

        def log_decay(shape):
            return jnp.log1p(-jnp.exp2(jnp.full(shape, -5.0, F32) - hd))

        diff = index((c, c), 0) - index((c, c), 1)
        d_intra = jnp.where(diff >= 0, jnp.exp(log_decay((c, c)) * jnp.maximum(diff, 0.0)), 0.0)
        q_decay = jnp.exp(log_decay((c, RET_DV)) * (index((c, RET_DV), 0) + 1.0))
        k_decay = jnp.exp(log_decay((RET_DK, c)) * (c - 1.0 - index((RET_DK, c), 1)))
        chunk_decay = jnp.exp(log_decay((RET_DK, RET_DV)) * c)
        return d_intra, q_decay, k_decay, chunk_decay

    per_head = [decays(sub) for sub in range(RET_HEADS_PER_STEP)]
    state_ref[...] = jnp.zeros_like(state_ref)

    def body(t, carry):
        off = pl.multiple_of(t * c, c)
        for sub, (d_intra, q_decay, k_decay, chunk_decay) in enumerate(per_head):
            qk = slice(sub * RET_DK, (sub + 1) * RET_DK)
            vo = slice(sub * RET_DV, (sub + 1) * RET_DV)
            q = q_ref[pl.ds(off, c), qk]
            kt = kt_ref[t, qk, :]
            v = v_ref[pl.ds(off, c), vo]
            state = state_ref[sub]
            s = _dot(q, kt) * d_intra
            o = _dot(s.astype(BF16), v) + _dot(q, state.astype(BF16)) * q_decay
            ktd = (kt.astype(F32) * k_decay).astype(BF16)
            state_ref[sub] = state * chunk_decay + _dot(ktd, v)
            o_ref[pl.ds(off, c), vo] = o.astype(BF16)
        return carry

    lax.fori_loop(0, seq // c, body, 0, unroll=True)


def _retention(proj, kt, batch, seq):
    m = proj.shape[0]
    g = RET_HEADS_PER_STEP
    qblk = RET_Q0 // (g * RET_DK)
    vblk = RET_V0 // (g * RET_DV)
    return pl.pallas_call(
        _retention_kernel,
        out_shape=jax.ShapeDtypeStruct((m, HEADS * RET_DV), BF16),
        grid=(batch, HEADS // g),
        in_specs=[
            pl.BlockSpec((seq, g * RET_DK), lambda b, h: (b, qblk + h)),
            pl.BlockSpec((seq // RET_CHUNK, g * RET_DK, RET_CHUNK), lambda b, h: (b, h, 0)),
            pl.BlockSpec((seq, g * RET_DV), lambda b, h: (b, vblk + h)),
        ],
        out_specs=pl.BlockSpec((seq, g * RET_DV), lambda b, h: (b, h)),
        scratch_shapes=[pltpu.VMEM((g, RET_DK, RET_DV), F32)],
        compiler_params=_params("parallel", "parallel"),
        name="retention",
    )(proj, kt, proj)


def _rope_tile_for_head(tile, hd):
    lane = lax.broadcasted_iota(jnp.int32, tile.shape, 1)
    keep = (lane < MLA_ROPE) if hd % 2 == 0 else (lane >= MLA_ROPE)
    return jnp.where(keep, tile, jnp.zeros_like(tile))


def _kv_kernel(x_ref, g_ref, wdc_ref, wdr_ref, lat_g_ref, wk_ref, wv_ref, kn_g_ref, kr_g_ref,
               cos_ref, sa_ref, sb_ref, k_out, v_out):
    h = _rms(x_ref[...], g_ref[...], D_MODEL).astype(BF16)
    lat = _rms(_dot(h, wdc_ref[...]), lat_g_ref[...], KV_LORA).astype(BF16)
    pe = _rms(_dot(h, wdr_ref[...]), kr_g_ref[...], LANES)
    pe = _rope64x2(pe, cos_ref[...], sa_ref[...], sb_ref[...]).astype(BF16)
    pe_tiles = (_rope_tile_for_head(pe, 0), _rope_tile_for_head(pe, 1))
    mean_nope = _group_mean_matrix(2 * MLA_NOPE, MLA_NOPE)
    kn_g = kn_g_ref[...]
    for pair in range(HEADS // 2):
        lo = pair * 2 * MLA_NOPE
        kn = _group_rms(_dot(lat, wk_ref[:, lo:lo + 2 * MLA_NOPE]), mean_nope, kn_g)
        kn = kn.astype(BF16)
        cols = slice(pair * 2 * MLA_V, (pair + 1) * 2 * MLA_V)
        v_out[:, cols] = _dot(lat, wv_ref[:, cols]).astype(BF16)
        for sub in range(2):
            hd = 2 * pair + sub
            k_out[:, hd * MLA_QK_PAD:hd * MLA_QK_PAD + MLA_NOPE] = (
                kn[:, sub * MLA_NOPE:(sub + 1) * MLA_NOPE])
            k_out[:, hd * MLA_QK_PAD + MLA_NOPE:(hd + 1) * MLA_QK_PAD] = pe_tiles[sub]


def _shared_kv(x, g, wdc, wdr, lat_g, wk, wv, kn_g, kr_g, cos, sa, sb):
    m = x.shape[0]
    return pl.pallas_call(
        _kv_kernel,
        out_shape=(jax.ShapeDtypeStruct((m, HEADS * MLA_QK_PAD), BF16),
                   jax.ShapeDtypeStruct((m, HEADS * MLA_V), BF16)),
        grid=(m // TM_MLA,),
        in_specs=[_rows(D_MODEL, TM_MLA), _resident((1, D_MODEL)), _resident((D_MODEL, KV_LORA)),
                  _resident((D_MODEL, LANES)), _resident((1, KV_LORA)),
                  _resident((KV_LORA, HEADS * MLA_NOPE)), _resident((KV_LORA, HEADS * MLA_V)),
                  _resident((1, 2 * MLA_NOPE)), _resident((1, LANES)),
                  _rows(LANES, TM_MLA), _rows(LANES, TM_MLA), _rows(LANES, TM_MLA)],
        out_specs=(_rows(HEADS * MLA_QK_PAD, TM_MLA), _rows(HEADS * MLA_V, TM_MLA)),
        compiler_params=_params("parallel"),
        name="shared_kv",
    )(x, g, wdc, wdr, lat_g, wk, wv, kn_g, kr_g, cos, sa, sb)


Q_SCALE = (MLA_NOPE + MLA_ROPE) ** -0.5 * math.log2(math.e)
Q_NOPE_COLS = HEADS * MLA_NOPE


def _q_kernel(x_ref, g_ref, wdq_ref, lora_g_ref, wuq_ref, qn_g_ref, qr_g_ref,
              cos_ref, sa_ref, sb_ref, q_out):
    h = _rms(x_ref[...], g_ref[...], D_MODEL).astype(BF16)
    cq = _rms(_dot(h, wdq_ref[...]), lora_g_ref[...], Q_LORA).astype(BF16)
    q = _dot(cq, wuq_ref[...])
    mean_nope = _group_mean_matrix(2 * MLA_NOPE, MLA_NOPE)
    mean_rope = _group_mean_matrix(2 * LANES, MLA_ROPE)
    qn_g = qn_g_ref[...] * Q_SCALE
    qr_g = qr_g_ref[...] * Q_SCALE
    cos = cos_ref[...]
    sa = sa_ref[...]
    sb = sb_ref[...]
    for quad in range(HEADS // 4):
        lo = Q_NOPE_COLS + quad * 2 * LANES
        qp = _group_rms(q[:, lo:lo + 2 * LANES], mean_rope, qr_g)
        for half in range(2):
            pair = 2 * quad + half
            lo = pair * 2 * MLA_NOPE
            qn = _group_rms(q[:, lo:lo + 2 * MLA_NOPE], mean_nope, qn_g).astype(BF16)
            tile = _rope64x2(qp[:, half * LANES:(half + 1) * LANES], cos, sa, sb).astype(BF16)
            for sub in range(2):
                hd = 2 * pair + sub
                q_out[:, hd * MLA_QK_PAD:hd * MLA_QK_PAD + MLA_NOPE] = (
                    qn[:, sub * MLA_NOPE:(sub + 1) * MLA_NOPE])
                q_out[:, hd * MLA_QK_PAD + MLA_NOPE:(hd + 1) * MLA_QK_PAD] = (
                    _rope_tile_for_head(tile, sub))


def _mla_q(x, g, wdq, lora_g, wuq, qn_g, qr_g, cos, sa, sb):
    m = x.shape[0]
    return pl.pallas_call(
        _q_kernel,
        out_shape=jax.ShapeDtypeStruct((m, HEADS * MLA_QK_PAD), BF16),
        grid=(m // TM_MLA,),
        in_specs=[_rows(D_MODEL, TM_MLA), _resident((1, D_MODEL)), _resident((D_MODEL, Q_LORA)),
                  _resident((1, Q_LORA)), _resident(wuq.shape),
                  _resident((1, 2 * MLA_NOPE)), _resident((1, 2 * LANES)),
                  _rows(LANES, TM_MLA), _rows(LANES, TM_MLA), _rows(LANES, TM_MLA)],
        out_specs=_rows(HEADS * MLA_QK_PAD, TM_MLA),
        compiler_params=_params("parallel"),
        name="mla_q",
    )(x, g, wdq, lora_g, wuq, qn_g, qr_g, cos, sa, sb)


def _flash_kernel(q_ref, k_ref, v_ref, o_ref, m_ref, acc_ref):
    tiles = q_ref.shape[0] // TQ
    i = pl.program_id(2)
    pair = (i, tiles - 1 - i)
    m_ref[...] = jnp.full_like(m_ref, NEG_BIG)
    acc_ref[...] = jnp.zeros_like(acc_ref)

    def item(k):
        if k < tiles - 1:
            second = k >= i
            return (jnp.where(second, 1, 0), jnp.where(second, pair[1], pair[0]),
                    jnp.where(second, k - i, k), False)
        which = k - (tiles - 1)
        return which, pair[which], pair[which], True

    def scores(k):
        _, qt, kt, _ = item(k)
        rows = pl.ds(pl.multiple_of(qt * TQ, TQ), TQ)
        keys = pl.ds(pl.multiple_of(kt * TQ, TQ), TQ)
        return [_dot_nt(q_ref[rows, hd * MLA_QK_PAD:(hd + 1) * MLA_QK_PAD],
                        k_ref[keys, hd * MLA_QK_PAD:(hd + 1) * MLA_QK_PAD])
                for hd in range(HEADS_PER_STEP)]

    def consume(k, s_heads):
        which, _, kt, diagonal = item(k)
        keys = pl.ds(pl.multiple_of(kt * TQ, TQ), TQ)
        for hd, s in enumerate(s_heads):
            if diagonal:
                row = lax.broadcasted_iota(jnp.int32, (TQ, TQ), 0)
                col = lax.broadcasted_iota(jnp.int32, (TQ, TQ), 1)
                s = jnp.where(col <= row, s, NEG_BIG)
            m_prev = m_ref[which, hd]
            m_new = jnp.maximum(m_prev, jnp.max(s, axis=1, keepdims=True))
            alpha = jnp.exp2(m_prev - m_new)
            p = jnp.exp2(s - _lane_tile(m_new, TQ // LANES))
            v = v_ref[keys, hd * MLA_V:(hd + 1) * MLA_V]
            pv = _dot(p.astype(BF16), jnp.concatenate([v, jnp.ones_like(v)], axis=1))
            acc_ref[which, hd] = _lane_tile(alpha, V_PAD // LANES) * acc_ref[which, hd] + pv
            m_ref[which, hd] = m_new

    n_items = tiles + 1
    pending = [scores(k) for k in range(SCORES_AHEAD)]
    for k in range(n_items):
        if k + SCORES_AHEAD < n_items:
            pending.append(scores(k + SCORES_AHEAD))
        consume(k, pending.pop(0))

    for which in range(2):
        rows = pl.ds(pl.multiple_of(pair[which] * TQ, TQ), TQ)
        for hd in range(HEADS_PER_STEP):
            acc = acc_ref[which, hd]
            o_ref[rows, hd * MLA_V:(hd + 1) * MLA_V] = (acc[:, :MLA_V] / acc[:, MLA_V:]).astype(BF16)


def _flash(q, k, v, batch, seq):
    m = q.shape[0]
    g = HEADS_PER_STEP
    tiles = seq // TQ
    assert tiles % 2 == 0, "query tiles are processed in (i, tiles-1-i) pairs"
    return pl.pallas_call(
        _flash_kernel,
        out_shape=jax.ShapeDtypeStruct((m, HEADS * MLA_V), BF16),
        grid=(batch, HEADS // g, tiles // 2),
        in_specs=[
            pl.BlockSpec((seq, g * MLA_QK_PAD), lambda b, h, i: (b, h)),
            pl.BlockSpec((seq, g * MLA_QK_PAD), lambda b, h, i: (b, h)),
            pl.BlockSpec((seq, g * MLA_V), lambda b, h, i: (b, h)),
        ],
        out_specs=pl.BlockSpec((seq, g * MLA_V), lambda b, h, i: (b, h)),
        scratch_shapes=[pltpu.VMEM((2, g, TQ, LANES), F32),
                        pltpu.VMEM((2, g, TQ, V_PAD), F32)],
        compiler_params=_params("parallel", "parallel", "arbitrary"),
        name="flash_attention",
    )(q, k, v)


def _row(g, repeat=1):
    return jnp.tile(g.reshape(1, -1).astype(F32), (1, repeat))


def kernel(x, positions, norm_g, ffn_w_gate, ffn_w_up, ffn_w_down, ret_w_in, ret_gn_g, ret_w_o,
           kv_norm_g, kv_w_down, kv_latent_norm_g, kv_w_up, k_nope_norm_g, k_rope_norm_g,
           mla_w_dq, mla_q_lora_norm_g, mla_w_uq, mla_q_nope_norm_g, mla_q_rope_norm_g, mla_w_o):
    batch, seq, d = x.shape
    depth = norm_g.shape[0]
    n_self = ret_w_in.shape[0]
    m = batch * seq
    x = x.reshape(m, d)
    def ffn_casts(layer, half):
        return [(w, (layer, half)) for w in (ffn_w_gate, ffn_w_up, ffn_w_down)]

    def mixer_casts(layer):
        if layer < n_self:
            return [(ret_w_in, (layer,)), (ret_w_o, (layer,))]
        return [(mla_w_dq, (layer - n_self,)), (mla_w_o, (layer - n_self,))]

    tables, cast = _rope_tables(positions.reshape(m, 1), ffn_casts(0, 0) + mixer_casts(0))
    rcos, rsin, mcos, msa, msb = tables
    ffn_w, (w_in, w_o) = cast[:3], cast[3:]

    k_shared = v_shared = None
    for layer in range(depth):
        x, ffn_w = _ffn(x, _row(norm_g[layer, 0]), *ffn_w, casts=ffn_casts(layer, 1))
        g_mix = _row(norm_g[layer, 1])
        if layer < n_self:
            proj, kt = _ret_proj(x, g_mix, w_in, rcos, rsin)
            mix = dict(ret_mix=(_retention(proj, kt, batch, seq), proj, _row(ret_gn_g[layer]), w_o))
        else:
            j = layer - n_self
            wuq = mla_w_uq[j].reshape(Q_LORA, HEADS, MLA_NOPE + MLA_ROPE)
            wuq = jnp.concatenate([wuq[:, :, :MLA_NOPE].reshape(Q_LORA, Q_NOPE_COLS),
                                   wuq[:, :, MLA_NOPE:].reshape(Q_LORA, HEADS * MLA_ROPE)], axis=1)
            q = _mla_q(x, g_mix, w_in, _row(mla_q_lora_norm_g[j]),
                       wuq.astype(BF16), _row(mla_q_nope_norm_g[j], 2), _row(mla_q_rope_norm_g[j], 4),
                       mcos, msa, msb)
            mix = dict(mix=(_flash(q, k_shared, v_shared, batch, seq), w_o))
        following = ffn_casts(layer + 1, 0) + mixer_casts(layer + 1) if layer + 1 < depth else []
        x, cast = _ffn(x, _row(norm_g[layer, 2]), *ffn_w, casts=following, **mix)
        if following:
            ffn_w, (w_in, w_o) = cast[:3], cast[3:]
        if layer == n_self - 1:
            wup = kv_w_up.reshape(KV_LORA, HEADS, MLA_NOPE + MLA_V)
            wk = wup[:, :, :MLA_NOPE].reshape(KV_LORA, HEADS * MLA_NOPE).astype(BF16)
            wv = wup[:, :, MLA_NOPE:].reshape(KV_LORA, HEADS * MLA_V).astype(BF16)
            wdr = jnp.tile(kv_w_down[:, KV_LORA:], (1, 2)).astype(BF16)
            k_shared, v_shared = _shared_kv(
                x, _row(kv_norm_g), kv_w_down[:, :KV_LORA].astype(BF16), wdr,
                _row(kv_latent_norm_g), wk, wv, _row(k_nope_norm_g, 2), _row(k_rope_norm_g, 2),
                mcos, msa, msb)
    return x.reshape(batch, seq, d)
```

```python
import math

import jax
import jax.numpy as jnp
from jax import lax
from jax.experimental import pallas as pl
from jax.experimental.pallas import tpu as pltpu

F32 = jnp.float32
BF16 = jnp.bfloat16

LANES = 128
BF16_ROWS = 16

D_MODEL = 1024
D_FF = 2816
HEADS = 8
RET_DK = 128
RET_DV = 256
RET_QK = HEADS * RET_DK
RET_PROJ = 2 * RET_QK + 2 * HEADS * RET_DV
RET_V0 = HEADS * RET_DV
RET_Q0 = 2 * HEADS * RET_DV
MLA_NOPE = 128
MLA_ROPE = 64
MLA_V = 128
MLA_QK_PAD = 2 * LANES
V_PAD = 2 * LANES
HEADS_PER_STEP = 2
Q_LORA = 384
KV_LORA = 256
ROPE_BASE = 10000.0
EPS = 1e-6

VMEM_LIMIT = 56 * 1024 * 1024

TM = 512
TM_FFN = 1024
TM_MLA = 1024
FF_CHUNK = 256
RET_CHUNK = 256
RET_HEADS_PER_STEP = 2
TQ = 512
SCORES_AHEAD = 2
NEG_BIG = -1e30


def _dot(a, b):
    return jnp.dot(a, b, preferred_element_type=F32)


def _dot_nt(a, b):
    return lax.dot_general(a, b, (((1,), (1,)), ((), ())), preferred_element_type=F32)


def _rms(x, g, n):
    ms = jnp.sum(x * x, axis=-1, keepdims=True) * (1.0 / n)
    return x * lax.rsqrt(ms + EPS) * g


def _silu(x):
    return x / (1.0 + jnp.exp2(x * -math.log2(math.e)))


def _lane_tile(x, n):
    return jnp.concatenate([x] * n, axis=1)


def _group_mean_matrix(width, group):
    shift = group.bit_length() - 1
    r = lax.broadcasted_iota(jnp.int32, (width, width), 0) >> shift
    c = lax.broadcasted_iota(jnp.int32, (width, width), 1) >> shift
    return jnp.where(r == c, 1.0 / group, 0.0).astype(BF16)


def _group_rms(x, mean_matrix, g):
    ms = _dot((x * x).astype(BF16), mean_matrix)
    return x * lax.rsqrt(ms + EPS) * g


def _params(*sem):
    return pltpu.CompilerParams(dimension_semantics=sem, vmem_limit_bytes=VMEM_LIMIT)


def _resident(shape):
    nd = len(shape)
    return pl.BlockSpec(shape, lambda *_: (0,) * nd, pipeline_mode=pl.Buffered(1))


def _rows(width, tm=TM):
    return pl.BlockSpec((tm, width), lambda i: (i, 0))


class _RideAlongCasts:
    def __init__(self, casts, steps):
        self.sources = tuple(src for src, _ in casts)
        self.in_specs, self.out_shapes, self.out_specs = [], [], []
        for src, prefix in casts:
            rows, cols = src.shape[-2:]
            slab = next(s for s in range(BF16_ROWS, rows + 1, BF16_ROWS)
                        if rows % s == 0 and rows // s <= steps)
            last = rows // slab - 1
            lead = (None,) * len(prefix)
            self.in_specs.append(pl.BlockSpec(
                (*lead, slab, cols),
                lambda i, prefix=prefix, last=last: (*prefix, jnp.minimum(i, last), 0)))
            self.out_shapes.append(jax.ShapeDtypeStruct((rows, cols), BF16))
            self.out_specs.append(pl.BlockSpec(
                (slab, cols), lambda i, last=last: (jnp.minimum(i, last), 0)))

    def wrap(self, body, n_in, n_out):
        k = len(self.sources)

        def kernel(*refs):
            ins, refs = refs[:n_in], refs[n_in:]
            srcs, refs = refs[:k], refs[k:]
            outs, refs = refs[:n_out], refs[n_out:]
            for src, dst in zip(srcs, refs[:k]):
                dst[...] = src[...].astype(BF16)
            body(*ins, *outs, *refs[k:])
        return kernel


def _rope_tables_kernel(pos_ref, rcos_ref, rsin_ref, mcos_ref, msa_ref, msb_ref):
    pos = pos_ref[...].astype(F32)
    lane = lax.broadcasted_iota(jnp.int32, (1, LANES), 1)
    is_ret = lane < 64
    freq = jnp.where(is_ret, lane, lane & 31).astype(F32)
    step = jnp.where(is_ret, -2.0 / RET_DK, -2.0 / MLA_ROPE) * math.log(ROPE_BASE)
    ang = pos * jnp.exp(freq * step)
    c = jnp.cos(ang)
    s = jnp.sin(ang)
    rcos_ref[...] = jnp.where(is_ret, c, pltpu.roll(c, 64, 1))
    rsin_ref[...] = jnp.where(is_ret, -s, pltpu.roll(s, 64, 1))

    def spread(t):
        quarter = lane >> 5
        return jnp.where(quarter == 0, pltpu.roll(t, 64, 1),
                         jnp.where(quarter == 1, pltpu.roll(t, 96, 1),
                                   jnp.where(quarter == 2, t, pltpu.roll(t, 32, 1))))

    s = spread(s)
    first_half = (lane & 63) < 32
    mcos_ref[...] = spread(c)
    msa_ref[...] = jnp.where(first_half, -s, 0.0)
    msb_ref[...] = jnp.where(first_half, 0.0, s)


def _rope_tables(pos, casts):
    m = pos.shape[0]
    tm = 1024
    tab = jax.ShapeDtypeStruct((m, LANES), F32)
    spec = pl.BlockSpec((tm, LANES), lambda i: (i, 0))
    riders = _RideAlongCasts(casts, m // tm)
    out = pl.pallas_call(
        riders.wrap(_rope_tables_kernel, n_in=1, n_out=5),
        out_shape=[tab] * 5 + riders.out_shapes,
        grid=(m // tm,),
        in_specs=[pl.BlockSpec((tm, 1), lambda i: (i, 0)), *riders.in_specs],
        out_specs=[spec] * 5 + riders.out_specs,
        compiler_params=_params("arbitrary"),
        name="rope_tables",
    )(pos, *riders.sources)
    return out[:5], out[5:]


def _rope128(x, cos, sin):
    return x * cos + pltpu.roll(x, 64, 1) * sin


def _rope64x2(x, cos, sa, sb):
    return x * cos + pltpu.roll(x, 96, 1) * sa + pltpu.roll(x, 32, 1) * sb


def _ffn_body(x, g_ref, wg_ref, wu_ref, wd_ref, o_ref):
    h = _rms(x, g_ref[...], D_MODEL).astype(BF16)
    hidden = []
    for c in range(D_FF // FF_CHUNK):
        sl = slice(c * FF_CHUNK, (c + 1) * FF_CHUNK)
        hidden.append((_silu(_dot(h, wg_ref[:, sl])) * _dot(h, wu_ref[:, sl])).astype(BF16))
    o_ref[...] = x + 0.5 * _dot(jnp.concatenate(hidden, axis=1), wd_ref[...])


def _ffn_kernel(x_ref, g_ref, wg_ref, wu_ref, wd_ref, o_ref):
    _ffn_body(x_ref[...], g_ref, wg_ref, wu_ref, wd_ref, o_ref)


def _mix_ffn_kernel(x_ref, a_ref, wo_ref, g_ref, wg_ref, wu_ref, wd_ref, o_ref):
    _ffn_body(x_ref[...] + _dot(a_ref[...], wo_ref[...]), g_ref, wg_ref, wu_ref, wd_ref, o_ref)


def _ret_mix_ffn_kernel(x_ref, ret_ref, gate_ref, gn_ref, wo_ref, g_ref, wg_ref, wu_ref, wd_ref,
                        o_ref):
    x = x_ref[...]
    for hd in range(HEADS):
        cols = slice(hd * RET_DV, (hd + 1) * RET_DV)
        o = ret_ref[:, cols].astype(F32)
        oc = o - jnp.mean(o, axis=-1, keepdims=True)
        var = jnp.mean(oc * oc, axis=-1, keepdims=True)
        on = oc * lax.rsqrt(var + EPS) * gn_ref[:, cols]
        a = (_silu(gate_ref[:, cols].astype(F32)) * on).astype(BF16)
        x = x + _dot(a, wo_ref[cols, :])
    _ffn_body(x, g_ref, wg_ref, wu_ref, wd_ref, o_ref)


def _ffn(x, g, wg, wu, wd, mix=None, ret_mix=None, casts=()):
    m = x.shape[0]
    w_specs = [_resident((1, D_MODEL)), _resident((D_MODEL, D_FF)), _resident((D_MODEL, D_FF)),
               _resident((D_FF, D_MODEL))]
    if ret_mix is not None:
        o, proj, gn, wo = ret_mix
        tm = TM
        name, body, args = "ret_mix_ffn", _ret_mix_ffn_kernel, (x, o, proj, gn, wo)
        specs = [_rows(D_MODEL, tm), _rows(HEADS * RET_DV, tm), _rows(HEADS * RET_DV, tm),
                 _resident(gn.shape), _resident(wo.shape)]
    elif mix is not None:
        a, wo = mix
        tm = TM_FFN
        name, body, args = "mix_ffn", _mix_ffn_kernel, (x, a, wo)
        specs = [_rows(D_MODEL, tm), _rows(a.shape[1], tm), _resident(wo.shape)]
    else:
        tm = TM_FFN
        name, body, args, specs = "ffn", _ffn_kernel, (x,), [_rows(D_MODEL, tm)]
    steps = m // tm
    args = (*args, g, wg, wu, wd)
    riders = _RideAlongCasts(casts, steps)
    out = pl.pallas_call(
        riders.wrap(body, n_in=len(args), n_out=1),
        out_shape=[jax.ShapeDtypeStruct((m, D_MODEL), F32)] + riders.out_shapes,
        grid=(steps,),
        in_specs=[*specs, *w_specs, *riders.in_specs],
        out_specs=[_rows(D_MODEL, tm)] + riders.out_specs,
        compiler_params=_params("arbitrary"),
        name=name,
    )(*args, *riders.sources)
    return out[0], out[1:]


def _ret_proj_kernel(x_ref, g_ref, w_ref, cos_ref, sin_ref, o_ref, kt_ref):
    h = _rms(x_ref[...], g_ref[...], D_MODEL).astype(BF16)
    cos = cos_ref[...]
    sin = sin_ref[...]
    q = _dot(h, w_ref[:, :RET_QK])
    k = _dot(h, w_ref[:, RET_QK:2 * RET_QK])
    for hd in range(HEADS):
        head = slice(hd * RET_DK, (hd + 1) * RET_DK)
        o_ref[:, RET_Q0 + hd * RET_DK:RET_Q0 + (hd + 1) * RET_DK] = (
            _rope128(q[:, head], cos, sin).astype(BF16))
        kh = _rope128(k[:, head], cos, sin) * RET_DK ** -0.5
        for c in range(TM // RET_CHUNK):
            kt_ref[c, head, :] = kh[c * RET_CHUNK:(c + 1) * RET_CHUNK, :].T.astype(BF16)
    v0 = 2 * RET_QK
    g0 = v0 + HEADS * RET_DV
    o_ref[:, RET_V0:RET_Q0] = _dot(h, w_ref[:, v0:g0]).astype(BF16)
    o_ref[:, :RET_V0] = _dot(h, w_ref[:, g0:]).astype(BF16)


def _ret_proj(x, g, w, cos, sin):
    m = x.shape[0]
    chunks = TM // RET_CHUNK
    return pl.pallas_call(
        _ret_proj_kernel,
        out_shape=(jax.ShapeDtypeStruct((m, RET_PROJ - RET_QK), BF16),
                   jax.ShapeDtypeStruct((m // RET_CHUNK, RET_QK, RET_CHUNK), BF16)),
        grid=(m // TM,),
        in_specs=[_rows(D_MODEL), _resident((1, D_MODEL)), _resident((D_MODEL, RET_PROJ)),
                  _rows(LANES), _rows(LANES)],
        out_specs=(_rows(RET_PROJ - RET_QK),
                   pl.BlockSpec((chunks, RET_QK, RET_CHUNK), lambda i: (i, 0, 0))),
        compiler_params=_params("parallel"),
        name="ret_proj",
    )(x, g, w, cos, sin)


def _retention_kernel(q_ref, kt_ref, v_ref, o_ref, state_ref):
    c = RET_CHUNK
    seq = q_ref.shape[0]

    def index(shape, axis):
        return lax.broadcasted_iota(jnp.int32, shape, axis).astype(F32)

    def decays(sub):
        hd = (pl.program_id(1) * RET_HEADS_PER_STEP + sub).astype(F32)

        def log_decay(shape):
            return jnp.log1p(-jnp.exp2(jnp.full(shape, -5.0, F32) - hd))

        diff = index((c, c), 0) - index((c, c), 1)
        d_intra = jnp.where(diff >= 0, jnp.exp(log_decay((c, c)) * jnp.maximum(diff, 0.0)), 0.0)
        q_decay = jnp.exp(log_decay((c, RET_DV)) * (index((c, RET_DV), 0) + 1.0))
        k_decay = jnp.exp(log_decay((RET_DK, c)) * (c - 1.0 - index((RET_DK, c), 1)))
        chunk_decay = jnp.exp(log_decay((RET_DK, RET_DV)) * c)
        return d_intra, q_decay, k_decay, chunk_decay

    per_head = [decays(sub) for sub in range(RET_HEADS_PER_STEP)]
    state_ref[...] = jnp.zeros_like(state_ref)

    def body(t, carry):
        off = pl.multiple_of(t * c, c)
        for sub, (d_intra, q_decay, k_decay, chunk_decay) in enumerate(per_head):
            qk = slice(sub * RET_DK, (sub + 1) * RET_DK)
            vo = slice(sub * RET_DV, (sub + 1) * RET_DV)
            q = q_ref[pl.ds(off, c), qk]
            kt = kt_ref[t, qk, :]
            v = v_ref[pl.ds(off, c), vo]
            state = state_ref[sub]
            s = _dot(q, kt) * d_intra
            o = _dot(s.astype(BF16), v) + _dot(q, state.astype(BF16)) * q_decay
            ktd = (kt.astype(F32) * k_decay).astype(BF16)
            state_ref[sub] = state * chunk_decay + _dot(ktd, v)
            o_ref[pl.ds(off, c), vo] = o.astype(BF16)
        return carry

    lax.fori_loop(0, seq // c, body, 0, unroll=True)


def _retention(proj, kt, batch, seq):
    m = proj.shape[0]
    g = RET_HEADS_PER_STEP
    qblk = RET_Q0 // (g * RET_DK)
    vblk = RET_V0 // (g * RET_DV)
    return pl.pallas_call(
        _retention_kernel,
        out_shape=jax.ShapeDtypeStruct((m, HEADS * RET_DV), BF16),
        grid=(batch, HEADS // g),
        in_specs=[
            pl.BlockSpec((seq, g * RET_DK), lambda b, h: (b, qblk + h)),
            pl.BlockSpec((seq // RET_CHUNK, g * RET_DK, RET_CHUNK), lambda b, h: (b, h, 0)),
            pl.BlockSpec((seq, g * RET_DV), lambda b, h: (b, vblk + h)),
        ],
        out_specs=pl.BlockSpec((seq, g * RET_DV), lambda b, h: (b, h)),
        scratch_shapes=[pltpu.VMEM((g, RET_DK, RET_DV), F32)],
        compiler_params=_params("parallel", "parallel"),
        name="retention",
    )(proj, kt, proj)


def _rope_tile_for_head(tile, hd):
    lane = lax.broadcasted_iota(jnp.int32, tile.shape, 1)
    keep = (lane < MLA_ROPE) if hd % 2 == 0 else (lane >= MLA_ROPE)
    return jnp.where(keep, tile, jnp.zeros_like(tile))


def _kv_kernel(x_ref, g_ref, wdc_ref, wdr_ref, lat_g_ref, wk_ref, wv_ref, kn_g_ref, kr_g_ref,
               cos_ref, sa_ref, sb_ref, k_out, v_out):
    h = _rms(x_ref[...], g_ref[...], D_MODEL).astype(BF16)
    lat = _rms(_dot(h, wdc_ref[...]), lat_g_ref[...], KV_LORA).astype(BF16)
    pe = _rms(_dot(h, wdr_ref[...]), kr_g_ref[...], LANES)
    pe = _rope64x2(pe, cos_ref[...], sa_ref[...], sb_ref[...]).astype(BF16)
    pe_tiles = (_rope_tile_for_head(pe, 0), _rope_tile_for_head(pe, 1))
    mean_nope = _group_mean_matrix(2 * MLA_NOPE, MLA_NOPE)
    kn_g = kn_g_ref[...]
    for pair in range(HEADS // 2):
        lo = pair * 2 * MLA_NOPE
        kn = _group_rms(_dot(lat, wk_ref[:, lo:lo + 2 * MLA_NOPE]), mean_nope, kn_g)
        kn = kn.astype(BF16)
        cols = slice(pair * 2 * MLA_V, (pair + 1) * 2 * MLA_V)
        v_out[:, cols] = _dot(lat, wv_ref[:, cols]).astype(BF16)
        for sub in range(2):
            hd = 2 * pair + sub
            k_out[:, hd * MLA_QK_PAD:hd * MLA_QK_PAD + MLA_NOPE] = (
                kn[:, sub * MLA_NOPE:(sub + 1) * MLA_NOPE])
            k_out[:, hd * MLA_QK_PAD + MLA_NOPE:(hd + 1) * MLA_QK_PAD] = pe_tiles[sub]


def _shared_kv(x, g, wdc, wdr, lat_g, wk, wv, kn_g, kr_g, cos, sa, sb):
    m = x.shape[0]
    return pl.pallas_call(
        _kv_kernel,
        out_shape=(jax.ShapeDtypeStruct((m, HEADS * MLA_QK_PAD), BF16),
                   jax.ShapeDtypeStruct((m, HEADS * MLA_V), BF16)),
        grid=(m // TM_MLA,),
        in_specs=[_rows(D_MODEL, TM_MLA), _resident((1, D_MODEL)), _resident((D_MODEL, KV_LORA)),
                  _resident((D_MODEL, LANES)), _resident((1, KV_LORA)),
                  _resident((KV_LORA, HEADS * MLA_NOPE)), _resident((KV_LORA, HEADS * MLA_V)),
                  _resident((1, 2 * MLA_NOPE)), _resident((1, LANES)),
                  _rows(LANES, TM_MLA), _rows(LANES, TM_MLA), _rows(LANES, TM_MLA)],
        out_specs=(_rows(HEADS * MLA_QK_PAD, TM_MLA), _rows(HEADS * MLA_V, TM_MLA)),
        compiler_params=_params("parallel"),
        name="shared_kv",
    )(x, g, wdc, wdr, lat_g, wk, wv, kn_g, kr_g, cos, sa, sb)


Q_SCALE = (MLA_NOPE + MLA_ROPE) ** -0.5 * math.log2(math.e)
Q_NOPE_COLS = HEADS * MLA_NOPE


def _q_kernel(x_ref, g_ref, wdq_ref, lora_g_ref, wuq_ref, qn_g_ref, qr_g_ref,
              cos_ref, sa_ref, sb_ref, q_out):
    h = _rms(x_ref[...], g_ref[...], D_MODEL).astype(BF16)
    cq = _rms(_dot(h, wdq_ref[...]), lora_g_ref[...], Q_LORA).astype(BF16)
    q = _dot(cq, wuq_ref[...])
    mean_nope = _group_mean_matrix(2 * MLA_NOPE, MLA_NOPE)
    mean_rope = _group_mean_matrix(2 * LANES, MLA_ROPE)
    qn_g = qn_g_ref[...] * Q_SCALE
    qr_g = qr_g_ref[...] * Q_SCALE
    cos = cos_ref[...]
    sa = sa_ref[...]
    sb = sb_ref[...]
    for quad in range(HEADS // 4):
        lo = Q_NOPE_COLS + quad * 2 * LANES
        qp = _group_rms(q[:, lo:lo + 2 * LANES], mean_rope, qr_g)
        for half in range(2):
            pair = 2 * quad + half
            lo = pair * 2 * MLA_NOPE
            qn = _group_rms(q[:, lo:lo + 2 * MLA_NOPE], mean_nope, qn_g).astype(BF16)
            tile = _rope64x2(qp[:, half * LANES:(half + 1) * LANES], cos, sa, sb).astype(BF16)
            for sub in range(2):
                hd = 2 * pair + sub
                q_out[:, hd * MLA_QK_PAD:hd * MLA_QK_PAD + MLA_NOPE] = (
                    qn[:, sub * MLA_NOPE:(sub + 1) * MLA_NOPE])
                q_out[:, hd * MLA_QK_PAD + MLA_NOPE:(hd + 1) * MLA_QK_PAD] = (
                    _rope_tile_for_head(tile, sub))


def _mla_q(x, g, wdq, lora_g, wuq, qn_g, qr_g, cos, sa, sb):
    m = x.shape[0]
    return pl.pallas_call(
        _q_kernel,
        out_shape=jax.ShapeDtypeStruct((m, HEADS * MLA_QK_PAD), BF16),
        grid=(m // TM_MLA,),
        in_specs=[_rows(D_MODEL, TM_MLA), _resident((1, D_MODEL)), _resident((D_MODEL, Q_LORA)),
                  _resident((1, Q_LORA)), _resident(wuq.shape),
                  _resident((1, 2 * MLA_NOPE)), _resident((1, 2 * LANES)),
                  _rows(LANES, TM_MLA), _rows(LANES, TM_MLA), _rows(LANES, TM_MLA)],
        out_specs=_rows(HEADS * MLA_QK_PAD, TM_MLA),
        compiler_params=_params("parallel"),
        name="mla_q",
    )(x, g, wdq, lora_g, wuq, qn_g, qr_g, cos, sa, sb)


def _flash_kernel(q_ref, k_ref, v_ref, o_ref, m_ref, acc_ref):
    tiles = q_ref.shape[0] // TQ
    i = pl.program_id(2)
    pair = (i, tiles - 1 - i)
    m_ref[...] = jnp.full_like(m_ref, NEG_BIG)
    acc_ref[...] = jnp.zeros_like(acc_ref)

    half = TQ // 2
    items = []
    for k in range(tiles - 1):
        second = k >= i
        items.append((jnp.where(second, 1, 0), jnp.where(second, pair[1], pair[0]),
                      jnp.where(second, k - i, k), slice(0, TQ), TQ, False))
    for which in range(2):
        items.append((which, pair[which], pair[which], slice(0, half), half, True))
        items.append((which, pair[which], pair[which], slice(half, TQ), TQ, True))

    def scores(item):
        _, qt, kt, rows, n_keys, _ = item
        q_rows = pl.ds(pl.multiple_of(qt * TQ + rows.start, half), rows.stop - rows.start)
        keys = pl.ds(pl.multiple_of(kt * TQ, TQ), n_keys)
        return [_dot_nt(q_ref[q_rows, hd * MLA_QK_PAD:(hd + 1) * MLA_QK_PAD],
                        k_ref[keys, hd * MLA_QK_PAD:(hd + 1) * MLA_QK_PAD])
                for hd in range(HEADS_PER_STEP)]

    def consume(item, s_heads):
        which, _, kt, rows, n_keys, causal = item
        keys = pl.ds(pl.multiple_of(kt * TQ, TQ), n_keys)
        for hd, s in enumerate(s_heads):
            if causal:
                row = lax.broadcasted_iota(jnp.int32, s.shape, 0) + rows.start
                col = lax.broadcasted_iota(jnp.int32, s.shape, 1)
                s = jnp.where(col <= row, s, NEG_BIG)
            m_prev = m_ref[which, hd, rows]
            m_new = jnp.maximum(m_prev, jnp.max(s, axis=1, keepdims=True))
            alpha = jnp.exp2(m_prev - m_new)
            p = jnp.exp2(s - _lane_tile(m_new, n_keys // LANES))
            v = v_ref[keys, hd * MLA_V:(hd + 1) * MLA_V]
            pv = _dot(p.astype(BF16), jnp.concatenate([v, jnp.ones_like(v)], axis=1))
            acc_ref[which, hd, rows] = (_lane_tile(alpha, V_PAD // LANES) * acc_ref[which, hd, rows]
                                        + pv)
            m_ref[which, hd, rows] = m_new

    pending = [scores(item) for item in items[:SCORES_AHEAD]]
    for k, item in enumerate(items):
        if k + SCORES_AHEAD < len(items):
            pending.append(scores(items[k + SCORES_AHEAD]))
        consume(item, pending.pop(0))

    for which in range(2):
        rows = pl.ds(pl.multiple_of(pair[which] * TQ, TQ), TQ)
        for hd in range(HEADS_PER_STEP):
            acc = acc_ref[which, hd]
            o_ref[rows, hd * MLA_V:(hd + 1) * MLA_V] = (acc[:, :MLA_V] / acc[:, MLA_V:]).astype(BF16)


def _flash(q, k, v, batch, seq):
    m = q.shape[0]
    g = HEADS_PER_STEP
    tiles = seq // TQ
    assert tiles % 2 == 0, "query tiles are processed in (i, tiles-1-i) pairs"
    return pl.pallas_call(
        _flash_kernel,
        out_shape=jax.ShapeDtypeStruct((m, HEADS * MLA_V), BF16),
        grid=(batch, HEADS // g, tiles // 2),
        in_specs=[
            pl.BlockSpec((seq, g * MLA_QK_PAD), lambda b, h, i: (b, h)),
            pl.BlockSpec((seq, g * MLA_QK_PAD), lambda b, h, i: (b, h)),
            pl.BlockSpec((seq, g * MLA_V), lambda b, h, i: (b, h)),
        ],
        out_specs=pl.BlockSpec((seq, g * MLA_V), lambda b, h, i: (b, h)),
        scratch_shapes=[pltpu.VMEM((2, g, TQ, LANES), F32),
                        pltpu.VMEM((2, g, TQ, V_PAD), F32)],
        compiler_params=_params("parallel", "parallel", "arbitrary"),
        name="flash_attention",
    )(q, k, v)


def _row(g, repeat=1):
    return jnp.tile(g.reshape(1, -1).astype(F32), (1, repeat))


def kernel(x, positions, norm_g, ffn_w_gate, ffn_w_up, ffn_w_down, ret_w_in, ret_gn_g, ret_w_o,
           kv_norm_g, kv_w_down, kv_latent_norm_g, kv_w_up, k_nope_norm_g, k_rope_norm_g,
           mla_w_dq, mla_q_lora_norm_g, mla_w_uq, mla_q_nope_norm_g, mla_q_rope_norm_g, mla_w_o):
    batch, seq, d = x.shape
    depth = norm_g.shape[0]
    n_self = ret_w_in.shape[0]
    m = batch * seq
    x = x.reshape(m, d)
    def ffn_casts(layer, half):
        return [(w, (layer, half)) for w in (ffn_w_gate, ffn_w_up, ffn_w_down)]

    def mixer_casts(layer):
        if layer < n_self:
            return [(ret_w_in, (layer,)), (ret_w_o, (layer,))]
        return [(mla_w_dq, (layer - n_self,)), (mla_w_o, (layer - n_self,))]

    tables, cast = _rope_tables(positions.reshape(m, 1), ffn_casts(0, 0) + mixer_casts(0))
    rcos, rsin, mcos, msa, msb = tables
    ffn_w, (w_in, w_o) = cast[:3], cast[3:]

    k_shared = v_shared = None
    for layer in range(depth):
        x, ffn_w = _ffn(x, _row(norm_g[layer, 0]), *ffn_w, casts=ffn_casts(layer, 1))
        g_mix = _row(norm_g[layer, 1])
        if layer < n_self:
            proj, kt = _ret_proj(x, g_mix, w_in, rcos, rsin)
            mix = dict(ret_mix=(_retention(proj, kt, batch, seq), proj, _row(ret_gn_g[layer]), w_o))
        else:
            j = layer - n_self
            wuq = mla_w_uq[j].reshape(Q_LORA, HEADS, MLA_NOPE + MLA_ROPE)
            wuq = jnp.concatenate([wuq[:, :, :MLA_NOPE].reshape(Q_LORA, Q_NOPE_COLS),
                                   wuq[:, :, MLA_NOPE:].reshape(Q_LORA, HEADS * MLA_ROPE)], axis=1)
            q = _mla_q(x, g_mix, w_in, _row(mla_q_lora_norm_g[j]),
                       wuq.astype(BF16), _row(mla_q_nope_norm_g[j], 2), _row(mla_q_rope_norm_g[j], 4),
                       mcos, msa, msb)
            mix = dict(mix=(_flash(q, k_shared, v_shared, batch, seq), w_o))
        following = ffn_casts(layer + 1, 0) + mixer_casts(layer + 1) if layer + 1 < depth else []
        x, cast = _ffn(x, _row(norm_g[layer, 2]), *ffn_w, casts=following, **mix)
        if following:
            ffn_w, (w_in, w_o) = cast[:3], cast[3:]
        if layer == n_self - 1:
            wup = kv_w_up.reshape(KV_LORA, HEADS, MLA_NOPE + MLA_V)
            wk = wup[:, :, :MLA_NOPE].reshape(KV_LORA, HEADS * MLA_NOPE).astype(BF16)
            wv = wup[:, :, MLA_NOPE:].reshape(KV_LORA, HEADS * MLA_V).astype(BF16)
            wdr = jnp.tile(kv_w_down[:, KV_LORA:], (1, 2)).astype(BF16)
            k_shared, v_shared = _shared_kv(
                x, _row(kv_norm_g), kv_w_down[:, :KV_LORA].astype(BF16), wdr,
                _row(kv_latent_norm_g), wk, wv, _row(k_nope_norm_g, 2), _row(k_rope_norm_g, 2),
                mcos, msa, msb)
    return x.reshape(batch, seq, d)
```
